```python
import jax
import jax.numpy as jnp
from jax import lax
import numpy as np

D_MODEL = 1024
BATCH = 16
SEQ = 2048
DEPTH = 2

GRID_W = 64
CTX_LEN = 256
HEAD_DIM = 64
GLA_HEADS = 4
GLA_DK = 32
GLA_DV = 64
GLA_QK = GLA_HEADS * GLA_DK
GLA_V = GLA_HEADS * GLA_DV
GLA_GATE_RANK = 16
GLA_GATE_TAU = 16.0
GLA_CHUNK = 64
GMLP_GROUPS = 4
GMLP_GDIM = 64
GMLP_WIDTH = GMLP_GROUPS * GMLP_GDIM
GMLP_CHUNK = 128
SWA_HEADS = 8
SWA_KV_HEADS = 2
SWA_REP = SWA_HEADS // SWA_KV_HEADS
SWA_Q = SWA_HEADS * HEAD_DIM
SWA_KV = SWA_KV_HEADS * HEAD_DIM
SWA_WINDOW = 128
ROPE_AXIS_DIM = HEAD_DIM // 2
ROPE_THETA = 10000.0
MIX_WIDTH = GLA_V + GMLP_WIDTH + SWA_Q
IN_SPLITS = (GLA_QK, GLA_QK, GLA_V, GLA_V, 2 * GLA_GATE_RANK, 2 * GMLP_WIDTH, SWA_Q, SWA_KV, SWA_KV)
IN_COLS = 2 * GLA_QK + 2 * GLA_V + 2 * GLA_GATE_RANK + 2 * GMLP_WIDTH + SWA_Q + 2 * SWA_KV
D_FF = -(-8 * D_MODEL // (3 * 256)) * 256

kernel_name = 'hybrid_dit_gla_gmlp_swa'


def rmsnorm(x, g, eps=1e-6):
    xf = x.astype(jnp.float32)
    y = xf * lax.rsqrt(jnp.mean(xf * xf, axis=-1, keepdims=True) + eps)
    return (y * g.astype(jnp.float32)).astype(x.dtype)


def layernorm(x, g, b, eps=1e-5):
    xf = x.astype(jnp.float32)
    mu = jnp.mean(xf, axis=-1, keepdims=True)
    xc = xf - mu
    y = xc * lax.rsqrt(jnp.mean(xc * xc, axis=-1, keepdims=True) + eps)
    return y * g.astype(jnp.float32) + b.astype(jnp.float32)


def modulate(h, shift, scale):
    return h * (1.0 + scale) + shift


def split_cols(z):
    offsets = np.cumsum(np.array(IN_SPLITS))[:-1].tolist()
    return jnp.split(z, offsets, axis=-1)


def rope_axis(x, ang):
    half = x.shape[-1] // 2
    cos = jnp.cos(ang)[:, None, :]
    sin = jnp.sin(ang)[:, None, :]
    x1, x2 = x[..., :half], x[..., half:]
    return jnp.concatenate([x1 * cos - x2 * sin, x2 * cos + x1 * sin], axis=-1)


def rope_2d(x, ang_row, ang_col):
    return jnp.concatenate([rope_axis(x[..., :ROPE_AXIS_DIM], ang_row),
                            rope_axis(x[..., ROPE_AXIS_DIM:], ang_col)], axis=-1)


def gla_heads(a, d):
    b, t = a.shape[:2]
    return a.astype(jnp.float32).reshape(b, t, GLA_HEADS, d).transpose(0, 2, 1, 3)


def gla_log_gates(code, wa2, ba):
    z = code @ wa2.astype(jnp.float32) + ba.astype(jnp.float32)
    return gla_heads(jax.nn.log_sigmoid(z) / GLA_GATE_TAU, GLA_DK)


def gla_chunk_scan(q, k, v, logg, s0):
    b, h, t, _ = q.shape
    n = t // GLA_CHUNK

    def chunks(a):
        return a.reshape(b, h, n, GLA_CHUNK, a.shape[-1]).transpose(2, 0, 1, 3, 4)

    causal = jnp.tril(jnp.ones((GLA_CHUNK, GLA_CHUNK), dtype=bool))[:, :, None]

    def step(s, inp):
        qc, kc, vc, gc = inp
        cum = jnp.cumsum(gc, axis=2)
        diff = cum[:, :, :, None, :] - cum[:, :, None, :, :]
        decay = jnp.exp(jnp.where(causal, diff, -jnp.inf))
        att = jnp.einsum('bhtd,bhsd,bhtsd->bhts', qc, kc, decay)
        o = att @ vc + jnp.einsum('bhtd,bhde->bhte', qc * jnp.exp(cum), s)
        cum_end = cum[:, :, -1:, :]
        s = jnp.exp(cum_end[:, :, 0, :])[..., None] * s + jnp.einsum(
            'bhsd,bhse->bhde', kc * jnp.exp(cum_end - cum), vc)
        return s, o

    s, o = lax.scan(step, s0, (chunks(q), chunks(k), chunks(v), chunks(logg)))
    return o.transpose(1, 2, 0, 3, 4).reshape(b, h, t, v.shape[-1]), s


def gla_mixer(zl, zc, wa2, ba, norm_g, need_ctx):
    def prep(q, k, v, code):
        cf, cb = jnp.split(code.astype(jnp.float32), 2, axis=-1)
        return (gla_heads(q, GLA_DK) * GLA_DK ** -0.5, gla_heads(k, GLA_DK), gla_heads(v, GLA_DV),
                gla_log_gates(cf, wa2[0], ba[0]), gla_log_gates(cb, wa2[1], ba[1]))

    lq, lk, lv, lgf, lgb = prep(zl[0], zl[1], zl[2], zl[4])
    cq, ck, cv, cgf, cgb = prep(zc[0], zc[1], zc[2], zc[4])
    s0 = jnp.zeros((lq.shape[0], GLA_HEADS, GLA_DK, GLA_DV), jnp.float32)
    flip = lambda a: a[:, :, ::-1]
    oc_f, s_f = gla_chunk_scan(cq, ck, cv, cgf, s0)
    oc_b, s_b = gla_chunk_scan(flip(cq), flip(ck), flip(cv), flip(cgb), s0)
    ol_f, _ = gla_chunk_scan(lq, lk, lv, lgf, s_f)
    ol_b, _ = gla_chunk_scan(flip(lq), flip(lk), flip(lv), flip(lgb), s_b)

    def finish(o, g):
        bb, _, t, _ = o.shape
        o = rmsnorm(o.transpose(0, 2, 1, 3), norm_g)
        g = g.astype(jnp.float32).reshape(bb, t, GLA_HEADS, GLA_DV)
        return (o * jax.nn.silu(g)).reshape(bb, t, GLA_V)

    y_lat = finish(ol_f + flip(ol_b), zl[3])
    y_ctx = finish(oc_f + flip(oc_b), zc[3]) if need_ctx else None
    return y_lat, y_ctx


def gmlp_mixer(z, ln_g, ln_b, ws, bs):
    zf = jax.nn.gelu(z.astype(jnp.float32), approximate=False)
    u, v = jnp.split(zf, 2, axis=-1)
    v = layernorm(v, ln_g, ln_b)
    b, t, _ = v.shape
    vb = v.reshape(b, t // GMLP_CHUNK, GMLP_CHUNK, GMLP_GROUPS, GMLP_GDIM)
    mixed = jnp.einsum('gpq,bnqgc->bnpgc', ws.astype(jnp.float32), vb) + bs.astype(jnp.float32).T[:, :, None]
    return u * mixed.reshape(b, t, GMLP_WIDTH)


def band_blocks(a):
    b, t = a.shape[:2]
    w = SWA_WINDOW
    ap = jnp.pad(a, ((0, 0), (w, w), (0, 0), (0, 0)))
    blk = ap.reshape(b, t // w + 2, w, *a.shape[2:])
    return jnp.concatenate([blk[:, :-2], blk[:, 1:-1], blk[:, 2:]], axis=2)


def swa_mixer(zl, zc, ang_row, ang_col, sink, need_ctx):
    f32 = jnp.float32
    w = SWA_WINDOW
    scale = HEAD_DIM ** -0.5
    q_l, k_l, v_l = [a.astype(f32) for a in zl]
    b, t = q_l.shape[:2]
    nb = t // w
    q_l = rope_2d(q_l.reshape(b, t, SWA_HEADS, HEAD_DIM), ang_row, ang_col) * scale
    k_l = rope_2d(k_l.reshape(b, t, SWA_KV_HEADS, HEAD_DIM), ang_row, ang_col)
    v_l = v_l.reshape(b, t, SWA_KV_HEADS, HEAD_DIM)
    n_ctx = zc[0].shape[1]
    k_c = zc[1].astype(f32).reshape(b, n_ctx, SWA_KV_HEADS, HEAD_DIM)
    v_c = zc[2].astype(f32).reshape(b, n_ctx, SWA_KV_HEADS, HEAD_DIM)
    sink = sink.astype(f32).reshape(SWA_KV_HEADS, SWA_REP)

    qb = q_l.reshape(b, nb, w, SWA_KV_HEADS, SWA_REP, HEAD_DIM)
    kb, vb = band_blocks(k_l), band_blocks(v_l)
    blk = jnp.arange(nb)[:, None, None]
    t_pos = blk * w + jnp.arange(w)[None, :, None]
    s_pos = blk * w + jnp.arange(3 * w)[None, None, :] - w
    valid = (s_pos >= 0) & (s_pos < t) & (jnp.abs(t_pos - s_pos) <= w)
    s_loc = jnp.where(valid[None, :, None, None], jnp.einsum('bnqgrd,bnkgd->bngrqk', qb, kb), -jnp.inf)
    s_ctx = jnp.einsum('bnqgrd,bkgd->bngrqk', qb, k_c)
    sink_l = sink[None, None, :, :, None, None]
    m = jnp.maximum(jnp.maximum(s_loc.max(-1, keepdims=True), s_ctx.max(-1, keepdims=True)), sink_l)
    p_loc = jnp.exp(s_loc - m)
    p_ctx = jnp.exp(s_ctx - m)
    inv = 1.0 / (p_loc.sum(-1, keepdims=True) + p_ctx.sum(-1, keepdims=True) + jnp.exp(sink_l - m))
    o = (jnp.einsum('bngrqk,bnkgd->bnqgrd', p_loc * inv, vb)
         + jnp.einsum('bngrqk,bkgd->bnqgrd', p_ctx * inv, v_c))
    y_lat = o.reshape(b, t, SWA_Q)
    if not need_ctx:
        return y_lat, None
    q_c = zc[0].astype(f32).reshape(b, n_ctx, SWA_KV_HEADS, SWA_REP, HEAD_DIM) * scale
    s = jnp.einsum('bqgrd,bkgd->bgrqk', q_c, k_c)
    sink_c = sink[None, :, :, None, None]
    mc = jnp.maximum(s.max(-1, keepdims=True), sink_c)
    p = jnp.exp(s - mc)
    p = p / (p.sum(-1, keepdims=True) + jnp.exp(sink_c - mc))
    y_ctx = jnp.einsum('bgrqk,bkgd->bqgrd', p, v_c).reshape(b, n_ctx, SWA_Q)
    return y_lat, y_ctx


def hybrid_mixer(h_lat, h_ctx, ang_row, ang_col, w_in, w_out, gla_wa2, gla_ba, gla_norm,
                 gmlp_ln_g, gmlp_ln_b, gmlp_ws, gmlp_bs, gmlp_out_g, swa_sink, swa_out_g, need_ctx):
    zl = split_cols(h_lat @ w_in)
    zc = split_cols(h_ctx @ w_in)
    a_lat, a_ctx = gla_mixer(zl[0:5], zc[0:5], gla_wa2, gla_ba, gla_norm, need_ctx)
    c_lat, c_ctx = swa_mixer(zl[6:9], zc[6:9], ang_row, ang_col, swa_sink, need_ctx)

    def merge(a_out, z_gm, c_out, dtype):
        b_out = rmsnorm(gmlp_mixer(z_gm, gmlp_ln_g, gmlp_ln_b, gmlp_ws, gmlp_bs), gmlp_out_g)
        cat = jnp.concatenate([a_out, b_out, rmsnorm(c_out, swa_out_g)], axis=-1)
        return cat.astype(dtype) @ w_out

    y_lat = merge(a_lat, zl[5], c_lat, h_lat.dtype)
    y_ctx = merge(a_ctx, zc[5], c_ctx, h_ctx.dtype) if need_ctx else None
    return y_lat, y_ctx


def swiglu(h, w_gu, w_down):
    g, u = jnp.split(h @ w_gu, 2, axis=-1)
    return (jax.nn.silu(g) * u) @ w_down


def setup_inputs(seed: int = 0) -> dict:
    key = jax.random.key(seed)
    ks = jax.random.split(key, 32)
    f32 = jnp.float32

    def nrm(k, shape, scale):
        return jax.random.normal(k, shape, f32) * scale

    def gain(k, shape):
        return 1.0 + 0.05 * jax.random.normal(k, shape, f32)

    d = D_MODEL
    return {
        'x': nrm(ks[0], (BATCH, SEQ, d), 1.0),
        'c': nrm(ks[1], (BATCH, d), 1.0),
        'ctx': nrm(ks[2], (BATCH, CTX_LEN, d), 1.0),
        'c_ctx': nrm(ks[3], (d,), 1.0),
        'mod_w': nrm(ks[4], (DEPTH, d, 6 * d), 0.5 * d ** -0.5),
        'mod_b': nrm(ks[5], (DEPTH, 6 * d), 0.01),
        'n1_pre': gain(ks[6], (DEPTH, d)),
        'n1_post': gain(ks[7], (DEPTH, d)),
        'n2_pre': gain(ks[8], (DEPTH, d)),
        'n2_post': gain(ks[9], (DEPTH, d)),
        'w_in': nrm(ks[10], (DEPTH, d, IN_COLS), d ** -0.5),
        'w_out': nrm(ks[11], (DEPTH, MIX_WIDTH, d), MIX_WIDTH ** -0.5),
        'gla_wa2': nrm(ks[12], (DEPTH, 2, GLA_GATE_RANK, GLA_QK), GLA_GATE_RANK ** -0.5),
        'gla_ba': nrm(ks[13], (DEPTH, 2, GLA_QK), 0.1),
        'gla_norm': gain(ks[14], (DEPTH, GLA_DV)),
        'gmlp_ln_g': gain(ks[15], (DEPTH, GMLP_WIDTH)),
        'gmlp_ln_b': nrm(ks[16], (DEPTH, GMLP_WIDTH), 0.02),
        'gmlp_ws': nrm(ks[17], (DEPTH, GMLP_GROUPS, GMLP_CHUNK, GMLP_CHUNK), GMLP_CHUNK ** -0.5),
        'gmlp_bs': 1.0 + nrm(ks[18], (DEPTH, GMLP_GROUPS, GMLP_CHUNK), 0.02),
        'gmlp_out_g': gain(ks[19], (DEPTH, GMLP_WIDTH)),
        'swa_sink': nrm(ks[20], (DEPTH, SWA_HEADS), 0.5),
        'swa_out_g': gain(ks[21], (DEPTH, SWA_Q)),
        'ffn_w_gu': nrm(ks[22], (DEPTH, d, 2 * D_FF), d ** -0.5),
        'ffn_w_down': nrm(ks[23], (DEPTH, D_FF, d), D_FF ** -0.5),
    }


def reference(x, c, ctx, c_ctx, mod_w, mod_b, n1_pre, n1_post, n2_pre, n2_post, w_in, w_out,
              gla_wa2, gla_ba, gla_norm, gmlp_ln_g, gmlp_ln_b, gmlp_ws, gmlp_bs, gmlp_out_g,
              swa_sink, swa_out_g, ffn_w_gu, ffn_w_down):
    n_lat = x.shape[1]
    rows = n_lat // GRID_W
    row = jnp.repeat(jnp.arange(rows), GRID_W).astype(jnp.float32)
    col = jnp.tile(jnp.arange(GRID_W), rows).astype(jnp.float32)
    inv_freq = jnp.power(ROPE_THETA, -jnp.arange(0, ROPE_AXIS_DIM, 2, dtype=jnp.float32) / ROPE_AXIS_DIM)
    ang_row = row[:, None] * inv_freq[None, :]
    ang_col = col[:, None] * inv_freq[None, :]

    x_lat, x_ctx = x, ctx
    for l in range(DEPTH):
        need_ctx = l < DEPTH - 1
        mod_l = [m[:, None, :] for m in jnp.split(jax.nn.silu(c) @ mod_w[l] + mod_b[l], 6, axis=-1)]
        mod_c = jnp.split(jax.nn.silu(c_ctx) @ mod_w[l] + mod_b[l], 6, axis=-1)

        h_lat = modulate(rmsnorm(x_lat, n1_pre[l]), mod_l[0], mod_l[1])
        h_ctx = modulate(rmsnorm(x_ctx, n1_pre[l]), mod_c[0], mod_c[1])
        y_lat, y_ctx = hybrid_mixer(h_lat, h_ctx, ang_row, ang_col, w_in[l], w_out[l],
                                    gla_wa2[l], gla_ba[l], gla_norm[l], gmlp_ln_g[l], gmlp_ln_b[l],
                                    gmlp_ws[l], gmlp_bs[l], gmlp_out_g[l], swa_sink[l], swa_out_g[l],
                                    need_ctx)
        x_lat = x_lat + mod_l[2] * rmsnorm(y_lat, n1_post[l])

        f_lat = swiglu(modulate(rmsnorm(x_lat, n2_pre[l]), mod_l[3], mod_l[4]), ffn_w_gu[l], ffn_w_down[l])
        x_lat = x_lat + mod_l[5] * rmsnorm(f_lat, n2_post[l])

        if need_ctx:
            x_ctx = x_ctx + mod_c[2] * rmsnorm(y_ctx, n1_post[l])
            f_ctx = swiglu(modulate(rmsnorm(x_ctx, n2_pre[l]), mod_c[3], mod_c[4]), ffn_w_gu[l], ffn_w_down[l])
            x_ctx = x_ctx + mod_c[5] * rmsnorm(f_ctx, n2_post[l])
    return x_lat
```

```python
import functools

import numpy as np
import jax
import jax.numpy as jnp
from jax import lax
from jax.experimental import pallas as pl
from jax.experimental.pallas import tpu as pltpu

F32 = jnp.float32
BF16 = jnp.bfloat16

D_MODEL = 1024
GRID_W = 64
HEAD_DIM = 64
GLA_HEADS = 4
GLA_DK = 32
GLA_DV = 64
GLA_QK = GLA_HEADS * GLA_DK
GLA_V = GLA_HEADS * GLA_DV
GLA_GATE_RANK = 16
GLA_GATE_TAU = 16.0
GLA_CHUNK = 64
GMLP_GROUPS = 4
GMLP_GDIM = 64
GMLP_WIDTH = GMLP_GROUPS * GMLP_GDIM
GMLP_CHUNK = 128
SWA_HEADS = 8
SWA_KV_HEADS = 2
SWA_REP = SWA_HEADS // SWA_KV_HEADS
SWA_Q = SWA_HEADS * HEAD_DIM
SWA_KV = SWA_KV_HEADS * HEAD_DIM
SWA_WINDOW = 128
ROPE_AXIS_DIM = HEAD_DIM // 2
ROPE_THETA = 10000.0
MIX_WIDTH = GLA_V + GMLP_WIDTH + SWA_Q
D_FF = -(-8 * D_MODEL // (3 * 256)) * 256
N_MOD = 6
MOD_ROWS = 24

COL_GLA = 2 * GLA_QK + 2 * GLA_V
COL_GM = 2 * GMLP_WIDTH
COL_SQ = SWA_Q
COL_SKV = 2 * SWA_KV
COL_CODE = 128
OFF_GM = COL_GLA
OFF_SQ = OFF_GM + COL_GM
OFF_SKV = OFF_SQ + COL_SQ
OFF_CODE = OFF_SKV + COL_SKV
COLS_IN = OFF_CODE + COL_CODE

VMEM_LIMIT = 56 * 1024 * 1024
GLA_LEVELS = (1, 2, 4, 8, 16, 32)


def _cparams(*sem):
    return pltpu.CompilerParams(dimension_semantics=sem, vmem_limit_bytes=VMEM_LIMIT)


def _const_spec(shape):
    nd = len(shape)
    return pl.BlockSpec(shape, lambda *_: (0,) * nd)


def _rms(x, g, eps=1e-6):
    return x * lax.rsqrt(jnp.mean(x * x, axis=-1, keepdims=True) + eps) * g


def _silu(x):
    return x * jax.nn.sigmoid(x)


def _mod_rows(mod_ref, b, idx, is_ctx, ctx_row):
    lo = idx * D_MODEL
    lat = mod_ref[pl.ds(b, 1), lo:lo + D_MODEL]
    ctx = mod_ref[ctx_row:ctx_row + 1, lo:lo + D_MODEL]
    return jnp.where(is_ctx, ctx, lat)


def _mod_kernel(c_ref, w_ref, b_ref, o_ref):
    s = _silu(c_ref[...]).astype(BF16)
    o_ref[0] = jnp.dot(s, w_ref[0].astype(BF16), preferred_element_type=F32) + b_ref[0]


def _modulation(cc, mod_w, mod_b):
    depth, d, n = mod_w.shape
    tn = n // 4
    return pl.pallas_call(
        _mod_kernel,
        grid=(depth, n // tn),
        in_specs=[pl.BlockSpec((MOD_ROWS, d), lambda l, j: (0, 0)),
                  pl.BlockSpec((1, d, tn), lambda l, j: (l, 0, j)),
                  pl.BlockSpec((1, 1, tn), lambda l, j: (l, 0, j))],
        out_specs=pl.BlockSpec((1, MOD_ROWS, tn), lambda l, j: (l, 0, j)),
        out_shape=jax.ShapeDtypeStruct((depth, MOD_ROWS, n), F32),
        compiler_params=_cparams("parallel", "parallel"),
        name="modulation",
    )(cc, mod_w, mod_b.reshape(depth, 1, n))


def _rope(z, cos, sin):
    lane = lax.broadcasted_iota(jnp.int32, z.shape, 1)
    first = (lane % 32) < 16
    rot = jnp.where(first, pltpu.roll(z, 128 - 16, 1), pltpu.roll(z, 16, 1))
    return z * cos + rot * sin


def _inproj_kernel(x_ref, mod_ref, g_ref, w_ref, wa_ref, ba_ref, cos_ref, sin_ref,
                   gla_ref, lg_ref, gm_ref, sq_ref, skv_ref, *, t_lat, tm, ctx_row):
    b = pl.program_id(0)
    j = pl.program_id(1)
    x = x_ref[0]
    row = j * tm + lax.broadcasted_iota(jnp.int32, (tm, 1), 0)
    is_ctx = row >= t_lat
    shift = _mod_rows(mod_ref, b, 0, is_ctx, ctx_row)
    scale = _mod_rows(mod_ref, b, 1, is_ctx, ctx_row)
    h = (_rms(x, g_ref[...]) * (1.0 + scale) + shift).astype(BF16)

    def proj(off, n):
        return jnp.dot(h, w_ref[:, off:off + n], preferred_element_type=F32)

    z = proj(0, COL_GLA)
    gla_ref[0, :, 0:GLA_QK] = (z[:, 0:GLA_QK] * (GLA_DK ** -0.5)).astype(BF16)
    gla_ref[0, :, GLA_QK:COL_GLA] = z[:, GLA_QK:COL_GLA].astype(BF16)

    gm_ref[0] = proj(OFF_GM, COL_GM).astype(BF16)

    cos = cos_ref[...]
    sin = sin_ref[...]
    z = proj(OFF_SQ, COL_SQ)
    for g in range(COL_SQ // 128):
        seg = _rope(z[:, g * 128:(g + 1) * 128], cos, sin) * (HEAD_DIM ** -0.5)
        sq_ref[0, :, g * 128:(g + 1) * 128] = seg.astype(BF16)

    z = proj(OFF_SKV, COL_SKV)
    skv_ref[0, :, 0:SWA_KV] = _rope(z[:, 0:SWA_KV], cos, sin).astype(BF16)
    skv_ref[0, :, SWA_KV:COL_SKV] = z[:, SWA_KV:COL_SKV].astype(BF16)

    code = proj(OFF_CODE, COL_CODE).astype(BF16)
    zg = jnp.dot(code, wa_ref[...], preferred_element_type=F32) + ba_ref[...]
    lg_ref[0] = jax.nn.log_sigmoid(zg) * (1.0 / GLA_GATE_TAU)


def _inproj(xs, mod, n_pre, w, wa, ba, cos, sin, *, t_lat, tm, ctx_row):
    bsz, t_all, d = xs.shape
    nt = t_all // tm
    tok = lambda n: pl.BlockSpec((1, tm, n), lambda b, j: (b, j, 0))
    out_shape = [jax.ShapeDtypeStruct((bsz, t_all, n), dt) for n, dt in
                 ((COL_GLA, BF16), (2 * GLA_QK, F32), (COL_GM, BF16), (COL_SQ, BF16), (COL_SKV, BF16))]
    return pl.pallas_call(
        functools.partial(_inproj_kernel, t_lat=t_lat, tm=tm, ctx_row=ctx_row),
        grid=(bsz, nt),
        in_specs=[tok(d), _const_spec(mod.shape), _const_spec(n_pre.shape), _const_spec(w.shape),
                  _const_spec(wa.shape), _const_spec(ba.shape),
                  pl.BlockSpec((tm, 128), lambda b, j: (j, 0)),
                  pl.BlockSpec((tm, 128), lambda b, j: (j, 0))],
        out_specs=[tok(COL_GLA), tok(2 * GLA_QK), tok(COL_GM), tok(COL_SQ), tok(COL_SKV)],
        out_shape=out_shape,
        compiler_params=_cparams("parallel", "parallel"),
        name="inproj",
    )(xs, mod, n_pre, w, wa, ba, cos, sin)


def _gla_tables():
    c = GLA_CHUNK
    t = np.arange(c)[:, None]
    r = np.arange(c)[None, :]
    sizes = [2 * b for b in GLA_LEVELS]
    same = lambda b: (t // b) == (r // b)
    a_f = [same(b) & (r <= t) for b in sizes] + [same(b) & (r > t) for b in sizes]
    a_b = [same(b) & (r >= t) for b in sizes] + [same(b) & (r < t) for b in sizes]
    m_f = [((t // b) % 2 == 1) & ((r // b) == (t // b) - 1) for b in GLA_LEVELS] + [t == r]
    m_b = [((t // b) % 2 == 0) & ((r // b) == (t // b) + 1) for b in GLA_LEVELS] + [t == r]
    amat = np.stack([np.concatenate(a_f, 0), np.concatenate(a_b, 0)]).astype(np.float32)
    mask = np.stack([np.stack([np.tile(m, (GLA_HEADS, 1)) for m in ms]) for ms in (m_f, m_b)])
    return amat, mask.astype(np.float32)


def _split3(x):
    p1 = x.astype(BF16)
    r1 = x - p1.astype(F32)
    p2 = r1.astype(BF16)
    p3 = (r1 - p2.astype(F32)).astype(BF16)
    return jnp.concatenate([p1, p2, p3], axis=-1)


def _gla_chunk(q, k, v, g, amat, masks, hmask_q, bd_v, bd_s, s_prev, *, backward):
    c = GLA_CHUNK
    nl = len(GLA_LEVELS)
    ps = jnp.dot(amat, _split3(g), preferred_element_type=F32)
    ps = ps[:, 0:GLA_QK] + ps[:, GLA_QK:2 * GLA_QK] + ps[:, 2 * GLA_QK:3 * GLA_QK]
    blk = lambda i: ps[i * c:(i + 1) * c]
    q_exp = [g] + [blk(i) for i in range(nl - 1)]
    k_exp = [None] + [blk(nl + i) for i in range(nl - 1)]
    q_full = blk(nl - 1)
    k_full = blk(2 * nl - 1)

    att = None
    for lvl in range(nl + 1):
        if lvl < nl:
            qe = q * jnp.exp(q_exp[lvl])
            ke = k if k_exp[lvl] is None else k * jnp.exp(k_exp[lvl])
        else:
            qe, ke = q, k
        qbd = jnp.concatenate([qe.astype(BF16)] * GLA_HEADS, axis=0) * hmask_q
        p = lax.dot_general(qbd, ke.astype(BF16), (((1,), (1,)), ((), ())), preferred_element_type=F32)
        p = p * masks[lvl]
        att = p if att is None else att + p

    r = jnp.dot(att.astype(BF16), v, preferred_element_type=F32) * bd_v
    o = r[0:c]
    for hh in range(1, GLA_HEADS):
        o = o + r[hh * c:(hh + 1) * c]
    o = o + jnp.dot((q * jnp.exp(q_full)).astype(BF16), s_prev.astype(BF16), preferred_element_type=F32)

    tot = q_full[0:1] if backward else q_full[c - 1:c]
    dcol = jnp.transpose(jnp.broadcast_to(jnp.exp(tot), (GLA_QK, GLA_QK)))
    dcol = jnp.concatenate([dcol, dcol], axis=1)
    kv = lax.dot_general((k * jnp.exp(k_full)).astype(BF16), v, (((0,), (0,)), ((), ())),
                         preferred_element_type=F32)
    return o, s_prev * dcol + kv * bd_s


def _gla_kernel(gla_ref, lg_ref, amat_ref, mask_ref, hq_ref, bdv_ref, bds_ref, ind_ref, norm_ref,
                out_ref, of_ref, ob_ref, *, t_lat, t_all, t_out):
    c = GLA_CHUNK
    n_lat = t_lat // c
    n_all = t_all // c
    hq = hq_ref[...]
    bdv = bdv_ref[...]
    bds = bds_ref[...]

    def load(start):
        start = pl.multiple_of(start, c)
        rows = pl.ds(start, c)
        q = gla_ref[0, rows, 0:GLA_QK].astype(F32)
        k = gla_ref[0, rows, GLA_QK:2 * GLA_QK].astype(F32)
        v = gla_ref[0, rows, 2 * GLA_QK:2 * GLA_QK + GLA_V]
        return rows, q, k, v

    def step(i, carry):
        s_f, s_b = carry
        cf = lax.rem(i + n_lat, n_all)
        cb = n_all - 1 - i
        rows, q, k, v = load(cf * c)
        o, s_f = _gla_chunk(q, k, v, lg_ref[0, rows, 0:GLA_QK], amat_ref[0],
                            [mask_ref[0, l] for l in range(len(GLA_LEVELS) + 1)], hq, bdv, bds, s_f,
                            backward=False)
        of_ref[rows, :] = o
        rows, q, k, v = load(cb * c)
        o, s_b = _gla_chunk(q, k, v, lg_ref[0, rows, GLA_QK:2 * GLA_QK], amat_ref[1],
                            [mask_ref[1, l] for l in range(len(GLA_LEVELS) + 1)], hq, bdv, bds, s_b,
                            backward=True)
        ob_ref[rows, :] = o
        return s_f, s_b

    zero = jnp.zeros((GLA_QK, GLA_V), F32)
    lax.fori_loop(0, n_all, step, (zero, zero))

    tr = 256
    ind = ind_ref[...]
    for r0 in range(0, t_out, tr):
        o = of_ref[r0:r0 + tr, :] + ob_ref[r0:r0 + tr, :]
        sq = o * o
        hi = sq.astype(BF16)
        lo = (sq - hi.astype(F32)).astype(BF16)
        ms = jnp.dot(hi, ind, preferred_element_type=F32) + jnp.dot(lo, ind, preferred_element_type=F32)
        gate = gla_ref[0, r0:r0 + tr, 2 * GLA_QK + GLA_V:COL_GLA].astype(F32)
        y = o * lax.rsqrt(ms + 1e-6) * norm_ref[...] * _silu(gate)
        out_ref[0, r0:r0 + tr, :] = y.astype(BF16)


def _gla(gla, lg, consts, norm_t, *, t_lat, t_out):
    bsz, t_all, _ = gla.shape
    amat, mask, hq, bdv, bds, ind = consts
    return pl.pallas_call(
        functools.partial(_gla_kernel, t_lat=t_lat, t_all=t_all, t_out=t_out),
        grid=(bsz,),
        in_specs=[pl.BlockSpec((1, t_all, COL_GLA), lambda b: (b, 0, 0)),
                  pl.BlockSpec((1, t_all, 2 * GLA_QK), lambda b: (b, 0, 0)),
                  _const_spec(amat.shape), _const_spec(mask.shape), _const_spec(hq.shape),
                  _const_spec(bdv.shape), _const_spec(bds.shape), _const_spec(ind.shape),
                  _const_spec(norm_t.shape)],
        out_specs=pl.BlockSpec((1, t_out, GLA_V), lambda b: (b, 0, 0)),
        out_shape=jax.ShapeDtypeStruct((bsz, t_out, GLA_V), BF16),
        scratch_shapes=[pltpu.VMEM((t_all, GLA_V), F32), pltpu.VMEM((t_all, GLA_V), F32)],
        compiler_params=_cparams("parallel"),
        name="gla",
    )(gla, lg, amat, mask, hq, bdv, bds, ind, norm_t)


def _gmlp_kernel(z_ref, lng_ref, lnb_ref, ws_ref, bs_ref, gm_ref, og_ref, out_ref, *, tm):
    z = z_ref[0].astype(F32)
    zf = 0.5 * z * (1.0 + lax.erf(z * (2.0 ** -0.5)))
    u = zf[:, 0:GMLP_WIDTH]
    v = zf[:, GMLP_WIDTH:]
    mu = jnp.mean(v, axis=-1, keepdims=True)
    vc = v - mu
    v = vc * lax.rsqrt(jnp.mean(vc * vc, axis=-1, keepdims=True) + 1e-5) * lng_ref[...] + lnb_ref[...]
    vb = v.astype(BF16)
    ws = ws_ref[...]
    p = GMLP_CHUNK
    for ci in range(tm // p):
        r = jnp.dot(ws, vb[ci * p:(ci + 1) * p], preferred_element_type=F32) * gm_ref[...]
        mixed = bs_ref[...]
        for g in range(GMLP_GROUPS):
            mixed = mixed + r[g * p:(g + 1) * p]
        y = u[ci * p:(ci + 1) * p] * mixed
        out_ref[0, ci * p:(ci + 1) * p, :] = _rms(y, og_ref[...]).astype(BF16)


def _gmlp(gm, ln_g, ln_b, ws_s, bs_t, gmask, out_g, *, t_out, tm):
    bsz = gm.shape[0]
    return pl.pallas_call(
        functools.partial(_gmlp_kernel, tm=tm),
        grid=(bsz, t_out // tm),
        in_specs=[pl.BlockSpec((1, tm, COL_GM), lambda b, j: (b, j, 0)),
                  _const_spec(ln_g.shape), _const_spec(ln_b.shape), _const_spec(ws_s.shape),
                  _const_spec(bs_t.shape), _const_spec(gmask.shape), _const_spec(out_g.shape)],
        out_specs=pl.BlockSpec((1, tm, GMLP_WIDTH), lambda b, j: (b, j, 0)),
        out_shape=jax.ShapeDtypeStruct((bsz, t_out, GMLP_WIDTH), BF16),
        compiler_params=_cparams("parallel", "parallel"),
        name="gmlp",
    )(gm, ln_g, ln_b, ws_s, bs_t, gmask, out_g)


def _swa_kernel(q_ref, kv_ref, sink_ref, og_ref, out_ref, *, t_lat, t_all):
    w = SWA_WINDOW
    n = pl.program_id(1)
    n_lat = t_lat // w
    lat_q = n < n_lat
    own = jnp.minimum(n, n_lat - 1)
    prev = jnp.maximum(own - 1, 0)
    nxt = jnp.minimum(own + 1, n_lat - 1)

    def rows(blk):
        return pl.ds(pl.multiple_of(blk * w, w), w)

    kv = jnp.concatenate([kv_ref[0, t_lat:t_all, :], kv_ref[0, rows(prev), :], kv_ref[0, rows(own), :],
                          kv_ref[0, rows(nxt), :]], axis=0)
    n_ctx = t_all - t_lat
    tq = lax.broadcasted_iota(jnp.int32, (w, w), 0)
    sk = lax.broadcasted_iota(jnp.int32, (w, w), 1)
    neg = jnp.full((w, w), -jnp.inf, F32)
    zero = jnp.zeros((w, w), F32)
    b_prev = jnp.where((sk >= tq) & jnp.logical_and(lat_q, n >= 1), zero, neg)
    b_own = jnp.where(lat_q, zero, neg)
    b_next = jnp.where((sk <= tq) & (n < n_lat - 1), zero, neg)
    bias = jnp.concatenate([jnp.zeros((w, n_ctx), F32), b_prev, b_own, b_next], axis=1)
    bias = jnp.concatenate([bias] * SWA_REP, axis=0)

    q = q_ref[0]
    outs = []
    for g in range(SWA_KV_HEADS):
        kg = kv[:, g * HEAD_DIM:(g + 1) * HEAD_DIM]
        vg = kv[:, SWA_KV + g * HEAD_DIM:SWA_KV + (g + 1) * HEAD_DIM]
        qg = jnp.concatenate([q[:, (g * SWA_REP + r) * HEAD_DIM:(g * SWA_REP + r + 1) * HEAD_DIM]
                              for r in range(SWA_REP)], axis=0)
        s = lax.dot_general(qg, kg, (((1,), (1,)), ((), ())), preferred_element_type=F32)
        s = s + bias
        sink = jnp.concatenate([jnp.broadcast_to(sink_ref[g * SWA_REP + r:g * SWA_REP + r + 1, 0:1], (w, 1))
                                for r in range(SWA_REP)], axis=0)
        m = jnp.maximum(jnp.max(s, axis=-1, keepdims=True), sink)
        p = jnp.exp(s - m)
        den = jnp.sum(p, axis=-1, keepdims=True) + jnp.exp(sink - m)
        o = jnp.dot(p.astype(BF16), vg, preferred_element_type=F32) / den
        outs += [o[r * w:(r + 1) * w] for r in range(SWA_REP)]
    o = jnp.concatenate(outs, axis=1)
    out_ref[0] = _rms(o, og_ref[...]).astype(BF16)


def _swa(sq, skv, sink_t, out_g, *, t_lat, t_out):
    bsz, t_all, _ = sq.shape
    w = SWA_WINDOW
    return pl.pallas_call(
        functools.partial(_swa_kernel, t_lat=t_lat, t_all=t_all),
        grid=(bsz, t_out // w),
        in_specs=[pl.BlockSpec((1, w, COL_SQ), lambda b, n: (b, n, 0)),
                  pl.BlockSpec((1, t_all, COL_SKV), lambda b, n: (b, 0, 0)),
                  _const_spec(sink_t.shape), _const_spec(out_g.shape)],
        out_specs=pl.BlockSpec((1, w, SWA_Q), lambda b, n: (b, n, 0)),
        out_shape=jax.ShapeDtypeStruct((bsz, t_out, SWA_Q), BF16),
        compiler_params=_cparams("parallel", "parallel"),
        name="swa",
    )(sq, skv, sink_t, out_g)


FFN_CHUNKS = ((0, 1536), (1536, 1280))


def _post_kernel(a_ref, b_ref, c_ref, x_ref, mod_ref, n1_ref, n2a_ref, n2b_ref, wo_ref, wgu_ref, wd_ref,
                 out_ref, *, t_lat, tm, ctx_row):
    b = pl.program_id(0)
    j = pl.program_id(1)
    row = j * tm + lax.broadcasted_iota(jnp.int32, (tm, 1), 0)
    is_ctx = row >= t_lat
    mod = lambda idx: _mod_rows(mod_ref, b, idx, is_ctx, ctx_row)

    cat = jnp.concatenate([a_ref[0], b_ref[0], c_ref[0]], axis=-1)
    y = jnp.dot(cat, wo_ref[...], preferred_element_type=F32)
    x1 = x_ref[0] + mod(2) * _rms(y, n1_ref[...])
    h = (_rms(x1, n2a_ref[...]) * (1.0 + mod(4)) + mod(3)).astype(BF16)
    f = None
    for off, n in FFN_CHUNKS:
        gt = jnp.dot(h, wgu_ref[:, off:off + n], preferred_element_type=F32)
        up = jnp.dot(h, wgu_ref[:, D_FF + off:D_FF + off + n], preferred_element_type=F32)
        act = (_silu(gt) * up).astype(BF16)
        part = jnp.dot(act, wd_ref[off:off + n, :], preferred_element_type=F32)
        f = part if f is None else f + part
    out_ref[0] = x1 + mod(5) * _rms(f, n2b_ref[...])


def _post(a, bo, c, xs, mod, n1_post, n2_pre, n2_post, wo, wgu, wd, *, t_lat, t_out, tm, ctx_row):
    bsz = xs.shape[0]
    tok = lambda n: pl.BlockSpec((1, tm, n), lambda b, j: (b, j, 0))
    once = lambda arr: pl.BlockSpec(arr.shape, lambda b, j: (0,) * arr.ndim, pipeline_mode=pl.Buffered(1))
    return pl.pallas_call(
        functools.partial(_post_kernel, t_lat=t_lat, tm=tm, ctx_row=ctx_row),
        grid=(bsz, t_out // tm),
        in_specs=[tok(GLA_V), tok(GMLP_WIDTH), tok(SWA_Q), tok(D_MODEL), _const_spec(mod.shape),
                  _const_spec(n1_post.shape), _const_spec(n2_pre.shape), _const_spec(n2_post.shape),
                  once(wo), once(wgu), once(wd)],
        out_specs=tok(D_MODEL),
        out_shape=jax.ShapeDtypeStruct((bsz, t_out, D_MODEL), F32),
        compiler_params=_cparams("parallel", "parallel"),
        name="post",
    )(a, bo, c, xs, mod, n1_post, n2_pre, n2_post, wo, wgu, wd)


def _rope_tables(t_lat, t_ctx):
    rows = t_lat // GRID_W
    row = jnp.repeat(jnp.arange(rows), GRID_W).astype(F32)
    col = jnp.tile(jnp.arange(GRID_W), rows).astype(F32)
    inv_freq = jnp.power(ROPE_THETA, -jnp.arange(0, ROPE_AXIS_DIM, 2, dtype=F32) / ROPE_AXIS_DIM)
    ang_row = row[:, None] * inv_freq[None, :]
    ang_col = col[:, None] * inv_freq[None, :]
    ang = jnp.concatenate([ang_row, ang_row, ang_col, ang_col], axis=-1)
    sign = jnp.tile(jnp.concatenate([-jnp.ones((16,), F32), jnp.ones((16,), F32)]), 2)
    cos = jnp.concatenate([jnp.cos(ang), jnp.ones((t_ctx, HEAD_DIM), F32)], axis=0)
    sin = jnp.concatenate([jnp.sin(ang) * sign, jnp.zeros((t_ctx, HEAD_DIM), F32)], axis=0)
    return jnp.tile(cos, (1, 2)), jnp.tile(sin, (1, 2))


def _reorder_w_in(w):
    o_code = 2 * GLA_QK + 2 * GLA_V
    o_gm = o_code + 2 * GLA_GATE_RANK
    o_sq = o_gm + 2 * GMLP_WIDTH
    pad = jnp.zeros((w.shape[0], COL_CODE - 2 * GLA_GATE_RANK), w.dtype)
    return jnp.concatenate([w[:, 0:o_code], w[:, o_gm:o_sq], w[:, o_sq:], w[:, o_code:o_gm], pad],
                           axis=1).astype(BF16)


def _gate_weights(wa2, ba):
    r = GLA_GATE_RANK
    wa = jnp.zeros((COL_CODE, 2 * GLA_QK), F32)
    wa = wa.at[0:r, 0:GLA_QK].set(wa2[0]).at[r:2 * r, GLA_QK:].set(wa2[1])
    return wa.astype(BF16), ba.reshape(1, 2 * GLA_QK)


def _gla_consts():
    amat, mask = _gla_tables()
    lane_head = np.arange(GLA_QK)[None, :] // GLA_DK
    row_head = np.arange(GLA_HEADS * GLA_CHUNK)[:, None] // GLA_CHUNK
    hq = (lane_head == row_head).astype(np.float32)
    vlane_head = np.arange(GLA_V)[None, :] // GLA_DV
    bdv = (vlane_head == row_head).astype(np.float32)
    bds = (vlane_head == (np.arange(GLA_QK)[:, None] // GLA_DK)).astype(np.float32)
    ind = (vlane_head == vlane_head.T).astype(np.float32) / GLA_DV
    return (jnp.asarray(amat, BF16), jnp.asarray(mask, F32), jnp.asarray(hq, BF16), jnp.asarray(bdv, F32),
            jnp.asarray(bds, F32), jnp.asarray(ind, BF16))


def kernel(x, c, ctx, c_ctx, mod_w, mod_b, n1_pre, n1_post, n2_pre, n2_post, w_in, w_out, gla_wa2, gla_ba,
           gla_norm, gmlp_ln_g, gmlp_ln_b, gmlp_ws, gmlp_bs, gmlp_out_g, swa_sink, swa_out_g, ffn_w_gu,
           ffn_w_down):
    bsz, t_lat, d = x.shape
    t_ctx = ctx.shape[1]
    t_all = t_lat + t_ctx
    depth = mod_w.shape[0]
    assert d == D_MODEL and bsz < MOD_ROWS
    assert t_lat % 1024 == 0 and t_ctx % 256 == 0 and t_all % 768 == 0
    ctx_row = bsz
    tm_all = 768

    cc = jnp.zeros((MOD_ROWS, d), F32).at[0:bsz].set(c).at[ctx_row].set(c_ctx)
    mods = _modulation(cc, mod_w, mod_b)
    cos, sin = _rope_tables(t_lat, t_ctx)
    gla_consts = _gla_consts()
    gmask = jnp.asarray((np.arange(GMLP_WIDTH)[None, :] // GMLP_GDIM ==
                         np.arange(GMLP_GROUPS * GMLP_CHUNK)[:, None] // GMLP_CHUNK).astype(np.float32))
    row = lambda v: v.reshape(1, -1)

    xs = jnp.concatenate([x, ctx], axis=1)
    for l in range(depth):
        last = l == depth - 1
        t_out = t_lat if last else t_all
        wa, ba = _gate_weights(gla_wa2[l], gla_ba[l])
        gla, lg, gm, sq, skv = _inproj(xs, mods[l], row(n1_pre[l]), _reorder_w_in(w_in[l]), wa, ba, cos, sin,
                                       t_lat=t_lat, tm=tm_all, ctx_row=ctx_row)
        a_out = _gla(gla, lg, gla_consts, row(jnp.tile(gla_norm[l], GLA_HEADS)), t_lat=t_lat, t_out=t_out)
        ws_s = gmlp_ws[l].reshape(GMLP_GROUPS * GMLP_CHUNK, GMLP_CHUNK).astype(BF16)
        bs_t = jnp.repeat(gmlp_bs[l].T, GMLP_GDIM, axis=1)
        b_out = _gmlp(gm, row(gmlp_ln_g[l]), row(gmlp_ln_b[l]), ws_s, bs_t, gmask, row(gmlp_out_g[l]),
                      t_out=t_out, tm=256)
        sink_t = jnp.broadcast_to(swa_sink[l][:, None], (SWA_HEADS, 128))
        c_out = _swa(sq, skv, sink_t, row(swa_out_g[l]), t_lat=t_lat, t_out=t_out)
        xs = _post(a_out, b_out, c_out, xs, mods[l], row(n1_post[l]), row(n2_pre[l]), row(n2_post[l]),
                   w_out[l].astype(BF16), ffn_w_gu[l].astype(BF16), ffn_w_down[l].astype(BF16),
                   t_lat=t_lat, t_out=t_out, tm=1024 if last else tm_all, ctx_row=ctx_row)
    return xs
```

```python
import functools

import numpy as np
import jax
import jax.numpy as jnp
from jax import lax
from jax.experimental import pallas as pl
from jax.experimental.pallas import tpu as pltpu

F32 = jnp.float32
BF16 = jnp.bfloat16

D_MODEL = 1024
GRID_W = 64
HEAD_DIM = 64
GLA_HEADS = 4
GLA_DK = 32
GLA_DV = 64
GLA_QK = GLA_HEADS * GLA_DK
GLA_V = GLA_HEADS * GLA_DV
GLA_GATE_RANK = 16
GLA_GATE_TAU = 16.0
GLA_CHUNK = 64
GMLP_GROUPS = 4
GMLP_GDIM = 64
GMLP_WIDTH = GMLP_GROUPS * GMLP_GDIM
GMLP_CHUNK = 128
SWA_HEADS = 8
SWA_KV_HEADS = 2
SWA_REP = SWA_HEADS // SWA_KV_HEADS
SWA_Q = SWA_HEADS * HEAD_DIM
SWA_KV = SWA_KV_HEADS * HEAD_DIM
SWA_WINDOW = 128
ROPE_AXIS_DIM = HEAD_DIM // 2
ROPE_THETA = 10000.0
MIX_WIDTH = GLA_V + GMLP_WIDTH + SWA_Q
D_FF = -(-8 * D_MODEL // (3 * 256)) * 256
N_MOD = 6
MOD_ROWS = 24

COL_GLA = 2 * GLA_QK + 2 * GLA_V
COL_GM = 2 * GMLP_WIDTH
COL_SQ = SWA_Q
COL_SKV = 2 * SWA_KV
COL_CODE = 128
OFF_GM = COL_GLA
OFF_SQ = OFF_GM + COL_GM
OFF_SKV = OFF_SQ + COL_SQ
OFF_CODE = OFF_SKV + COL_SKV
COLS_IN = OFF_CODE + COL_CODE

VMEM_LIMIT = 56 * 1024 * 1024
GLA_LEVELS = (1, 2, 4, 8, 16, 32)


def _cparams(*sem):
    return pltpu.CompilerParams(dimension_semantics=sem, vmem_limit_bytes=VMEM_LIMIT)


def _const_spec(shape):
    nd = len(shape)
    return pl.BlockSpec(shape, lambda *_: (0,) * nd)


def _rms(x, g, eps=1e-6):
    return x * lax.rsqrt(jnp.mean(x * x, axis=-1, keepdims=True) + eps) * g


def _silu(x):
    return x * jax.nn.sigmoid(x)


def _mod_rows(mod_ref, b, idx, is_ctx, ctx_row):
    lo = idx * D_MODEL
    lat = mod_ref[pl.ds(b, 1), lo:lo + D_MODEL]
    ctx = mod_ref[ctx_row:ctx_row + 1, lo:lo + D_MODEL]
    return jnp.where(is_ctx, ctx, lat)


def _mod_kernel(c_ref, w_ref, b_ref, o_ref):
    s = _silu(c_ref[...]).astype(BF16)
    o_ref[0] = jnp.dot(s, w_ref[0].astype(BF16), preferred_element_type=F32) + b_ref[0]


def _modulation(cc, mod_w, mod_b):
    depth, d, n = mod_w.shape
    tn = n // 4
    return pl.pallas_call(
        _mod_kernel,
        grid=(depth, n // tn),
        in_specs=[pl.BlockSpec((MOD_ROWS, d), lambda l, j: (0, 0)),
                  pl.BlockSpec((1, d, tn), lambda l, j: (l, 0, j)),
                  pl.BlockSpec((1, 1, tn), lambda l, j: (l, 0, j))],
        out_specs=pl.BlockSpec((1, MOD_ROWS, tn), lambda l, j: (l, 0, j)),
        out_shape=jax.ShapeDtypeStruct((depth, MOD_ROWS, n), F32),
        compiler_params=_cparams("parallel", "parallel"),
        name="modulation",
    )(cc, mod_w, mod_b.reshape(depth, 1, n))


def _rope(z, cos, sin):
    lane = lax.broadcasted_iota(jnp.int32, z.shape, 1)
    first = (lane % 32) < 16
    rot = jnp.where(first, pltpu.roll(z, 128 - 16, 1), pltpu.roll(z, 16, 1))
    return z * cos + rot * sin


def _split3(x):
    p1 = x.astype(BF16)
    r1 = x - p1.astype(F32)
    p2 = r1.astype(BF16)
    p3 = (r1 - p2.astype(F32)).astype(BF16)
    return p1, p2, p3


def _log_gate(z):
    return jax.nn.log_sigmoid(z) * (1.0 / GLA_GATE_TAU)


def _inproj_kernel(x_ref, mod_ref, g_ref, w_ref, wa_ref, ba_ref, wkt_ref, wct_ref, wat_ref, bat_ref,
                   cos_ref, sin_ref, gla_ref, lg_ref, gm_ref, sq_ref, skv_ref, kt_ref, lgt_ref,
                   *, t_lat, tm, ctx_row):
    b = pl.program_id(0)
    j = pl.program_id(1)
    x = x_ref[0]
    row = j * tm + lax.broadcasted_iota(jnp.int32, (tm, 1), 0)
    is_ctx = row >= t_lat
    shift = _mod_rows(mod_ref, b, 0, is_ctx, ctx_row)
    scale = _mod_rows(mod_ref, b, 1, is_ctx, ctx_row)
    h = (_rms(x, g_ref[...]) * (1.0 + scale) + shift).astype(BF16)

    def proj(off, n):
        return jnp.dot(h, w_ref[:, off:off + n], preferred_element_type=F32)

    def proj_t(wt_ref):
        return lax.dot_general(wt_ref[...], h, (((1,), (1,)), ((), ())), preferred_element_type=F32)

    z = proj(0, COL_GLA)
    gla_ref[0, :, 0:GLA_QK] = (z[:, 0:GLA_QK] * (GLA_DK ** -0.5)).astype(BF16)
    gla_ref[0, :, GLA_QK:COL_GLA] = z[:, GLA_QK:COL_GLA].astype(BF16)
    kt_ref[0] = proj_t(wkt_ref).astype(BF16)

    gm_ref[0] = proj(OFF_GM, COL_GM).astype(BF16)

    cos = cos_ref[...]
    sin = sin_ref[...]
    z = proj(OFF_SQ, COL_SQ)
    for g in range(COL_SQ // 128):
        seg = _rope(z[:, g * 128:(g + 1) * 128], cos, sin) * (HEAD_DIM ** -0.5)
        sq_ref[0, :, g * 128:(g + 1) * 128] = seg.astype(BF16)

    z = proj(OFF_SKV, COL_SKV)
    skv_ref[0, :, 0:SWA_KV] = _rope(z[:, 0:SWA_KV], cos, sin).astype(BF16)
    skv_ref[0, :, SWA_KV:COL_SKV] = z[:, SWA_KV:COL_SKV].astype(BF16)

    n2 = 2 * GLA_QK
    code = proj(OFF_CODE, COL_CODE).astype(BF16)
    lg = _log_gate(jnp.dot(code, wa_ref[...], preferred_element_type=F32) + ba_ref[...])
    for i, p in enumerate(_split3(lg)):
        lg_ref[0, :, i * n2:(i + 1) * n2] = p
    code_t = proj_t(wct_ref).astype(BF16)
    lgt = _log_gate(jnp.dot(wat_ref[...], code_t, preferred_element_type=F32) + bat_ref[...])
    for i, p in enumerate(_split3(lgt)):
        lgt_ref[0, i * n2:(i + 1) * n2, :] = p


def _inproj(xs, mod, n_pre, w, gate_w, cos, sin, *, t_lat, tm, ctx_row):
    bsz, t_all, d = xs.shape
    nt = t_all // tm
    tok = lambda n: pl.BlockSpec((1, tm, n), lambda b, j: (b, j, 0))
    tok_t = lambda n: pl.BlockSpec((1, n, tm), lambda b, j: (b, 0, j))
    n2 = 2 * GLA_QK
    out_shape = [jax.ShapeDtypeStruct((bsz, t_all, n), BF16) for n in (COL_GLA, 3 * n2, COL_GM, COL_SQ, COL_SKV)]
    out_shape += [jax.ShapeDtypeStruct((bsz, n, t_all), BF16) for n in (GLA_QK, 3 * n2)]
    consts = (mod, n_pre, w) + tuple(gate_w)
    return pl.pallas_call(
        functools.partial(_inproj_kernel, t_lat=t_lat, tm=tm, ctx_row=ctx_row),
        grid=(bsz, nt),
        in_specs=[tok(d)] + [_const_spec(a.shape) for a in consts] +
                 [pl.BlockSpec((tm, 128), lambda b, j: (j, 0)), pl.BlockSpec((tm, 128), lambda b, j: (j, 0))],
        out_specs=[tok(COL_GLA), tok(3 * n2), tok(COL_GM), tok(COL_SQ), tok(COL_SKV), tok_t(GLA_QK), tok_t(3 * n2)],
        out_shape=out_shape,
        compiler_params=_cparams("parallel", "parallel"),
        name="inproj",
    )(xs, *consts, cos, sin)


GLA_FAST_CHUNK = 128
GLA_FAST_UNROLL = 2
GLA_SAFE_LOG_DECAY = -40.0


def _gla_fast_tables():
    c = GLA_FAST_CHUNK
    t = np.arange(c)[:, None]
    r = np.arange(c)[None, :]
    incl = [(r <= t), (r >= t)]
    a3 = np.stack([np.tile(m, (1, 3)) for m in incl]).astype(np.float32)
    a3t_neg = -np.stack([np.tile(m.T, (3, 1)) for m in incl]).astype(np.float32)
    mask_w = np.stack([np.tile(m, (1, GLA_HEADS)) for m in incl]).astype(np.float32)
    half = (np.arange(128)[None, :] // GLA_DV) == ((np.arange(GLA_QK)[:, None] // GLA_DK) % 2)
    return a3, a3t_neg, mask_w, half.astype(np.float32)


def _gla_fast_chunks(jobs, a3, a3t_neg, mask_w, bd_c, states):
    c = GLA_FAST_CHUNK
    dk, hd = GLA_DK, GLA_QK
    cums = [(jnp.dot(a3[d], g3, preferred_element_type=F32),
             jnp.dot(gt3, a3t_neg[d], preferred_element_type=F32))
            for d, _, _, _, g3, gt3 in jobs]

    zr = lambda n: jnp.zeros((n, c), BF16)
    qes, dcols, khs, atts = [], [], [], []
    for (d, q, kt, _, _, _), (cum, ncum_t) in zip(jobs, cums):
        qe = (q.astype(F32) * jnp.exp(cum)).astype(BF16)
        ke_t = kt.astype(F32) * jnp.exp(ncum_t)
        edge = 0 if d else c - 1
        dcol = jnp.exp(-ncum_t[:, edge:edge + 1])
        khs.append((ke_t * dcol).astype(BF16))
        ke_t = ke_t.astype(BF16)
        cols = []
        for hh in range(GLA_HEADS):
            parts = ([zr(hh * dk)] if hh else []) + [ke_t[hh * dk:(hh + 1) * dk]]
            parts += [zr(hd - (hh + 1) * dk)] if hh < GLA_HEADS - 1 else []
            cols.append(jnp.concatenate(parts, axis=0))
        kbd = jnp.concatenate(cols, axis=1)
        atts.append(jnp.dot(qe, kbd, preferred_element_type=F32))
        qes.append(qe)
        dcols.append(dcol)

    kvs = [jnp.concatenate([jnp.dot(kh[0:hd // 2], v[:, 0:128], preferred_element_type=F32),
                            jnp.dot(kh[hd // 2:hd], v[:, 128:256], preferred_element_type=F32)], axis=0)
           for (_, _, _, v, _, _), kh in zip(jobs, khs)]

    lane = lax.broadcasted_iota(jnp.int32, (1, 128), 1)
    m_lo = (lane < GLA_DV).astype(BF16)
    m_hi = (lane >= GLA_DV).astype(BF16)
    z128 = jnp.zeros((c, 128), BF16)
    z64 = jnp.zeros((hd // 2, 128), BF16)
    states = list(states)
    outs = []
    for (d, _, _, v, _, _), qe, dcol, att, kv in zip(jobs, qes, dcols, atts, kvs):
        v_lo, v_hi = v[:, 0:128], v[:, 128:256]
        vbd = jnp.concatenate([jnp.concatenate([v_lo * m_lo, z128], axis=1),
                               jnp.concatenate([v_lo * m_hi, z128], axis=1),
                               jnp.concatenate([z128, v_hi * m_lo], axis=1),
                               jnp.concatenate([z128, v_hi * m_hi], axis=1)], axis=0)
        sb = states[d].astype(BF16)
        s_full = jnp.concatenate([jnp.concatenate([sb[0:hd // 2], z64], axis=1),
                                  jnp.concatenate([z64, sb[hd // 2:hd]], axis=1)], axis=0)
        lhs = jnp.concatenate([att.astype(BF16) * mask_w[d], qe], axis=1)
        outs.append(jnp.dot(lhs, jnp.concatenate([vbd, s_full], axis=0),
                            preferred_element_type=F32))
        states[d] = states[d] * dcol + kv * bd_c
    return outs, states


def _gla_tables():
    c = GLA_CHUNK
    t = np.arange(c)[:, None]
    r = np.arange(c)[None, :]
    sizes = [2 * b for b in GLA_LEVELS]
    same = lambda b: (t // b) == (r // b)
    a_f = [same(b) & (r <= t) for b in sizes] + [same(b) & (r > t) for b in sizes]
    a_b = [same(b) & (r >= t) for b in sizes] + [same(b) & (r < t) for b in sizes]
    m_f = [((t // b) % 2 == 1) & ((r // b) == (t // b) - 1) for b in GLA_LEVELS] + [t == r]
    m_b = [((t // b) % 2 == 0) & ((r // b) == (t // b) + 1) for b in GLA_LEVELS] + [t == r]
    amat = np.stack([np.tile(np.concatenate(a, 0), (1, 3)) for a in (a_f, a_b)]).astype(np.float32)
    mask = np.stack([np.stack([np.tile(m, (GLA_HEADS, 1)) for m in ms]) for ms in (m_f, m_b)])
    return amat, mask.astype(np.float32)


def _gla_chunk(q, k, v, g3, amat, masks, hmask_q, bd_v, bd_s, s_prev, *, backward):
    c = GLA_CHUNK
    nl = len(GLA_LEVELS)
    g = g3[0:c].astype(F32) + g3[c:2 * c].astype(F32) + g3[2 * c:3 * c].astype(F32)
    ps = jnp.dot(amat, g3, preferred_element_type=F32)
    blk = lambda i: ps[i * c:(i + 1) * c]
    q_exp = [g] + [blk(i) for i in range(nl - 1)]
    k_exp = [None] + [blk(nl + i) for i in range(nl - 1)]
    q_full = blk(nl - 1)
    k_full = blk(2 * nl - 1)

    att = None
    for lvl in range(nl + 1):
        if lvl < nl:
            qe = q * jnp.exp(q_exp[lvl])
            ke = k if k_exp[lvl] is None else k * jnp.exp(k_exp[lvl])
        else:
            qe, ke = q, k
        qbd = jnp.concatenate([qe.astype(BF16)] * GLA_HEADS, axis=0) * hmask_q
        p = lax.dot_general(qbd, ke.astype(BF16), (((1,), (1,)), ((), ())), preferred_element_type=F32)
        p = p * masks[lvl]
        att = p if att is None else att + p

    r = jnp.dot(att.astype(BF16), v, preferred_element_type=F32) * bd_v
    o = r[0:c]
    for hh in range(1, GLA_HEADS):
        o = o + r[hh * c:(hh + 1) * c]
    o = o + jnp.dot((q * jnp.exp(q_full)).astype(BF16), s_prev.astype(BF16), preferred_element_type=F32)

    tot = q_full[0:1] if backward else q_full[c - 1:c]
    dcol = jnp.transpose(jnp.broadcast_to(jnp.exp(tot), (GLA_QK, GLA_QK)))
    dcol = jnp.concatenate([dcol, dcol], axis=1)
    kv = lax.dot_general((k * jnp.exp(k_full)).astype(BF16), v, (((0,), (0,)), ((), ())),
                         preferred_element_type=F32)
    return o, s_prev * dcol + kv * bd_s


def _gla_kernel(gla_ref, lg_ref, kt_ref, lgt_ref, a3_ref, a3t_ref, mw_ref, bdc_ref, amat_ref, mask_ref,
                hq_ref, bdv_ref, bds_ref, ind_ref, norm_ref, out_ref, of_ref, ob_ref, *, t_lat, t_all, t_out):
    n2 = 2 * GLA_QK
    o_v = 2 * GLA_QK

    def pieces(rows, d):
        return jnp.concatenate([lg_ref[0, rows, i * n2 + d * GLA_QK:i * n2 + (d + 1) * GLA_QK]
                                for i in range(3)], axis=0)

    cf_ = GLA_FAST_CHUNK
    worst = None
    for ci in range(t_all // cf_):
        tot = jnp.sum(lg_ref[0, ci * cf_:(ci + 1) * cf_, 0:n2].astype(F32), axis=0, keepdims=True)
        worst = tot if worst is None else jnp.minimum(worst, tot)
    safe = jnp.min(worst) >= GLA_SAFE_LOG_DECAY

    @pl.when(safe)
    def _fast():
        c = GLA_FAST_CHUNK
        n_lat = t_lat // c
        n_all = t_all // c
        per_step = GLA_FAST_UNROLL
        assert n_all % per_step == 0

        def job(ci, d):
            rows = pl.ds(pl.multiple_of(ci * c, c), c)
            gt3 = jnp.concatenate([lgt_ref[0, i * n2 + d * GLA_QK:i * n2 + (d + 1) * GLA_QK, rows]
                                   for i in range(3)], axis=1)
            return rows, (d, gla_ref[0, rows, 0:GLA_QK], kt_ref[0, :, rows], gla_ref[0, rows, o_v:o_v + GLA_V],
                          pieces(rows, d), gt3)

        def step(i, carry):
            chunks = [(lax.rem(i * per_step + u + n_lat, n_all), 0) for u in range(per_step)]
            chunks += [(n_all - 1 - (i * per_step + u), 1) for u in range(per_step)]
            rows, jobs = zip(*[job(ci, d) for ci, d in chunks])
            outs, states = _gla_fast_chunks(jobs, a3_ref, a3t_ref, mw_ref, bdc_ref[...], carry)
            for (_, d), r, o in zip(chunks, rows, outs):
                (ob_ref if d else of_ref)[r, :] = o
            return tuple(states)

        zero = jnp.zeros((GLA_QK, 128), F32)
        lax.fori_loop(0, n_all // per_step, step, (zero, zero))

    @pl.when(jnp.logical_not(safe))
    def _robust():
        c = GLA_CHUNK
        n_lat = t_lat // c
        n_all = t_all // c
        hq = hq_ref[...]
        bdv = bdv_ref[...]
        bds = bds_ref[...]

        def one(ci, d, s):
            start = pl.multiple_of(ci * c, c)
            rows = pl.ds(start, c)
            q = gla_ref[0, rows, 0:GLA_QK].astype(F32)
            k = gla_ref[0, rows, GLA_QK:2 * GLA_QK].astype(F32)
            v = gla_ref[0, rows, o_v:o_v + GLA_V]
            o, s = _gla_chunk(q, k, v, pieces(rows, d), amat_ref[d],
                              [mask_ref[d, l] for l in range(len(GLA_LEVELS) + 1)], hq, bdv, bds, s,
                              backward=bool(d))
            (ob_ref if d else of_ref)[rows, :] = o
            return s

        def step(i, carry):
            s_f, s_b = carry
            return one(lax.rem(i + n_lat, n_all), 0, s_f), one(n_all - 1 - i, 1, s_b)

        zero = jnp.zeros((GLA_QK, GLA_V), F32)
        lax.fori_loop(0, n_all, step, (zero, zero))

    tr = 256
    ind = ind_ref[...]
    for r0 in range(0, t_out, tr):
        o = of_ref[r0:r0 + tr, :] + ob_ref[r0:r0 + tr, :]
        sq = o * o
        hi = sq.astype(BF16)
        lo = (sq - hi.astype(F32)).astype(BF16)
        ms = jnp.dot(hi, ind, preferred_element_type=F32) + jnp.dot(lo, ind, preferred_element_type=F32)
        gate = gla_ref[0, r0:r0 + tr, o_v + GLA_V:COL_GLA].astype(F32)
        y = o * lax.rsqrt(ms + 1e-6) * norm_ref[...] * _silu(gate)
        out_ref[0, r0:r0 + tr, :] = y.astype(BF16)


def _gla(gla, lg3, kt, lgt3, consts, norm_t, *, t_lat, t_out):
    bsz, t_all, _ = gla.shape
    n2 = 2 * GLA_QK
    return pl.pallas_call(
        functools.partial(_gla_kernel, t_lat=t_lat, t_all=t_all, t_out=t_out),
        grid=(bsz,),
        in_specs=[pl.BlockSpec((1, t_all, COL_GLA), lambda b: (b, 0, 0)),
                  pl.BlockSpec((1, t_all, 3 * n2), lambda b: (b, 0, 0)),
                  pl.BlockSpec((1, GLA_QK, t_all), lambda b: (b, 0, 0)),
                  pl.BlockSpec((1, 3 * n2, t_all), lambda b: (b, 0, 0))] +
                 [_const_spec(a.shape) for a in consts] + [_const_spec(norm_t.shape)],
        out_specs=pl.BlockSpec((1, t_out, GLA_V), lambda b: (b, 0, 0)),
        out_shape=jax.ShapeDtypeStruct((bsz, t_out, GLA_V), BF16),
        scratch_shapes=[pltpu.VMEM((t_all, GLA_V), F32), pltpu.VMEM((t_all, GLA_V), F32)],
        compiler_params=_cparams("parallel"),
        name="gla",
    )(gla, lg3, kt, lgt3, *consts, norm_t)


def _gmlp_kernel(z_ref, lng_ref, lnb_ref, ws_ref, bs_ref, gm_ref, og_ref, out_ref, *, tm):
    z = z_ref[0].astype(F32)
    zf = 0.5 * z * (1.0 + lax.erf(z * (2.0 ** -0.5)))
    u = zf[:, 0:GMLP_WIDTH]
    v = zf[:, GMLP_WIDTH:]
    mu = jnp.mean(v, axis=-1, keepdims=True)
    vc = v - mu
    v = vc * lax.rsqrt(jnp.mean(vc * vc, axis=-1, keepdims=True) + 1e-5) * lng_ref[...] + lnb_ref[...]
    vb = v.astype(BF16)
    ws = ws_ref[...]
    p = GMLP_CHUNK
    for ci in range(tm // p):
        r = jnp.dot(ws, vb[ci * p:(ci + 1) * p], preferred_element_type=F32) * gm_ref[...]
        mixed = bs_ref[...]
        for g in range(GMLP_GROUPS):
            mixed = mixed + r[g * p:(g + 1) * p]
        y = u[ci * p:(ci + 1) * p] * mixed
        out_ref[0, ci * p:(ci + 1) * p, :] = _rms(y, og_ref[...]).astype(BF16)


def _gmlp(gm, ln_g, ln_b, ws_s, bs_t, gmask, out_g, *, t_out, tm):
    bsz = gm.shape[0]
    return pl.pallas_call(
        functools.partial(_gmlp_kernel, tm=tm),
        grid=(bsz, t_out // tm),
        in_specs=[pl.BlockSpec((1, tm, COL_GM), lambda b, j: (b, j, 0)),
                  _const_spec(ln_g.shape), _const_spec(ln_b.shape), _const_spec(ws_s.shape),
                  _const_spec(bs_t.shape), _const_spec(gmask.shape), _const_spec(out_g.shape)],
        out_specs=pl.BlockSpec((1, tm, GMLP_WIDTH), lambda b, j: (b, j, 0)),
        out_shape=jax.ShapeDtypeStruct((bsz, t_out, GMLP_WIDTH), BF16),
        compiler_params=_cparams("parallel", "parallel"),
        name="gmlp",
    )(gm, ln_g, ln_b, ws_s, bs_t, gmask, out_g)


def _swa_kernel(q_ref, kv_ref, sink_ref, og_ref, out_ref, *, t_lat, t_all):
    w = SWA_WINDOW
    n = pl.program_id(1)
    n_lat = t_lat // w
    lat_q = n < n_lat
    own = jnp.minimum(n, n_lat - 1)
    prev = jnp.maximum(own - 1, 0)
    nxt = jnp.minimum(own + 1, n_lat - 1)

    def rows(blk):
        return pl.ds(pl.multiple_of(blk * w, w), w)

    kv = jnp.concatenate([kv_ref[0, t_lat:t_all, :], kv_ref[0, rows(prev), :], kv_ref[0, rows(own), :],
                          kv_ref[0, rows(nxt), :]], axis=0)
    n_ctx = t_all - t_lat
    tq = lax.broadcasted_iota(jnp.int32, (w, w), 0)
    sk = lax.broadcasted_iota(jnp.int32, (w, w), 1)
    neg = jnp.full((w, w), -jnp.inf, F32)
    zero = jnp.zeros((w, w), F32)
    b_prev = jnp.where((sk >= tq) & jnp.logical_and(lat_q, n >= 1), zero, neg)
    b_own = jnp.where(lat_q, zero, neg)
    b_next = jnp.where((sk <= tq) & (n < n_lat - 1), zero, neg)
    bias = jnp.concatenate([jnp.zeros((w, n_ctx), F32), b_prev, b_own, b_next], axis=1)
    bias = jnp.concatenate([bias] * SWA_REP, axis=0)

    q = q_ref[0]
    outs = []
    for g in range(SWA_KV_HEADS):
        kg = kv[:, g * HEAD_DIM:(g + 1) * HEAD_DIM]
        vg = kv[:, SWA_KV + g * HEAD_DIM:SWA_KV + (g + 1) * HEAD_DIM]
        qg = jnp.concatenate([q[:, (g * SWA_REP + r) * HEAD_DIM:(g * SWA_REP + r + 1) * HEAD_DIM]
                              for r in range(SWA_REP)], axis=0)
        s = lax.dot_general(qg, kg, (((1,), (1,)), ((), ())), preferred_element_type=F32)
        s = s + bias
        sink = jnp.concatenate([jnp.broadcast_to(sink_ref[g * SWA_REP + r:g * SWA_REP + r + 1, 0:1], (w, 1))
                                for r in range(SWA_REP)], axis=0)
        m = jnp.maximum(jnp.max(s, axis=-1, keepdims=True), sink)
        p = jnp.exp(s - m)
        den = jnp.sum(p, axis=-1, keepdims=True) + jnp.exp(sink - m)
        o = jnp.dot(p.astype(BF16), vg, preferred_element_type=F32) / den
        outs += [o[r * w:(r + 1) * w] for r in range(SWA_REP)]
    o = jnp.concatenate(outs, axis=1)
    out_ref[0] = _rms(o, og_ref[...]).astype(BF16)


def _swa(sq, skv, sink_t, out_g, *, t_lat, t_out):
    bsz, t_all, _ = sq.shape
    w = SWA_WINDOW
    return pl.pallas_call(
        functools.partial(_swa_kernel, t_lat=t_lat, t_all=t_all),
        grid=(bsz, t_out // w),
        in_specs=[pl.BlockSpec((1, w, COL_SQ), lambda b, n: (b, n, 0)),
                  pl.BlockSpec((1, t_all, COL_SKV), lambda b, n: (b, 0, 0)),
                  _const_spec(sink_t.shape), _const_spec(out_g.shape)],
        out_specs=pl.BlockSpec((1, w, SWA_Q), lambda b, n: (b, n, 0)),
        out_shape=jax.ShapeDtypeStruct((bsz, t_out, SWA_Q), BF16),
        compiler_params=_cparams("parallel", "parallel"),
        name="swa",
    )(sq, skv, sink_t, out_g)


FFN_CHUNKS = ((0, 1536), (1536, 1280))


def _post_kernel(a_ref, b_ref, c_ref, x_ref, mod_ref, n1_ref, n2a_ref, n2b_ref, wo_ref, wgu_ref, wd_ref,
                 out_ref, *, t_lat, tm, ctx_row):
    b = pl.program_id(0)
    j = pl.program_id(1)
    row = j * tm + lax.broadcasted_iota(jnp.int32, (tm, 1), 0)
    is_ctx = row >= t_lat
    mod = lambda idx: _mod_rows(mod_ref, b, idx, is_ctx, ctx_row)

    cat = jnp.concatenate([a_ref[0], b_ref[0], c_ref[0]], axis=-1)
    y = jnp.dot(cat, wo_ref[...], preferred_element_type=F32)
    x1 = x_ref[0] + mod(2) * _rms(y, n1_ref[...])
    h = (_rms(x1, n2a_ref[...]) * (1.0 + mod(4)) + mod(3)).astype(BF16)
    f = None
    for off, n in FFN_CHUNKS:
        gt = jnp.dot(h, wgu_ref[:, off:off + n], preferred_element_type=F32)
        up = jnp.dot(h, wgu_ref[:, D_FF + off:D_FF + off + n], preferred_element_type=F32)
        act = (_silu(gt) * up).astype(BF16)
        part = jnp.dot(act, wd_ref[off:off + n, :], preferred_element_type=F32)
        f = part if f is None else f + part
    out_ref[0] = x1 + mod(5) * _rms(f, n2b_ref[...])


def _post(a, bo, c, xs, mod, n1_post, n2_pre, n2_post, wo, wgu, wd, *, t_lat, t_out, tm, ctx_row):
    bsz = xs.shape[0]
    tok = lambda n: pl.BlockSpec((1, tm, n), lambda b, j: (b, j, 0))
    once = lambda arr: pl.BlockSpec(arr.shape, lambda b, j: (0,) * arr.ndim, pipeline_mode=pl.Buffered(1))
    return pl.pallas_call(
        functools.partial(_post_kernel, t_lat=t_lat, tm=tm, ctx_row=ctx_row),
        grid=(bsz, t_out // tm),
        in_specs=[tok(GLA_V), tok(GMLP_WIDTH), tok(SWA_Q), tok(D_MODEL), _const_spec(mod.shape),
                  _const_spec(n1_post.shape), _const_spec(n2_pre.shape), _const_spec(n2_post.shape),
                  once(wo), once(wgu), once(wd)],
        out_specs=tok(D_MODEL),
        out_shape=jax.ShapeDtypeStruct((bsz, t_out, D_MODEL), F32),
        compiler_params=_cparams("parallel", "parallel"),
        name="post",
    )(a, bo, c, xs, mod, n1_post, n2_pre, n2_post, wo, wgu, wd)


def _rope_tables(t_lat, t_ctx):
    rows = t_lat // GRID_W
    row = jnp.repeat(jnp.arange(rows), GRID_W).astype(F32)
    col = jnp.tile(jnp.arange(GRID_W), rows).astype(F32)
    inv_freq = jnp.power(ROPE_THETA, -jnp.arange(0, ROPE_AXIS_DIM, 2, dtype=F32) / ROPE_AXIS_DIM)
    ang_row = row[:, None] * inv_freq[None, :]
    ang_col = col[:, None] * inv_freq[None, :]
    ang = jnp.concatenate([ang_row, ang_row, ang_col, ang_col], axis=-1)
    sign = jnp.tile(jnp.concatenate([-jnp.ones((16,), F32), jnp.ones((16,), F32)]), 2)
    cos = jnp.concatenate([jnp.cos(ang), jnp.ones((t_ctx, HEAD_DIM), F32)], axis=0)
    sin = jnp.concatenate([jnp.sin(ang) * sign, jnp.zeros((t_ctx, HEAD_DIM), F32)], axis=0)
    return jnp.tile(cos, (1, 2)), jnp.tile(sin, (1, 2))


def _reorder_w_in(w):
    o_code = 2 * GLA_QK + 2 * GLA_V
    o_gm = o_code + 2 * GLA_GATE_RANK
    o_sq = o_gm + 2 * GMLP_WIDTH
    pad = jnp.zeros((w.shape[0], COL_CODE - 2 * GLA_GATE_RANK), w.dtype)
    return jnp.concatenate([w[:, 0:o_code], w[:, o_gm:o_sq], w[:, o_sq:], w[:, o_code:o_gm], pad],
                           axis=1).astype(BF16)


def _gate_weights(w_in, wa2, ba):
    r = GLA_GATE_RANK
    o_code = 2 * GLA_QK + 2 * GLA_V
    wa = jnp.zeros((COL_CODE, 2 * GLA_QK), F32)
    wa = wa.at[0:r, 0:GLA_QK].set(wa2[0]).at[r:2 * r, GLA_QK:].set(wa2[1])
    wk_t = w_in[:, GLA_QK:2 * GLA_QK].T
    wc_t = jnp.zeros((COL_CODE, w_in.shape[0]), F32).at[0:2 * r].set(w_in[:, o_code:o_code + 2 * r].T)
    return (wa.astype(BF16), ba.reshape(1, 2 * GLA_QK), wk_t.astype(BF16), wc_t.astype(BF16),
            wa.T.astype(BF16), ba.reshape(2 * GLA_QK, 1))


def _gla_consts():
    a3, a3t_neg, mask_w, bd_c = _gla_fast_tables()
    amat, mask = _gla_tables()
    lane_head = np.arange(GLA_QK)[None, :] // GLA_DK
    row_head = np.arange(GLA_HEADS * GLA_CHUNK)[:, None] // GLA_CHUNK
    hq = (lane_head == row_head).astype(np.float32)
    vlane_head = np.arange(GLA_V)[None, :] // GLA_DV
    bdv = (vlane_head == row_head).astype(np.float32)
    bds = (vlane_head == (np.arange(GLA_QK)[:, None] // GLA_DK)).astype(np.float32)
    ind = (vlane_head == vlane_head.T).astype(np.float32) / GLA_DV
    return (jnp.asarray(a3, BF16), jnp.asarray(a3t_neg, BF16), jnp.asarray(mask_w, BF16),
            jnp.asarray(bd_c, F32), jnp.asarray(amat, BF16), jnp.asarray(mask, F32), jnp.asarray(hq, BF16),
            jnp.asarray(bdv, F32), jnp.asarray(bds, F32), jnp.asarray(ind, BF16))


def kernel(x, c, ctx, c_ctx, mod_w, mod_b, n1_pre, n1_post, n2_pre, n2_post, w_in, w_out, gla_wa2, gla_ba,
           gla_norm, gmlp_ln_g, gmlp_ln_b, gmlp_ws, gmlp_bs, gmlp_out_g, swa_sink, swa_out_g, ffn_w_gu,
           ffn_w_down):
    bsz, t_lat, d = x.shape
    t_ctx = ctx.shape[1]
    t_all = t_lat + t_ctx
    depth = mod_w.shape[0]
    assert d == D_MODEL and bsz < MOD_ROWS
    assert t_lat % 1024 == 0 and t_ctx % 256 == 0 and t_all % 768 == 0
    ctx_row = bsz
    tm_all = 768

    cc = jnp.zeros((MOD_ROWS, d), F32).at[0:bsz].set(c).at[ctx_row].set(c_ctx)
    mods = _modulation(cc, mod_w, mod_b)
    cos, sin = _rope_tables(t_lat, t_ctx)
    gla_consts = _gla_consts()
    gmask = jnp.asarray((np.arange(GMLP_WIDTH)[None, :] // GMLP_GDIM ==
                         np.arange(GMLP_GROUPS * GMLP_CHUNK)[:, None] // GMLP_CHUNK).astype(np.float32))
    row = lambda v: v.reshape(1, -1)

    xs = jnp.concatenate([x, ctx], axis=1)
    for l in range(depth):
        last = l == depth - 1
        t_out = t_lat if last else t_all
        gla, lg3, gm, sq, skv, kt, lgt3 = _inproj(
            xs, mods[l], row(n1_pre[l]), _reorder_w_in(w_in[l]), _gate_weights(w_in[l], gla_wa2[l], gla_ba[l]),
            cos, sin, t_lat=t_lat, tm=tm_all, ctx_row=ctx_row)
        a_out = _gla(gla, lg3, kt, lgt3, gla_consts, row(jnp.tile(gla_norm[l], GLA_HEADS)),
                     t_lat=t_lat, t_out=t_out)
        ws_s = gmlp_ws[l].reshape(GMLP_GROUPS * GMLP_CHUNK, GMLP_CHUNK).astype(BF16)
        bs_t = jnp.repeat(gmlp_bs[l].T, GMLP_GDIM, axis=1)
        b_out = _gmlp(gm, row(gmlp_ln_g[l]), row(gmlp_ln_b[l]), ws_s, bs_t, gmask, row(gmlp_out_g[l]),
                      t_out=t_out, tm=256)
        sink_t = jnp.broadcast_to(swa_sink[l][:, None], (SWA_HEADS, 128))
        c_out = _swa(sq, skv, sink_t, row(swa_out_g[l]), t_lat=t_lat, t_out=t_out)
        xs = _post(a_out, b_out, c_out, xs, mods[l], row(n1_post[l]), row(n2_pre[l]), row(n2_post[l]),
                   w_out[l].astype(BF16), ffn_w_gu[l].astype(BF16), ffn_w_down[l].astype(BF16),
                   t_lat=t_lat, t_out=t_out, tm=1024 if last else tm_all, ctx_row=ctx_row)
    return xs
```

```python
import functools

import numpy as np
import jax
import jax.numpy as jnp
from jax import lax
from jax.experimental import pallas as pl
from jax.experimental.pallas import tpu as pltpu

F32 = jnp.float32
BF16 = jnp.bfloat16

D_MODEL = 1024
GRID_W = 64
HEAD_DIM = 64
GLA_HEADS = 4
GLA_DK = 32
GLA_DV = 64
GLA_QK = GLA_HEADS * GLA_DK
GLA_V = GLA_HEADS * GLA_DV
GLA_GATE_RANK = 16
GLA_GATE_TAU = 16.0
GLA_CHUNK = 64
GMLP_GROUPS = 4
GMLP_GDIM = 64
GMLP_WIDTH = GMLP_GROUPS * GMLP_GDIM
GMLP_CHUNK = 128
SWA_HEADS = 8
SWA_KV_HEADS = 2
SWA_REP = SWA_HEADS // SWA_KV_HEADS
SWA_Q = SWA_HEADS * HEAD_DIM
SWA_KV = SWA_KV_HEADS * HEAD_DIM
SWA_WINDOW = 128
ROPE_AXIS_DIM = HEAD_DIM // 2
ROPE_THETA = 10000.0
MIX_WIDTH = GLA_V + GMLP_WIDTH + SWA_Q
D_FF = -(-8 * D_MODEL // (3 * 256)) * 256
N_MOD = 6
MOD_ROWS = 24

COL_GLA = 2 * GLA_QK + 2 * GLA_V
COL_GM = 2 * GMLP_WIDTH
COL_SK = SWA_KV
COL_CODE = 128
OFF_GM = COL_GLA
OFF_SK = OFF_GM + COL_GM
OFF_CODE = OFF_SK + COL_SK
ROW_SQ = 0
ROW_SV = ROW_SQ + SWA_Q
ROW_GK = ROW_SV + SWA_KV
ROW_CODE = ROW_GK + GLA_QK
ROWS_T = ROW_CODE + COL_CODE

LOG2E = 1.4426950408889634
SWA_Q_SCALE = HEAD_DIM ** -0.5 * LOG2E
SWA_KEY_TILE = 16
SWA_DEN_ROWS = 16

VMEM_LIMIT = 56 * 1024 * 1024
GLA_LEVELS = (1, 2, 4, 8, 16, 32)


def _cparams(*sem):
    return pltpu.CompilerParams(dimension_semantics=sem, vmem_limit_bytes=VMEM_LIMIT)


def _const_spec(shape):
    nd = len(shape)
    return pl.BlockSpec(shape, lambda *_: (0,) * nd)


def _rms(x, g, eps=1e-6):
    return x * lax.rsqrt(jnp.mean(x * x, axis=-1, keepdims=True) + eps) * g


def _silu(x):
    return x * jax.nn.sigmoid(x)


def _row_split(t_lat, tm):
    return t_lat % tm if t_lat % tm else tm


def _mod_vectors(mod_ref, b, j, idx, *, t_lat, tm, ctx_row):
    lo = idx * D_MODEL
    lat = mod_ref[pl.ds(b, 1), lo:lo + D_MODEL]
    ctx = mod_ref[ctx_row:ctx_row + 1, lo:lo + D_MODEL]
    return lat, jnp.where(j * tm + _row_split(t_lat, tm) >= t_lat, ctx, lat)


def _by_rows(fn, split, *arrays):
    tm = arrays[0].shape[0]
    if split == tm:
        return fn(0, *arrays)
    return jnp.concatenate([fn(0, *[a[0:split] for a in arrays]), fn(1, *[a[split:tm] for a in arrays])], axis=0)


def _mod_kernel(c_ref, w_ref, b_ref, o_ref):
    s = _silu(c_ref[...]).astype(BF16)
    o_ref[0] = jnp.dot(s, w_ref[0].astype(BF16), preferred_element_type=F32) + b_ref[0]


def _modulation(cc, mod_w, mod_b):
    depth, d, n = mod_w.shape
    tn = n // 4
    return pl.pallas_call(
        _mod_kernel,
        grid=(depth, n // tn),
        in_specs=[pl.BlockSpec((MOD_ROWS, d), lambda l, j: (0, 0)),
                  pl.BlockSpec((1, d, tn), lambda l, j: (l, 0, j)),
                  pl.BlockSpec((1, 1, tn), lambda l, j: (l, 0, j))],
        out_specs=pl.BlockSpec((1, MOD_ROWS, tn), lambda l, j: (l, 0, j)),
        out_shape=jax.ShapeDtypeStruct((depth, MOD_ROWS, n), F32),
        compiler_params=_cparams("parallel", "parallel"),
        name="modulation",
    )(cc, mod_w, mod_b.reshape(depth, 1, n))


def _rope(z, cos, sin):
    lane = lax.broadcasted_iota(jnp.int32, z.shape, 1)
    first = (lane % 32) < 16
    rot = jnp.where(first, pltpu.roll(z, 128 - 16, 1), pltpu.roll(z, 16, 1))
    return z * cos + rot * sin


def _split3(x):
    p1 = x.astype(BF16)
    r1 = x - p1.astype(F32)
    p2 = r1.astype(BF16)
    p3 = (r1 - p2.astype(F32)).astype(BF16)
    return p1, p2, p3


def _log_gate(z):
    return jax.nn.log_sigmoid(z) * (1.0 / GLA_GATE_TAU)


def _rope_t(z, cos_t, sin_t):
    parts = []
    for r0 in range(0, z.shape[0], 32):
        parts += [z[r0 + 16:r0 + 32], z[r0:r0 + 16]]
    rot = jnp.concatenate(parts, axis=0)
    reps = z.shape[0] // cos_t.shape[0]
    return z * jnp.concatenate([cos_t] * reps, axis=0) + rot * jnp.concatenate([sin_t] * reps, axis=0)


def _inproj_kernel(x_ref, mod_ref, g_ref, w_ref, wt_ref, wa_ref, ba_ref, wat_ref, bat_ref,
                   cos_ref, sin_ref, cost_ref, sint_ref,
                   gla_ref, lg_ref, gm_ref, sk_ref, sqt_ref, svt_ref, kt_ref, lgt_ref, *, t_lat, tm, ctx_row):
    b = pl.program_id(0)
    j = pl.program_id(1)
    mod = functools.partial(_mod_vectors, mod_ref, b, j, t_lat=t_lat, tm=tm, ctx_row=ctx_row)
    shift, scale, gain = mod(0), mod(1), g_ref[...]
    h = _by_rows(lambda part, x: (_rms(x, gain) * (1.0 + scale[part]) + shift[part]).astype(BF16),
                 _row_split(t_lat, tm), x_ref[0])

    def proj(off, n):
        return jnp.dot(h, w_ref[:, off:off + n], preferred_element_type=F32)

    def proj_t(off, n):
        return lax.dot_general(wt_ref[off:off + n, :], h, (((1,), (1,)), ((), ())), preferred_element_type=F32)

    z = proj(0, COL_GLA)
    gla_ref[0, :, 0:GLA_QK] = (z[:, 0:GLA_QK] * (GLA_DK ** -0.5)).astype(BF16)
    gla_ref[0, :, GLA_QK:COL_GLA] = z[:, GLA_QK:COL_GLA].astype(BF16)
    kt_ref[0] = proj_t(ROW_GK, GLA_QK).astype(BF16)

    gm_ref[0] = proj(OFF_GM, COL_GM).astype(BF16)

    sk_ref[0] = _rope(proj(OFF_SK, COL_SK), cos_ref[...], sin_ref[...]).astype(BF16)
    sqt_ref[0] = (_rope_t(proj_t(ROW_SQ, SWA_Q), cost_ref[...], sint_ref[...]) * SWA_Q_SCALE).astype(BF16)
    svt_ref[0] = proj_t(ROW_SV, SWA_KV).astype(BF16)

    n2 = 2 * GLA_QK
    code = proj(OFF_CODE, COL_CODE).astype(BF16)
    lg = _log_gate(jnp.dot(code, wa_ref[...], preferred_element_type=F32) + ba_ref[...])
    for i, p in enumerate(_split3(lg)):
        lg_ref[0, :, i * n2:(i + 1) * n2] = p
    code_t = proj_t(ROW_CODE, COL_CODE).astype(BF16)
    lgt = _log_gate(jnp.dot(wat_ref[...], code_t, preferred_element_type=F32) + bat_ref[...])
    for i, p in enumerate(_split3(lgt)):
        lgt_ref[0, i * n2:(i + 1) * n2, :] = p


def _inproj(xs, mod, n_pre, weights, rope, *, t_lat, tm, ctx_row):
    bsz, t_all, d = xs.shape
    nt = t_all // tm
    tok = lambda n: pl.BlockSpec((1, tm, n), lambda b, j: (b, j, 0))
    tok_t = lambda n: pl.BlockSpec((1, n, tm), lambda b, j: (b, 0, j))
    n2 = 2 * GLA_QK
    cols = (COL_GLA, 3 * n2, COL_GM, COL_SK)
    rows = (SWA_Q, SWA_KV, GLA_QK, 3 * n2)
    out_shape = [jax.ShapeDtypeStruct((bsz, t_all, n), BF16) for n in cols]
    out_shape += [jax.ShapeDtypeStruct((bsz, n, t_all), BF16) for n in rows]
    consts = (mod, n_pre) + tuple(weights)
    cos, sin, cos_t, sin_t = rope
    return pl.pallas_call(
        functools.partial(_inproj_kernel, t_lat=t_lat, tm=tm, ctx_row=ctx_row),
        grid=(bsz, nt),
        in_specs=[tok(d)] + [_const_spec(a.shape) for a in consts] +
                 [pl.BlockSpec((tm, 128), lambda b, j: (j, 0)), pl.BlockSpec((tm, 128), lambda b, j: (j, 0)),
                  pl.BlockSpec((128, tm), lambda b, j: (0, j)), pl.BlockSpec((128, tm), lambda b, j: (0, j))],
        out_specs=[tok(n) for n in cols] + [tok_t(n) for n in rows],
        out_shape=out_shape,
        compiler_params=_cparams("parallel", "parallel"),
        name="inproj",
    )(xs, *consts, cos, sin, cos_t, sin_t)


GLA_FAST_CHUNK = 128
GLA_FAST_UNROLL = 2
GLA_SAFE_LOG_DECAY = -40.0


def _gla_fast_tables():
    c = GLA_FAST_CHUNK
    t = np.arange(c)[:, None]
    r = np.arange(c)[None, :]
    incl = [(r <= t), (r >= t)]
    a3 = np.stack([np.tile(m, (1, 3)) for m in incl]).astype(np.float32)
    a3t_neg = -np.stack([np.tile(m.T, (3, 1)) for m in incl]).astype(np.float32)
    mask_w = np.stack([np.tile(m, (1, GLA_HEADS)) for m in incl]).astype(np.float32)
    half = (np.arange(128)[None, :] // GLA_DV) == ((np.arange(GLA_QK)[:, None] // GLA_DK) % 2)
    return a3, a3t_neg, mask_w, half.astype(np.float32)


def _gla_fast_chunks(jobs, a3, a3t_neg, mask_w, bd_c, states):
    c = GLA_FAST_CHUNK
    dk, hd = GLA_DK, GLA_QK
    cums = [(jnp.dot(a3[d], g3, preferred_element_type=F32),
             jnp.dot(gt3, a3t_neg[d], preferred_element_type=F32))
            for d, _, _, _, g3, gt3 in jobs]

    zr = lambda n: jnp.zeros((n, c), BF16)
    qes, dcols, khs, atts = [], [], [], []
    for (d, q, kt, _, _, _), (cum, ncum_t) in zip(jobs, cums):
        qe = (q.astype(F32) * jnp.exp(cum)).astype(BF16)
        ke_t = kt.astype(F32) * jnp.exp(ncum_t)
        edge = 0 if d else c - 1
        dcol = jnp.exp(-ncum_t[:, edge:edge + 1])
        khs.append((ke_t * dcol).astype(BF16))
        ke_t = ke_t.astype(BF16)
        cols = []
        for hh in range(GLA_HEADS):
            parts = ([zr(hh * dk)] if hh else []) + [ke_t[hh * dk:(hh + 1) * dk]]
            parts += [zr(hd - (hh + 1) * dk)] if hh < GLA_HEADS - 1 else []
            cols.append(jnp.concatenate(parts, axis=0))
        kbd = jnp.concatenate(cols, axis=1)
        atts.append(jnp.dot(qe, kbd, preferred_element_type=F32))
        qes.append(qe)
        dcols.append(dcol)

    kvs = [jnp.concatenate([jnp.dot(kh[0:hd // 2], v[:, 0:128], preferred_element_type=F32),
                            jnp.dot(kh[hd // 2:hd], v[:, 128:256], preferred_element_type=F32)], axis=0)
           for (_, _, _, v, _, _), kh in zip(jobs, khs)]

    lane = lax.broadcasted_iota(jnp.int32, (1, 128), 1)
    m_lo = (lane < GLA_DV).astype(BF16)
    m_hi = (lane >= GLA_DV).astype(BF16)
    z128 = jnp.zeros((c, 128), BF16)
    z64 = jnp.zeros((hd // 2, 128), BF16)
    states = list(states)
    outs = []
    for (d, _, _, v, _, _), qe, dcol, att, kv in zip(jobs, qes, dcols, atts, kvs):
        v_lo, v_hi = v[:, 0:128], v[:, 128:256]
        vbd = jnp.concatenate([jnp.concatenate([v_lo * m_lo, z128], axis=1),
                               jnp.concatenate([v_lo * m_hi, z128], axis=1),
                               jnp.concatenate([z128, v_hi * m_lo], axis=1),
                               jnp.concatenate([z128, v_hi * m_hi], axis=1)], axis=0)
        sb = states[d].astype(BF16)
        s_full = jnp.concatenate([jnp.concatenate([sb[0:hd // 2], z64], axis=1),
                                  jnp.concatenate([z64, sb[hd // 2:hd]], axis=1)], axis=0)
        lhs = jnp.concatenate([att.astype(BF16) * mask_w[d], qe], axis=1)
        outs.append(jnp.dot(lhs, jnp.concatenate([vbd, s_full], axis=0),
                            preferred_element_type=F32))
        states[d] = states[d] * dcol + kv * bd_c
    return outs, states


def _gla_tables():
    c = GLA_CHUNK
    t = np.arange(c)[:, None]
    r = np.arange(c)[None, :]
    sizes = [2 * b for b in GLA_LEVELS]
    same = lambda b: (t // b) == (r // b)
    a_f = [same(b) & (r <= t) for b in sizes] + [same(b) & (r > t) for b in sizes]
    a_b = [same(b) & (r >= t) for b in sizes] + [same(b) & (r < t) for b in sizes]
    m_f = [((t // b) % 2 == 1) & ((r // b) == (t // b) - 1) for b in GLA_LEVELS] + [t == r]
    m_b = [((t // b) % 2 == 0) & ((r // b) == (t // b) + 1) for b in GLA_LEVELS] + [t == r]
    amat = np.stack([np.tile(np.concatenate(a, 0), (1, 3)) for a in (a_f, a_b)]).astype(np.float32)
    mask = np.stack([np.stack([np.tile(m, (GLA_HEADS, 1)) for m in ms]) for ms in (m_f, m_b)])
    return amat, mask.astype(np.float32)


def _gla_chunk(q, k, v, g3, amat, masks, hmask_q, bd_v, bd_s, s_prev, *, backward):
    c = GLA_CHUNK
    nl = len(GLA_LEVELS)
    g = g3[0:c].astype(F32) + g3[c:2 * c].astype(F32) + g3[2 * c:3 * c].astype(F32)
    ps = jnp.dot(amat, g3, preferred_element_type=F32)
    blk = lambda i: ps[i * c:(i + 1) * c]
    q_exp = [g] + [blk(i) for i in range(nl - 1)]
    k_exp = [None] + [blk(nl + i) for i in range(nl - 1)]
    q_full = blk(nl - 1)
    k_full = blk(2 * nl - 1)

    att = None
    for lvl in range(nl + 1):
        if lvl < nl:
            qe = q * jnp.exp(q_exp[lvl])
            ke = k if k_exp[lvl] is None else k * jnp.exp(k_exp[lvl])
        else:
            qe, ke = q, k
        qbd = jnp.concatenate([qe.astype(BF16)] * GLA_HEADS, axis=0) * hmask_q
        p = lax.dot_general(qbd, ke.astype(BF16), (((1,), (1,)), ((), ())), preferred_element_type=F32)
        p = p * masks[lvl]
        att = p if att is None else att + p

    r = jnp.dot(att.astype(BF16), v, preferred_element_type=F32) * bd_v
    o = r[0:c]
    for hh in range(1, GLA_HEADS):
        o = o + r[hh * c:(hh + 1) * c]
    o = o + jnp.dot((q * jnp.exp(q_full)).astype(BF16), s_prev.astype(BF16), preferred_element_type=F32)

    tot = q_full[0:1] if backward else q_full[c - 1:c]
    dcol = jnp.transpose(jnp.broadcast_to(jnp.exp(tot), (GLA_QK, GLA_QK)))
    dcol = jnp.concatenate([dcol, dcol], axis=1)
    kv = lax.dot_general((k * jnp.exp(k_full)).astype(BF16), v, (((0,), (0,)), ((), ())),
                         preferred_element_type=F32)
    return o, s_prev * dcol + kv * bd_s


def _gla_kernel(gla_ref, lg_ref, kt_ref, lgt_ref, a3_ref, a3t_ref, mw_ref, bdc_ref, amat_ref, mask_ref,
                hq_ref, bdv_ref, bds_ref, ind_ref, norm_ref, out_ref, of_ref, ob_ref, *, t_lat, t_all, t_out):
    n2 = 2 * GLA_QK
    o_v = 2 * GLA_QK

    def pieces(rows, d):
        return jnp.concatenate([lg_ref[0, rows, i * n2 + d * GLA_QK:i * n2 + (d + 1) * GLA_QK]
                                for i in range(3)], axis=0)

    cf_ = GLA_FAST_CHUNK
    worst = None
    for ci in range(t_all // cf_):
        tot = jnp.sum(lg_ref[0, ci * cf_:(ci + 1) * cf_, 0:n2].astype(F32), axis=0, keepdims=True)
        worst = tot if worst is None else jnp.minimum(worst, tot)
    safe = jnp.min(worst) >= GLA_SAFE_LOG_DECAY

    @pl.when(safe)
    def _fast():
        c = GLA_FAST_CHUNK
        n_lat = t_lat // c
        n_all = t_all // c
        per_step = GLA_FAST_UNROLL
        assert n_all % per_step == 0

        def job(ci, d):
            rows = pl.ds(pl.multiple_of(ci * c, c), c)
            gt3 = jnp.concatenate([lgt_ref[0, i * n2 + d * GLA_QK:i * n2 + (d + 1) * GLA_QK, rows]
                                   for i in range(3)], axis=1)
            return rows, (d, gla_ref[0, rows, 0:GLA_QK], kt_ref[0, :, rows], gla_ref[0, rows, o_v:o_v + GLA_V],
                          pieces(rows, d), gt3)

        def step(i, carry):
            chunks = [(lax.rem(i * per_step + u + n_lat, n_all), 0) for u in range(per_step)]
            chunks += [(n_all - 1 - (i * per_step + u), 1) for u in range(per_step)]
            rows, jobs = zip(*[job(ci, d) for ci, d in chunks])
            outs, states = _gla_fast_chunks(jobs, a3_ref, a3t_ref, mw_ref, bdc_ref[...], carry)
            for (_, d), r, o in zip(chunks, rows, outs):
                (ob_ref if d else of_ref)[r, :] = o
            return tuple(states)

        zero = jnp.zeros((GLA_QK, 128), F32)
        lax.fori_loop(0, n_all // per_step, step, (zero, zero))

    @pl.when(jnp.logical_not(safe))
    def _robust():
        c = GLA_CHUNK
        n_lat = t_lat // c
        n_all = t_all // c
        hq = hq_ref[...]
        bdv = bdv_ref[...]
        bds = bds_ref[...]

        def one(ci, d, s):
            start = pl.multiple_of(ci * c, c)
            rows = pl.ds(start, c)
            q = gla_ref[0, rows, 0:GLA_QK].astype(F32)
            k = gla_ref[0, rows, GLA_QK:2 * GLA_QK].astype(F32)
            v = gla_ref[0, rows, o_v:o_v + GLA_V]
            o, s = _gla_chunk(q, k, v, pieces(rows, d), amat_ref[d],
                              [mask_ref[d, l] for l in range(len(GLA_LEVELS) + 1)], hq, bdv, bds, s,
                              backward=bool(d))
            (ob_ref if d else of_ref)[rows, :] = o
            return s

        def step(i, carry):
            s_f, s_b = carry
            return one(lax.rem(i + n_lat, n_all), 0, s_f), one(n_all - 1 - i, 1, s_b)

        zero = jnp.zeros((GLA_QK, GLA_V), F32)
        lax.fori_loop(0, n_all, step, (zero, zero))

    tr = 256
    ind = ind_ref[...]
    for r0 in range(0, t_out, tr):
        o = of_ref[r0:r0 + tr, :] + ob_ref[r0:r0 + tr, :]
        sq = o * o
        hi = sq.astype(BF16)
        lo = (sq - hi.astype(F32)).astype(BF16)
        ms = jnp.dot(hi, ind, preferred_element_type=F32) + jnp.dot(lo, ind, preferred_element_type=F32)
        gate = gla_ref[0, r0:r0 + tr, o_v + GLA_V:COL_GLA].astype(F32)
        y = o * lax.rsqrt(ms + 1e-6) * norm_ref[...] * _silu(gate)
        out_ref[0, r0:r0 + tr, :] = y.astype(BF16)


def _gla(gla, lg3, kt, lgt3, consts, norm_t, *, t_lat, t_out):
    bsz, t_all, _ = gla.shape
    n2 = 2 * GLA_QK
    return pl.pallas_call(
        functools.partial(_gla_kernel, t_lat=t_lat, t_all=t_all, t_out=t_out),
        grid=(bsz,),
        in_specs=[pl.BlockSpec((1, t_all, COL_GLA), lambda b: (b, 0, 0)),
                  pl.BlockSpec((1, t_all, 3 * n2), lambda b: (b, 0, 0)),
                  pl.BlockSpec((1, GLA_QK, t_all), lambda b: (b, 0, 0)),
                  pl.BlockSpec((1, 3 * n2, t_all), lambda b: (b, 0, 0))] +
                 [_const_spec(a.shape) for a in consts] + [_const_spec(norm_t.shape)],
        out_specs=pl.BlockSpec((1, t_out, GLA_V), lambda b: (b, 0, 0)),
        out_shape=jax.ShapeDtypeStruct((bsz, t_out, GLA_V), BF16),
        scratch_shapes=[pltpu.VMEM((t_all, GLA_V), F32), pltpu.VMEM((t_all, GLA_V), F32)],
        compiler_params=_cparams("parallel"),
        name="gla",
    )(gla, lg3, kt, lgt3, *consts, norm_t)


def _gmlp_kernel(z_ref, lng_ref, lnb_ref, ws_ref, bs_ref, gm_ref, og_ref, out_ref, *, tm):
    z = z_ref[0].astype(F32)
    zf = 0.5 * z * (1.0 + lax.erf(z * (2.0 ** -0.5)))
    u = zf[:, 0:GMLP_WIDTH]
    v = zf[:, GMLP_WIDTH:]
    mu = jnp.mean(v, axis=-1, keepdims=True)
    vc = v - mu
    v = vc * lax.rsqrt(jnp.mean(vc * vc, axis=-1, keepdims=True) + 1e-5) * lng_ref[...] + lnb_ref[...]
    vb = v.astype(BF16)
    ws = ws_ref[...]
    p = GMLP_CHUNK
    for ci in range(tm // p):
        r = jnp.dot(ws, vb[ci * p:(ci + 1) * p], preferred_element_type=F32) * gm_ref[...]
        mixed = bs_ref[...]
        for g in range(GMLP_GROUPS):
            mixed = mixed + r[g * p:(g + 1) * p]
        y = u[ci * p:(ci + 1) * p] * mixed
        out_ref[0, ci * p:(ci + 1) * p, :] = _rms(y, og_ref[...]).astype(BF16)


def _gmlp(gm, ln_g, ln_b, ws_s, bs_t, gmask, out_g, *, t_out, tm):
    bsz = gm.shape[0]
    return pl.pallas_call(
        functools.partial(_gmlp_kernel, tm=tm),
        grid=(bsz, t_out // tm),
        in_specs=[pl.BlockSpec((1, tm, COL_GM), lambda b, j: (b, j, 0)),
                  _const_spec(ln_g.shape), _const_spec(ln_b.shape), _const_spec(ws_s.shape),
                  _const_spec(bs_t.shape), _const_spec(gmask.shape), _const_spec(out_g.shape)],
        out_specs=pl.BlockSpec((1, tm, GMLP_WIDTH), lambda b, j: (b, j, 0)),
        out_shape=jax.ShapeDtypeStruct((bsz, t_out, GMLP_WIDTH), BF16),
        compiler_params=_cparams("parallel", "parallel"),
        name="gmlp",
    )(gm, ln_g, ln_b, ws_s, bs_t, gmask, out_g)


def _swa_attend(qt_ref, k_ref, vt_ref, sink_ref, og_ref, eye_ref, out_ref, s_ref, p_ref, key_rows, biases):
    w = SWA_WINDOW
    d = HEAD_DIM
    kt = SWA_KEY_TILE
    lanes = SWA_HEADS * w
    keys = jnp.concatenate([k_ref[0, pl.ds(start, size), :] for start, size in key_rows], axis=0)
    nk = keys.shape[0]

    zero = jnp.zeros((d, w), BF16)
    qbd = jnp.concatenate([
        jnp.concatenate([qt_ref[0, h * d:(h + 1) * d, :] if h // SWA_REP == g else zero for h in range(SWA_HEADS)],
                        axis=1) for g in range(SWA_KV_HEADS)], axis=0)
    s_ref[0:nk, :] = jnp.dot(keys, qbd, preferred_element_type=F32)

    def tiles():
        off = 0
        for (_, size), bias in zip(key_rows, biases):
            for r0 in range(0, size, kt):
                t = s_ref[off + r0:off + r0 + kt, :]
                if bias is not None:
                    t = t + jnp.concatenate([bias[r0:r0 + kt]] * SWA_HEADS, axis=1)
                yield off + r0, t
            off += size

    sink = sink_ref[...] * LOG2E
    best = None
    for _, t in tiles():
        best = t if best is None else jnp.maximum(best, t)
    top = jnp.maximum(jnp.max(best, axis=0, keepdims=True), sink)
    for r0, t in tiles():
        p_ref[r0:r0 + kt, :] = jnp.exp2(t - top).astype(BF16)

    vt = jnp.concatenate([vt_ref[0, :, pl.ds(start, size)] for start, size in key_rows], axis=1)
    lhs = jnp.concatenate([vt, jnp.ones((SWA_DEN_ROWS, nk), BF16)], axis=0)
    o_ext = jnp.dot(lhs, p_ref[0:nk, :], preferred_element_type=F32)
    inv = 1.0 / (o_ext[SWA_KV:SWA_KV + 1, :] + jnp.exp2(sink - top))
    o_t = jnp.concatenate([o_ext[(h // SWA_REP) * d:(h // SWA_REP + 1) * d, h * w:(h + 1) * w] *
                           inv[:, h * w:(h + 1) * w] for h in range(SWA_HEADS)], axis=0)
    y_t = o_t * lax.rsqrt(jnp.mean(o_t * o_t, axis=0, keepdims=True) + 1e-6) * og_ref[...]
    out = lax.dot_general(eye_ref[...], y_t.astype(BF16), (((1,), (1,)), ((), ())), preferred_element_type=F32)
    out_ref[0] = out.astype(BF16)


def _swa_kernel(qt_ref, k_ref, vt_ref, sink_ref, og_ref, eye_ref, out_ref, s_ref, p_ref, *, t_lat, t_all, t_out):
    w = SWA_WINDOW
    n = pl.program_id(1)
    n_lat = t_lat // w
    ctx_rows = (t_lat, t_all - t_lat)
    attend = functools.partial(_swa_attend, qt_ref, k_ref, vt_ref, sink_ref, og_ref, eye_ref, out_ref, s_ref, p_ref)

    @pl.when(n < n_lat)
    def _latent():
        blk = lambda i: (pl.multiple_of(i * w, w), w)
        sk = lax.broadcasted_iota(jnp.int32, (w, w), 0)
        tq = lax.broadcasted_iota(jnp.int32, (w, w), 1)
        neg = jnp.full((w, w), -jnp.inf, F32)
        zero = jnp.zeros((w, w), F32)
        b_prev = jnp.where((sk >= tq) & (n >= 1), zero, neg)
        b_next = jnp.where((sk <= tq) & (n < n_lat - 1), zero, neg)
        attend([ctx_rows, blk(jnp.maximum(n - 1, 0)), blk(n), blk(jnp.minimum(n + 1, n_lat - 1))],
               [None, b_prev, None, b_next])

    if t_out > t_lat:
        @pl.when(n >= n_lat)
        def _context():
            attend([ctx_rows], [None])


def _swa(sqt, sk, svt, sink_t, out_g, eye, *, t_lat, t_out):
    bsz, t_all, _ = sk.shape
    w = SWA_WINDOW
    n_keys = t_all - t_lat + 3 * w
    return pl.pallas_call(
        functools.partial(_swa_kernel, t_lat=t_lat, t_all=t_all, t_out=t_out),
        grid=(bsz, t_out // w),
        in_specs=[pl.BlockSpec((1, SWA_Q, w), lambda b, n: (b, 0, n)),
                  pl.BlockSpec((1, t_all, SWA_KV), lambda b, n: (b, 0, 0)),
                  pl.BlockSpec((1, SWA_KV, t_all), lambda b, n: (b, 0, 0)),
                  _const_spec(sink_t.shape), _const_spec(out_g.shape), _const_spec(eye.shape)],
        out_specs=pl.BlockSpec((1, w, SWA_Q), lambda b, n: (b, n, 0)),
        out_shape=jax.ShapeDtypeStruct((bsz, t_out, SWA_Q), BF16),
        scratch_shapes=[pltpu.VMEM((n_keys, SWA_HEADS * w), F32), pltpu.VMEM((n_keys, SWA_HEADS * w), BF16)],
        compiler_params=_cparams("parallel", "parallel"),
        name="swa",
    )(sqt, sk, svt, sink_t, out_g, eye)


FFN_CHUNKS = ((0, 1536), (1536, 1280))


def _post_kernel(a_ref, b_ref, c_ref, x_ref, mod_ref, n1_ref, n2a_ref, n2b_ref, wo_ref, wgu_ref, wd_ref,
                 out_ref, *, t_lat, tm, ctx_row):
    b = pl.program_id(0)
    j = pl.program_id(1)
    mod = functools.partial(_mod_vectors, mod_ref, b, j, t_lat=t_lat, tm=tm, ctx_row=ctx_row)
    split = _row_split(t_lat, tm)
    gate1, shift2, scale2, gate2 = mod(2), mod(3), mod(4), mod(5)
    n1, n2a, n2b = n1_ref[...], n2a_ref[...], n2b_ref[...]

    cat = jnp.concatenate([a_ref[0], b_ref[0], c_ref[0]], axis=-1)
    y = jnp.dot(cat, wo_ref[...], preferred_element_type=F32)
    x1 = _by_rows(lambda p, x, y: x + gate1[p] * _rms(y, n1), split, x_ref[0], y)
    h = _by_rows(lambda p, x: (_rms(x, n2a) * (1.0 + scale2[p]) + shift2[p]).astype(BF16), split, x1)
    f = None
    for off, n in FFN_CHUNKS:
        gt = jnp.dot(h, wgu_ref[:, off:off + n], preferred_element_type=F32)
        up = jnp.dot(h, wgu_ref[:, D_FF + off:D_FF + off + n], preferred_element_type=F32)
        act = (_silu(gt) * up).astype(BF16)
        part = jnp.dot(act, wd_ref[off:off + n, :], preferred_element_type=F32)
        f = part if f is None else f + part
    out_ref[0] = _by_rows(lambda p, x, f: x + gate2[p] * _rms(f, n2b), split, x1, f)


def _post(a, bo, c, xs, mod, n1_post, n2_pre, n2_post, wo, wgu, wd, *, t_lat, t_out, tm, ctx_row):
    bsz = xs.shape[0]
    tok = lambda n: pl.BlockSpec((1, tm, n), lambda b, j: (b, j, 0))
    once = lambda arr: pl.BlockSpec(arr.shape, lambda b, j: (0,) * arr.ndim, pipeline_mode=pl.Buffered(1))
    return pl.pallas_call(
        functools.partial(_post_kernel, t_lat=t_lat, tm=tm, ctx_row=ctx_row),
        grid=(bsz, t_out // tm),
        in_specs=[tok(GLA_V), tok(GMLP_WIDTH), tok(SWA_Q), tok(D_MODEL), _const_spec(mod.shape),
                  _const_spec(n1_post.shape), _const_spec(n2_pre.shape), _const_spec(n2_post.shape),
                  once(wo), once(wgu), once(wd)],
        out_specs=tok(D_MODEL),
        out_shape=jax.ShapeDtypeStruct((bsz, t_out, D_MODEL), F32),
        compiler_params=_cparams("parallel", "parallel"),
        name="post",
    )(a, bo, c, xs, mod, n1_post, n2_pre, n2_post, wo, wgu, wd)


def _rope_tables(t_lat, t_ctx):
    rows = t_lat // GRID_W
    row = jnp.repeat(jnp.arange(rows), GRID_W).astype(F32)
    col = jnp.tile(jnp.arange(GRID_W), rows).astype(F32)
    inv_freq = jnp.power(ROPE_THETA, -jnp.arange(0, ROPE_AXIS_DIM, 2, dtype=F32) / ROPE_AXIS_DIM)
    ang_row = row[:, None] * inv_freq[None, :]
    ang_col = col[:, None] * inv_freq[None, :]
    ang = jnp.concatenate([ang_row, ang_row, ang_col, ang_col], axis=-1)
    sign = jnp.tile(jnp.concatenate([-jnp.ones((16,), F32), jnp.ones((16,), F32)]), 2)
    cos = jnp.tile(jnp.concatenate([jnp.cos(ang), jnp.ones((t_ctx, HEAD_DIM), F32)], axis=0), (1, 2))
    sin = jnp.tile(jnp.concatenate([jnp.sin(ang) * sign, jnp.zeros((t_ctx, HEAD_DIM), F32)], axis=0), (1, 2))
    return cos, sin, cos.T, sin.T


def _inproj_weights(w, wa2, ba):
    r = GLA_GATE_RANK
    o_code = 2 * GLA_QK + 2 * GLA_V
    o_gm = o_code + 2 * r
    o_sq = o_gm + 2 * GMLP_WIDTH
    o_sk = o_sq + SWA_Q
    o_sv = o_sk + SWA_KV
    code = jnp.concatenate([w[:, o_code:o_gm], jnp.zeros((w.shape[0], COL_CODE - 2 * r), w.dtype)], axis=1)
    w_tok = jnp.concatenate([w[:, 0:o_code], w[:, o_gm:o_sq], w[:, o_sk:o_sv], code], axis=1)
    w_t = jnp.concatenate([w[:, o_sq:o_sk], w[:, o_sv:], w[:, GLA_QK:2 * GLA_QK], code], axis=1).T
    wa = jnp.zeros((COL_CODE, 2 * GLA_QK), F32)
    wa = wa.at[0:r, 0:GLA_QK].set(wa2[0]).at[r:2 * r, GLA_QK:].set(wa2[1])
    return (w_tok.astype(BF16), w_t.astype(BF16), wa.astype(BF16), ba.reshape(1, 2 * GLA_QK),
            wa.T.astype(BF16), ba.reshape(2 * GLA_QK, 1))


def _gla_consts():
    a3, a3t_neg, mask_w, bd_c = _gla_fast_tables()
    amat, mask = _gla_tables()
    lane_head = np.arange(GLA_QK)[None, :] // GLA_DK
    row_head = np.arange(GLA_HEADS * GLA_CHUNK)[:, None] // GLA_CHUNK
    hq = (lane_head == row_head).astype(np.float32)
    vlane_head = np.arange(GLA_V)[None, :] // GLA_DV
    bdv = (vlane_head == row_head).astype(np.float32)
    bds = (vlane_head == (np.arange(GLA_QK)[:, None] // GLA_DK)).astype(np.float32)
    ind = (vlane_head == vlane_head.T).astype(np.float32) / GLA_DV
    return (jnp.asarray(a3, BF16), jnp.asarray(a3t_neg, BF16), jnp.asarray(mask_w, BF16),
            jnp.asarray(bd_c, F32), jnp.asarray(amat, BF16), jnp.asarray(mask, F32), jnp.asarray(hq, BF16),
            jnp.asarray(bdv, F32), jnp.asarray(bds, F32), jnp.asarray(ind, BF16))


def kernel(x, c, ctx, c_ctx, mod_w, mod_b, n1_pre, n1_post, n2_pre, n2_post, w_in, w_out, gla_wa2, gla_ba,
           gla_norm, gmlp_ln_g, gmlp_ln_b, gmlp_ws, gmlp_bs, gmlp_out_g, swa_sink, swa_out_g, ffn_w_gu,
           ffn_w_down):
    bsz, t_lat, d = x.shape
    t_ctx = ctx.shape[1]
    t_all = t_lat + t_ctx
    depth = mod_w.shape[0]
    assert d == D_MODEL and bsz < MOD_ROWS
    assert t_lat % 1024 == 0 and t_ctx % 256 == 0 and t_all % 768 == 0
    ctx_row = bsz
    tm_all = 768
    assert t_ctx == tm_all - _row_split(t_lat, tm_all)

    cc = jnp.zeros((MOD_ROWS, d), F32).at[0:bsz].set(c).at[ctx_row].set(c_ctx)
    mods = _modulation(cc, mod_w, mod_b)
    rope = _rope_tables(t_lat, t_ctx)
    gla_consts = _gla_consts()
    eye = jnp.eye(SWA_WINDOW, dtype=BF16)
    gmask = jnp.asarray((np.arange(GMLP_WIDTH)[None, :] // GMLP_GDIM ==
                         np.arange(GMLP_GROUPS * GMLP_CHUNK)[:, None] // GMLP_CHUNK).astype(np.float32))
    row = lambda v: v.reshape(1, -1)

    xs = jnp.concatenate([x, ctx], axis=1)
    for l in range(depth):
        last = l == depth - 1
        t_out = t_lat if last else t_all
        gla, lg3, gm, sk, sqt, svt, kt, lgt3 = _inproj(
            xs, mods[l], row(n1_pre[l]), _inproj_weights(w_in[l], gla_wa2[l], gla_ba[l]), rope,
            t_lat=t_lat, tm=tm_all, ctx_row=ctx_row)
        a_out = _gla(gla, lg3, kt, lgt3, gla_consts, row(jnp.tile(gla_norm[l], GLA_HEADS)),
                     t_lat=t_lat, t_out=t_out)
        ws_s = gmlp_ws[l].reshape(GMLP_GROUPS * GMLP_CHUNK, GMLP_CHUNK).astype(BF16)
        bs_t = jnp.repeat(gmlp_bs[l].T, GMLP_GDIM, axis=1)
        b_out = _gmlp(gm, row(gmlp_ln_g[l]), row(gmlp_ln_b[l]), ws_s, bs_t, gmask, row(gmlp_out_g[l]),
                      t_out=t_out, tm=256)
        sink_t = row(jnp.repeat(swa_sink[l], SWA_WINDOW))
        out_g_t = jnp.broadcast_to(swa_out_g[l][:, None], (SWA_Q, SWA_WINDOW))
        c_out = _swa(sqt, sk, svt, sink_t, out_g_t, eye, t_lat=t_lat, t_out=t_out)
        xs = _post(a_out, b_out, c_out, xs, mods[l], row(n1_post[l]), row(n2_pre[l]), row(n2_post[l]),
                   w_out[l].astype(BF16), ffn_w_gu[l].astype(BF16), ffn_w_down[l].astype(BF16),
                   t_lat=t_lat, t_out=t_out, tm=1024 if last else tm_all, ctx_row=ctx_row)
    return xs
```

```python
import functools

import numpy as np
import jax
import jax.numpy as jnp
from jax import lax
from jax.experimental import pallas as pl
from jax.experimental.pallas import tpu as pltpu

F32 = jnp.float32
BF16 = jnp.bfloat16

D_MODEL = 1024
GRID_W = 64
HEAD_DIM = 64
GLA_HEADS = 4
GLA_DK = 32
GLA_DV = 64
GLA_QK = GLA_HEADS * GLA_DK
GLA_V = GLA_HEADS * GLA_DV
GLA_GATE_RANK = 16
GLA_GATE_TAU = 16.0
GLA_CHUNK = 64
GMLP_GROUPS = 4
GMLP_GDIM = 64
GMLP_WIDTH = GMLP_GROUPS * GMLP_GDIM
GMLP_CHUNK = 128
SWA_HEADS = 8
SWA_KV_HEADS = 2
SWA_REP = SWA_HEADS // SWA_KV_HEADS
SWA_Q = SWA_HEADS * HEAD_DIM
SWA_KV = SWA_KV_HEADS * HEAD_DIM
SWA_WINDOW = 128
ROPE_AXIS_DIM = HEAD_DIM // 2
ROPE_THETA = 10000.0
MIX_WIDTH = GLA_V + GMLP_WIDTH + SWA_Q
D_FF = -(-8 * D_MODEL // (3 * 256)) * 256
N_MOD = 6
MOD_ROWS = 24

COL_GLA = 2 * GLA_QK + 2 * GLA_V
COL_GM = 2 * GMLP_WIDTH
COL_SK = SWA_KV
COL_CODE = 128
OFF_GM = COL_GLA
OFF_SK = OFF_GM + COL_GM
OFF_CODE = OFF_SK + COL_SK
ROW_SQ = 0
ROW_SV = ROW_SQ + SWA_Q
ROW_GK = ROW_SV + SWA_KV

LOG2E = 1.4426950408889634
SWA_Q_SCALE = HEAD_DIM ** -0.5 * LOG2E
SWA_KEY_TILE = 16
SWA_PART_HEADS = 2
SWA_DEN_ROWS = 16

VMEM_LIMIT = 56 * 1024 * 1024
GLA_LEVELS = (1, 2, 4, 8, 16, 32)


def _cparams(*sem):
    return pltpu.CompilerParams(dimension_semantics=sem, vmem_limit_bytes=VMEM_LIMIT)


def _const_spec(shape):
    nd = len(shape)
    return pl.BlockSpec(shape, lambda *_: (0,) * nd)


def _rms(x, g, eps=1e-6):
    return x * lax.rsqrt(jnp.mean(x * x, axis=-1, keepdims=True) + eps) * g


def _silu(x):
    return x * jax.nn.sigmoid(x)


def _row_split(t_lat, tm):
    return t_lat % tm if t_lat % tm else tm


def _mod_vectors(mod_ref, b, j, idx, *, t_lat, tm, ctx_row):
    lo = idx * D_MODEL
    lat = mod_ref[pl.ds(b, 1), lo:lo + D_MODEL]
    ctx = mod_ref[ctx_row:ctx_row + 1, lo:lo + D_MODEL]
    return lat, jnp.where(j * tm + _row_split(t_lat, tm) >= t_lat, ctx, lat)


def _by_rows(fn, split, *arrays):
    tm = arrays[0].shape[0]
    if split == tm:
        return fn(0, *arrays)
    return jnp.concatenate([fn(0, *[a[0:split] for a in arrays]), fn(1, *[a[split:tm] for a in arrays])], axis=0)


def _mod_kernel(c_ref, w_ref, b_ref, o_ref):
    s = _silu(c_ref[...]).astype(BF16)
    o_ref[0] = jnp.dot(s, w_ref[0].astype(BF16), preferred_element_type=F32) + b_ref[0]


def _modulation(cc, mod_w, mod_b):
    depth, d, n = mod_w.shape
    tn = n // 4
    return pl.pallas_call(
        _mod_kernel,
        grid=(depth, n // tn),
        in_specs=[pl.BlockSpec((MOD_ROWS, d), lambda l, j: (0, 0)),
                  pl.BlockSpec((1, d, tn), lambda l, j: (l, 0, j)),
                  pl.BlockSpec((1, 1, tn), lambda l, j: (l, 0, j))],
        out_specs=pl.BlockSpec((1, MOD_ROWS, tn), lambda l, j: (l, 0, j)),
        out_shape=jax.ShapeDtypeStruct((depth, MOD_ROWS, n), F32),
        compiler_params=_cparams("parallel", "parallel"),
        name="modulation",
    )(cc, mod_w, mod_b.reshape(depth, 1, n))


def _rope(z, cos, sin):
    lane = lax.broadcasted_iota(jnp.int32, z.shape, 1)
    first = (lane % 32) < 16
    rot = jnp.where(first, pltpu.roll(z, 128 - 16, 1), pltpu.roll(z, 16, 1))
    return z * cos + rot * sin


def _split3(x):
    p1 = x.astype(BF16)
    r1 = x - p1.astype(F32)
    p2 = r1.astype(BF16)
    p3 = (r1 - p2.astype(F32)).astype(BF16)
    return p1, p2, p3


def _log_gate(z):
    return (jnp.minimum(z, 0.0) - jnp.log(1.0 + jnp.exp(-jnp.abs(z)))) * (1.0 / GLA_GATE_TAU)


def _rope_t(z, cos_t, sin_t):
    parts = []
    for r0 in range(0, z.shape[0], 32):
        parts += [z[r0 + 16:r0 + 32], z[r0:r0 + 16]]
    rot = jnp.concatenate(parts, axis=0)
    reps = z.shape[0] // cos_t.shape[0]
    return z * jnp.concatenate([cos_t] * reps, axis=0) + rot * jnp.concatenate([sin_t] * reps, axis=0)


def _inproj_kernel(x_ref, mod_ref, g_ref, w_ref, wt_ref, wa_ref, ba_ref, cos_ref, sin_ref, cost_ref, sint_ref,
                   gla_ref, lg_ref, gm_ref, sk_ref, sqt_ref, svt_ref, kt_ref, lgt_ref, *, t_lat, tm, ctx_row):
    b = pl.program_id(0)
    j = pl.program_id(1)
    mod = functools.partial(_mod_vectors, mod_ref, b, j, t_lat=t_lat, tm=tm, ctx_row=ctx_row)
    shift, scale, gain = mod(0), mod(1), g_ref[...]
    hf = _by_rows(lambda part, x: _rms(x, gain) * (1.0 + scale[part]) + shift[part], _row_split(t_lat, tm), x_ref[0])
    h = hf.astype(BF16)
    h_t = hf.T.astype(BF16)

    def proj(off, n):
        return jnp.dot(h, w_ref[:, off:off + n], preferred_element_type=F32)

    def proj_t(off, n):
        return jnp.dot(wt_ref[off:off + n, :], h_t, preferred_element_type=F32)

    n2 = 2 * GLA_QK
    code = proj(OFF_CODE, COL_CODE).astype(BF16)
    lg = _log_gate(jnp.dot(code, wa_ref[...], preferred_element_type=F32) + ba_ref[...])
    for i, p in enumerate(_split3(lg)):
        lg_ref[0, :, i * n2:(i + 1) * n2] = p
    for i, p in enumerate(_split3(lg.T)):
        lgt_ref[0, i * n2:(i + 1) * n2, :] = p

    z = proj(0, COL_GLA)
    gla_ref[0, :, 0:GLA_QK] = (z[:, 0:GLA_QK] * (GLA_DK ** -0.5)).astype(BF16)
    gla_ref[0, :, GLA_QK:COL_GLA] = z[:, GLA_QK:COL_GLA].astype(BF16)
    kt_ref[0] = proj_t(ROW_GK, GLA_QK).astype(BF16)

    gm_ref[0] = proj(OFF_GM, COL_GM).astype(BF16)

    sk_ref[0] = _rope(proj(OFF_SK, COL_SK), cos_ref[...], sin_ref[...]).astype(BF16)
    sqt_ref[0] = (_rope_t(proj_t(ROW_SQ, SWA_Q), cost_ref[...], sint_ref[...]) * SWA_Q_SCALE).astype(BF16)
    svt_ref[0] = proj_t(ROW_SV, SWA_KV).astype(BF16)


def _inproj(xs, mod, n_pre, weights, rope, *, t_lat, tm, ctx_row):
    bsz, t_all, d = xs.shape
    nt = t_all // tm
    tok = lambda n: pl.BlockSpec((1, tm, n), lambda b, j: (b, j, 0))
    tok_t = lambda n: pl.BlockSpec((1, n, tm), lambda b, j: (b, 0, j))
    n2 = 2 * GLA_QK
    cols = (COL_GLA, 3 * n2, COL_GM, COL_SK)
    rows = (SWA_Q, SWA_KV, GLA_QK, 3 * n2)
    out_shape = [jax.ShapeDtypeStruct((bsz, t_all, n), BF16) for n in cols]
    out_shape += [jax.ShapeDtypeStruct((bsz, n, t_all), BF16) for n in rows]
    consts = (mod, n_pre) + tuple(weights)
    cos, sin, cos_t, sin_t = rope
    return pl.pallas_call(
        functools.partial(_inproj_kernel, t_lat=t_lat, tm=tm, ctx_row=ctx_row),
        grid=(bsz, nt),
        in_specs=[tok(d)] + [_const_spec(a.shape) for a in consts] +
                 [pl.BlockSpec((tm, 128), lambda b, j: (j, 0)), pl.BlockSpec((tm, 128), lambda b, j: (j, 0)),
                  pl.BlockSpec((128, tm), lambda b, j: (0, j)), pl.BlockSpec((128, tm), lambda b, j: (0, j))],
        out_specs=[tok(n) for n in cols] + [tok_t(n) for n in rows],
        out_shape=out_shape,
        compiler_params=_cparams("parallel", "parallel"),
        name="inproj",
    )(xs, *consts, cos, sin, cos_t, sin_t)


GLA_FAST_CHUNK = 128
GLA_FAST_UNROLL = 2
GLA_SAFE_LOG_DECAY = -40.0


def _gla_fast_tables():
    c = GLA_FAST_CHUNK
    t = np.arange(c)[:, None]
    r = np.arange(c)[None, :]
    incl = [(r <= t), (r >= t)]
    a3 = np.stack([np.tile(m, (1, 3)) for m in incl]).astype(np.float32)
    a3t_neg = -np.stack([np.tile(m.T, (3, 1)) for m in incl]).astype(np.float32)
    mask_w = np.stack([np.tile(m, (1, GLA_HEADS)) for m in incl]).astype(np.float32)
    half = (np.arange(128)[None, :] // GLA_DV) == ((np.arange(GLA_QK)[:, None] // GLA_DK) % 2)
    return a3, a3t_neg, mask_w, half.astype(np.float32)


def _gla_fast_chunks(jobs, a3, a3t_neg, mask_w, bd_c, states):
    c = GLA_FAST_CHUNK
    dk, hd = GLA_DK, GLA_QK
    cums = [(jnp.dot(a3[d], g3, preferred_element_type=F32),
             jnp.dot(gt3, a3t_neg[d], preferred_element_type=F32))
            for d, _, _, _, g3, gt3 in jobs]

    zr = lambda n: jnp.zeros((n, c), BF16)
    qes, dcols, khs, atts = [], [], [], []
    for (d, q, kt, _, _, _), (cum, ncum_t) in zip(jobs, cums):
        qe = (q.astype(F32) * jnp.exp(cum)).astype(BF16)
        ke_t = kt.astype(F32) * jnp.exp(ncum_t)
        edge = 0 if d else c - 1
        dcol = jnp.exp(-ncum_t[:, edge:edge + 1])
        khs.append((ke_t * dcol).astype(BF16))
        ke_t = ke_t.astype(BF16)
        cols = []
        for hh in range(GLA_HEADS):
            parts = ([zr(hh * dk)] if hh else []) + [ke_t[hh * dk:(hh + 1) * dk]]
            parts += [zr(hd - (hh + 1) * dk)] if hh < GLA_HEADS - 1 else []
            cols.append(jnp.concatenate(parts, axis=0))
        kbd = jnp.concatenate(cols, axis=1)
        atts.append(jnp.dot(qe, kbd, preferred_element_type=F32))
        qes.append(qe)
        dcols.append(dcol)

    kvs = [jnp.concatenate([jnp.dot(kh[0:hd // 2], v[:, 0:128], preferred_element_type=F32),
                            jnp.dot(kh[hd // 2:hd], v[:, 128:256], preferred_element_type=F32)], axis=0)
           for (_, _, _, v, _, _), kh in zip(jobs, khs)]

    lane = lax.broadcasted_iota(jnp.int32, (1, 128), 1)
    m_lo = (lane < GLA_DV).astype(BF16)
    m_hi = (lane >= GLA_DV).astype(BF16)
    z128 = jnp.zeros((c, 128), BF16)
    z64 = jnp.zeros((hd // 2, 128), BF16)
    states = list(states)
    outs = []
    for (d, _, _, v, _, _), qe, dcol, att, kv in zip(jobs, qes, dcols, atts, kvs):
        v_lo, v_hi = v[:, 0:128], v[:, 128:256]
        vbd = jnp.concatenate([jnp.concatenate([v_lo * m_lo, z128], axis=1),
                               jnp.concatenate([v_lo * m_hi, z128], axis=1),
                               jnp.concatenate([z128, v_hi * m_lo], axis=1),
                               jnp.concatenate([z128, v_hi * m_hi], axis=1)], axis=0)
        sb = states[d].astype(BF16)
        s_full = jnp.concatenate([jnp.concatenate([sb[0:hd // 2], z64], axis=1),
                                  jnp.concatenate([z64, sb[hd // 2:hd]], axis=1)], axis=0)
        lhs = jnp.concatenate([att.astype(BF16) * mask_w[d], qe], axis=1)
        outs.append(jnp.dot(lhs, jnp.concatenate([vbd, s_full], axis=0),
                            preferred_element_type=F32))
        states[d] = states[d] * dcol + kv * bd_c
    return outs, states


def _gla_tables():
    c = GLA_CHUNK
    t = np.arange(c)[:, None]
    r = np.arange(c)[None, :]
    sizes = [2 * b for b in GLA_LEVELS]
    same = lambda b: (t // b) == (r // b)
    a_f = [same(b) & (r <= t) for b in sizes] + [same(b) & (r > t) for b in sizes]
    a_b = [same(b) & (r >= t) for b in sizes] + [same(b) & (r < t) for b in sizes]
    m_f = [((t // b) % 2 == 1) & ((r // b) == (t // b) - 1) for b in GLA_LEVELS] + [t == r]
    m_b = [((t // b) % 2 == 0) & ((r // b) == (t // b) + 1) for b in GLA_LEVELS] + [t == r]
    amat = np.stack([np.tile(np.concatenate(a, 0), (1, 3)) for a in (a_f, a_b)]).astype(np.float32)
    mask = np.stack([np.stack([np.tile(m, (GLA_HEADS, 1)) for m in ms]) for ms in (m_f, m_b)])
    return amat, mask.astype(np.float32)


def _gla_chunk(q, k, v, g3, amat, masks, hmask_q, bd_v, bd_s, s_prev, *, backward):
    c = GLA_CHUNK
    nl = len(GLA_LEVELS)
    g = g3[0:c].astype(F32) + g3[c:2 * c].astype(F32) + g3[2 * c:3 * c].astype(F32)
    ps = jnp.dot(amat, g3, preferred_element_type=F32)
    blk = lambda i: ps[i * c:(i + 1) * c]
    q_exp = [g] + [blk(i) for i in range(nl - 1)]
    k_exp = [None] + [blk(nl + i) for i in range(nl - 1)]
    q_full = blk(nl - 1)
    k_full = blk(2 * nl - 1)

    att = None
    for lvl in range(nl + 1):
        if lvl < nl:
            qe = q * jnp.exp(q_exp[lvl])
            ke = k if k_exp[lvl] is None else k * jnp.exp(k_exp[lvl])
        else:
            qe, ke = q, k
        qbd = jnp.concatenate([qe.astype(BF16)] * GLA_HEADS, axis=0) * hmask_q
        p = lax.dot_general(qbd, ke.astype(BF16), (((1,), (1,)), ((), ())), preferred_element_type=F32)
        p = p * masks[lvl]
        att = p if att is None else att + p

    r = jnp.dot(att.astype(BF16), v, preferred_element_type=F32) * bd_v
    o = r[0:c]
    for hh in range(1, GLA_HEADS):
        o = o + r[hh * c:(hh + 1) * c]
    o = o + jnp.dot((q * jnp.exp(q_full)).astype(BF16), s_prev.astype(BF16), preferred_element_type=F32)

    tot = q_full[0:1] if backward else q_full[c - 1:c]
    dcol = jnp.transpose(jnp.broadcast_to(jnp.exp(tot), (GLA_QK, GLA_QK)))
    dcol = jnp.concatenate([dcol, dcol], axis=1)
    kv = lax.dot_general((k * jnp.exp(k_full)).astype(BF16), v, (((0,), (0,)), ((), ())),
                         preferred_element_type=F32)
    return o, s_prev * dcol + kv * bd_s


def _gla_kernel(gla_ref, lg_ref, kt_ref, lgt_ref, a3_ref, a3t_ref, mw_ref, bdc_ref, amat_ref, mask_ref,
                hq_ref, bdv_ref, bds_ref, ind_ref, norm_ref, out_ref, of_ref, ob_ref, *, t_lat, t_all, t_out):
    n2 = 2 * GLA_QK
    o_v = 2 * GLA_QK

    def pieces(rows, d):
        return jnp.concatenate([lg_ref[0, rows, i * n2 + d * GLA_QK:i * n2 + (d + 1) * GLA_QK]
                                for i in range(3)], axis=0)

    cf_ = GLA_FAST_CHUNK
    worst = None
    for ci in range(t_all // cf_):
        tot = jnp.sum(lg_ref[0, ci * cf_:(ci + 1) * cf_, 0:n2].astype(F32), axis=0, keepdims=True)
        worst = tot if worst is None else jnp.minimum(worst, tot)
    safe = jnp.min(worst) >= GLA_SAFE_LOG_DECAY

    @pl.when(safe)
    def _fast():
        c = GLA_FAST_CHUNK
        n_lat = t_lat // c
        n_all = t_all // c
        per_step = GLA_FAST_UNROLL
        assert n_all % per_step == 0

        def job(ci, d):
            rows = pl.ds(pl.multiple_of(ci * c, c), c)
            gt3 = jnp.concatenate([lgt_ref[0, i * n2 + d * GLA_QK:i * n2 + (d + 1) * GLA_QK, rows]
                                   for i in range(3)], axis=1)
            return rows, (d, gla_ref[0, rows, 0:GLA_QK], kt_ref[0, :, rows], gla_ref[0, rows, o_v:o_v + GLA_V],
                          pieces(rows, d), gt3)

        def step(i, carry):
            chunks = [(lax.rem(i * per_step + u + n_lat, n_all), 0) for u in range(per_step)]
            chunks += [(n_all - 1 - (i * per_step + u), 1) for u in range(per_step)]
            rows, jobs = zip(*[job(ci, d) for ci, d in chunks])
            outs, states = _gla_fast_chunks(jobs, a3_ref, a3t_ref, mw_ref, bdc_ref[...], carry)
            for (_, d), r, o in zip(chunks, rows, outs):
                (ob_ref if d else of_ref)[r, :] = o
            return tuple(states)

        zero = jnp.zeros((GLA_QK, 128), F32)
        lax.fori_loop(0, n_all // per_step, step, (zero, zero))

    @pl.when(jnp.logical_not(safe))
    def _robust():
        c = GLA_CHUNK
        n_lat = t_lat // c
        n_all = t_all // c
        hq = hq_ref[...]
        bdv = bdv_ref[...]
        bds = bds_ref[...]

        def one(ci, d, s):
            start = pl.multiple_of(ci * c, c)
            rows = pl.ds(start, c)
            q = gla_ref[0, rows, 0:GLA_QK].astype(F32)
            k = gla_ref[0, rows, GLA_QK:2 * GLA_QK].astype(F32)
            v = gla_ref[0, rows, o_v:o_v + GLA_V]
            o, s = _gla_chunk(q, k, v, pieces(rows, d), amat_ref[d],
                              [mask_ref[d, l] for l in range(len(GLA_LEVELS) + 1)], hq, bdv, bds, s,
                              backward=bool(d))
            (ob_ref if d else of_ref)[rows, :] = o
            return s

        def step(i, carry):
            s_f, s_b = carry
            return one(lax.rem(i + n_lat, n_all), 0, s_f), one(n_all - 1 - i, 1, s_b)

        zero = jnp.zeros((GLA_QK, GLA_V), F32)
        lax.fori_loop(0, n_all, step, (zero, zero))

    tr = 256
    ind = ind_ref[...]
    for r0 in range(0, t_out, tr):
        o = of_ref[r0:r0 + tr, :] + ob_ref[r0:r0 + tr, :]
        sq = o * o
        hi = sq.astype(BF16)
        lo = (sq - hi.astype(F32)).astype(BF16)
        ms = jnp.dot(hi, ind, preferred_element_type=F32) + jnp.dot(lo, ind, preferred_element_type=F32)
        gate = gla_ref[0, r0:r0 + tr, o_v + GLA_V:COL_GLA].astype(F32)
        y = o * lax.rsqrt(ms + 1e-6) * norm_ref[...] * _silu(gate)
        out_ref[0, r0:r0 + tr, :] = y.astype(BF16)


def _gla(gla, lg3, kt, lgt3, consts, norm_t, *, t_lat, t_out):
    bsz, t_all, _ = gla.shape
    n2 = 2 * GLA_QK
    return pl.pallas_call(
        functools.partial(_gla_kernel, t_lat=t_lat, t_all=t_all, t_out=t_out),
        grid=(bsz,),
        in_specs=[pl.BlockSpec((1, t_all, COL_GLA), lambda b: (b, 0, 0)),
                  pl.BlockSpec((1, t_all, 3 * n2), lambda b: (b, 0, 0)),
                  pl.BlockSpec((1, GLA_QK, t_all), lambda b: (b, 0, 0)),
                  pl.BlockSpec((1, 3 * n2, t_all), lambda b: (b, 0, 0))] +
                 [_const_spec(a.shape) for a in consts] + [_const_spec(norm_t.shape)],
        out_specs=pl.BlockSpec((1, t_out, GLA_V), lambda b: (b, 0, 0)),
        out_shape=jax.ShapeDtypeStruct((bsz, t_out, GLA_V), BF16),
        scratch_shapes=[pltpu.VMEM((t_all, GLA_V), F32), pltpu.VMEM((t_all, GLA_V), F32)],
        compiler_params=_cparams("parallel"),
        name="gla",
    )(gla, lg3, kt, lgt3, *consts, norm_t)


def _gmlp_kernel(z_ref, lng_ref, lnb_ref, ws_ref, bs_ref, gm_ref, og_ref, out_ref, *, tm):
    z = z_ref[0].astype(F32)
    zf = 0.5 * z * (1.0 + lax.erf(z * (2.0 ** -0.5)))
    u = zf[:, 0:GMLP_WIDTH]
    v = zf[:, GMLP_WIDTH:]
    mu = jnp.mean(v, axis=-1, keepdims=True)
    vc = v - mu
    v = vc * lax.rsqrt(jnp.mean(vc * vc, axis=-1, keepdims=True) + 1e-5) * lng_ref[...] + lnb_ref[...]
    vb = v.astype(BF16)
    ws = ws_ref[...]
    p = GMLP_CHUNK
    for ci in range(tm // p):
        r = jnp.dot(ws, vb[ci * p:(ci + 1) * p], preferred_element_type=F32) * gm_ref[...]
        mixed = bs_ref[...]
        for g in range(GMLP_GROUPS):
            mixed = mixed + r[g * p:(g + 1) * p]
        y = u[ci * p:(ci + 1) * p] * mixed
        out_ref[0, ci * p:(ci + 1) * p, :] = _rms(y, og_ref[...]).astype(BF16)


def _gmlp(gm, ln_g, ln_b, ws_s, bs_t, gmask, out_g, *, t_out, tm):
    bsz = gm.shape[0]
    return pl.pallas_call(
        functools.partial(_gmlp_kernel, tm=tm),
        grid=(bsz, t_out // tm),
        in_specs=[pl.BlockSpec((1, tm, COL_GM), lambda b, j: (b, j, 0)),
                  _const_spec(ln_g.shape), _const_spec(ln_b.shape), _const_spec(ws_s.shape),
                  _const_spec(bs_t.shape), _const_spec(gmask.shape), _const_spec(out_g.shape)],
        out_specs=pl.BlockSpec((1, tm, GMLP_WIDTH), lambda b, j: (b, j, 0)),
        out_shape=jax.ShapeDtypeStruct((bsz, t_out, GMLP_WIDTH), BF16),
        compiler_params=_cparams("parallel", "parallel"),
        name="gmlp",
    )(gm, ln_g, ln_b, ws_s, bs_t, gmask, out_g)


def _swa_attend(qt_ref, k_ref, vt_ref, sink_ref, og_ref, eye_ref, out_ref, s_ref, p_ref, key_rows, biases):
    w = SWA_WINDOW
    d = HEAD_DIM
    kt = SWA_KEY_TILE
    ph = SWA_PART_HEADS
    pw = ph * w
    parts = range(SWA_HEADS // ph)
    group = lambda part: part * ph // SWA_REP
    keys = jnp.concatenate([k_ref[0, pl.ds(start, size), :] for start, size in key_rows], axis=0)
    nk = keys.shape[0]

    zero = jnp.zeros((d, pw), BF16)
    for part in parts:
        q = jnp.concatenate([qt_ref[0, h * d:(h + 1) * d, :] for h in range(part * ph, (part + 1) * ph)], axis=1)
        qbd = jnp.concatenate([q if g == group(part) else zero for g in range(SWA_KV_HEADS)], axis=0)
        s_ref[part, 0:nk, :] = jnp.dot(keys, qbd, preferred_element_type=F32)

    def tiles(part):
        off = 0
        for (_, size), bias in zip(key_rows, biases):
            for r0 in range(0, size, kt):
                t = s_ref[part, off + r0:off + r0 + kt, :]
                if bias is not None:
                    t = t + jnp.concatenate([bias[r0:r0 + kt]] * ph, axis=1)
                yield off + r0, t
            off += size

    sinks, tops = [], []
    for part in parts:
        sink = sink_ref[:, part * pw:(part + 1) * pw] * LOG2E
        best = None
        for _, t in tiles(part):
            best = t if best is None else jnp.maximum(best, t)
        top = jnp.maximum(jnp.max(best, axis=0, keepdims=True), sink)
        for r0, t in tiles(part):
            p_ref[part, r0:r0 + kt, :] = jnp.exp2(t - top).astype(BF16)
        sinks.append(sink)
        tops.append(top)

    ones = jnp.ones((SWA_DEN_ROWS, nk), BF16)
    vts = [jnp.concatenate([vt_ref[0, g * d:(g + 1) * d, pl.ds(start, size)] for start, size in key_rows] , axis=1)
           for g in range(SWA_KV_HEADS)]
    heads = []
    for part in parts:
        o_ext = jnp.dot(jnp.concatenate([vts[group(part)], ones], axis=0), p_ref[part, 0:nk, :],
                        preferred_element_type=F32)
        scaled = o_ext[0:d] / (o_ext[d:d + 1] + jnp.exp2(sinks[part] - tops[part]))
        heads += [scaled[:, r * w:(r + 1) * w] for r in range(ph)]
    o_t = jnp.concatenate(heads, axis=0)
    y_t = o_t * lax.rsqrt(jnp.mean(o_t * o_t, axis=0, keepdims=True) + 1e-6) * og_ref[...]
    out = lax.dot_general(eye_ref[...], y_t.astype(BF16), (((1,), (1,)), ((), ())), preferred_element_type=F32)
    out_ref[0] = out.astype(BF16)


def _swa_kernel(qt_ref, k_ref, vt_ref, sink_ref, og_ref, eye_ref, out_ref, s_ref, p_ref, *, t_lat, t_all, t_out):
    w = SWA_WINDOW
    n = pl.program_id(1)
    n_lat = t_lat // w
    ctx_rows = (t_lat, t_all - t_lat)
    attend = functools.partial(_swa_attend, qt_ref, k_ref, vt_ref, sink_ref, og_ref, eye_ref, out_ref, s_ref, p_ref)

    @pl.when(n < n_lat)
    def _latent():
        blk = lambda i: (pl.multiple_of(i * w, w), w)
        sk = lax.broadcasted_iota(jnp.int32, (w, w), 0)
        tq = lax.broadcasted_iota(jnp.int32, (w, w), 1)
        neg = jnp.full((w, w), -jnp.inf, F32)
        zero = jnp.zeros((w, w), F32)
        b_prev = jnp.where((sk >= tq) & (n >= 1), zero, neg)
        b_next = jnp.where((sk <= tq) & (n < n_lat - 1), zero, neg)
        attend([ctx_rows, blk(jnp.maximum(n - 1, 0)), blk(n), blk(jnp.minimum(n + 1, n_lat - 1))],
               [None, b_prev, None, b_next])

    if t_out > t_lat:
        @pl.when(n >= n_lat)
        def _context():
            attend([ctx_rows], [None])


def _swa(sqt, sk, svt, sink_t, out_g, eye, *, t_lat, t_out):
    bsz, t_all, _ = sk.shape
    w = SWA_WINDOW
    n_keys = t_all - t_lat + 3 * w
    return pl.pallas_call(
        functools.partial(_swa_kernel, t_lat=t_lat, t_all=t_all, t_out=t_out),
        grid=(bsz, t_out // w),
        in_specs=[pl.BlockSpec((1, SWA_Q, w), lambda b, n: (b, 0, n)),
                  pl.BlockSpec((1, t_all, SWA_KV), lambda b, n: (b, 0, 0)),
                  pl.BlockSpec((1, SWA_KV, t_all), lambda b, n: (b, 0, 0)),
                  _const_spec(sink_t.shape), _const_spec(out_g.shape), _const_spec(eye.shape)],
        out_specs=pl.BlockSpec((1, w, SWA_Q), lambda b, n: (b, n, 0)),
        out_shape=jax.ShapeDtypeStruct((bsz, t_out, SWA_Q), BF16),
        scratch_shapes=[pltpu.VMEM((SWA_HEADS // SWA_PART_HEADS, n_keys, SWA_PART_HEADS * w), F32),
                        pltpu.VMEM((SWA_HEADS // SWA_PART_HEADS, n_keys, SWA_PART_HEADS * w), BF16)],
        compiler_params=_cparams("parallel", "parallel"),
        name="swa",
    )(sqt, sk, svt, sink_t, out_g, eye)


FFN_CHUNKS = ((0, 1536), (1536, 1280))


def _post_kernel(a_ref, b_ref, c_ref, x_ref, mod_ref, n1_ref, n2a_ref, n2b_ref, wo_ref, wgu_ref, wd_ref,
                 out_ref, *, t_lat, tm, ctx_row):
    b = pl.program_id(0)
    j = pl.program_id(1)
    mod = functools.partial(_mod_vectors, mod_ref, b, j, t_lat=t_lat, tm=tm, ctx_row=ctx_row)
    split = _row_split(t_lat, tm)
    gate1, shift2, scale2, gate2 = mod(2), mod(3), mod(4), mod(5)
    n1, n2a, n2b = n1_ref[...], n2a_ref[...], n2b_ref[...]

    cat = jnp.concatenate([a_ref[0], b_ref[0], c_ref[0]], axis=-1)
    y = jnp.dot(cat, wo_ref[...], preferred_element_type=F32)
    x1 = _by_rows(lambda p, x, y: x + gate1[p] * _rms(y, n1), split, x_ref[0], y)
    h = _by_rows(lambda p, x: (_rms(x, n2a) * (1.0 + scale2[p]) + shift2[p]).astype(BF16), split, x1)
    f = None
    for off, n in FFN_CHUNKS:
        gt = jnp.dot(h, wgu_ref[:, off:off + n], preferred_element_type=F32)
        up = jnp.dot(h, wgu_ref[:, D_FF + off:D_FF + off + n], preferred_element_type=F32)
        act = (_silu(gt) * up).astype(BF16)
        part = jnp.dot(act, wd_ref[off:off + n, :], preferred_element_type=F32)
        f = part if f is None else f + part
    out_ref[0] = _by_rows(lambda p, x, f: x + gate2[p] * _rms(f, n2b), split, x1, f)


def _post(a, bo, c, xs, mod, n1_post, n2_pre, n2_post, wo, wgu, wd, *, t_lat, t_out, tm, ctx_row):
    bsz = xs.shape[0]
    tok = lambda n: pl.BlockSpec((1, tm, n), lambda b, j: (b, j, 0))
    once = lambda arr: pl.BlockSpec(arr.shape, lambda b, j: (0,) * arr.ndim, pipeline_mode=pl.Buffered(1))
    return pl.pallas_call(
        functools.partial(_post_kernel, t_lat=t_lat, tm=tm, ctx_row=ctx_row),
        grid=(bsz, t_out // tm),
        in_specs=[tok(GLA_V), tok(GMLP_WIDTH), tok(SWA_Q), tok(D_MODEL), _const_spec(mod.shape),
                  _const_spec(n1_post.shape), _const_spec(n2_pre.shape), _const_spec(n2_post.shape),
                  once(wo), once(wgu), once(wd)],
        out_specs=tok(D_MODEL),
        out_shape=jax.ShapeDtypeStruct((bsz, t_out, D_MODEL), F32),
        compiler_params=_cparams("parallel", "parallel"),
        name="post",
    )(a, bo, c, xs, mod, n1_post, n2_pre, n2_post, wo, wgu, wd)


def _rope_tables(t_lat, t_ctx):
    rows = t_lat // GRID_W
    row = jnp.repeat(jnp.arange(rows), GRID_W).astype(F32)
    col = jnp.tile(jnp.arange(GRID_W), rows).astype(F32)
    inv_freq = jnp.power(ROPE_THETA, -jnp.arange(0, ROPE_AXIS_DIM, 2, dtype=F32) / ROPE_AXIS_DIM)
    ang_row = row[:, None] * inv_freq[None, :]
    ang_col = col[:, None] * inv_freq[None, :]
    ang = jnp.concatenate([ang_row, ang_row, ang_col, ang_col], axis=-1)
    sign = jnp.tile(jnp.concatenate([-jnp.ones((16,), F32), jnp.ones((16,), F32)]), 2)
    cos = jnp.tile(jnp.concatenate([jnp.cos(ang), jnp.ones((t_ctx, HEAD_DIM), F32)], axis=0), (1, 2))
    sin = jnp.tile(jnp.concatenate([jnp.sin(ang) * sign, jnp.zeros((t_ctx, HEAD_DIM), F32)], axis=0), (1, 2))
    return cos, sin, cos.T, sin.T


def _inproj_weights(w, wa2, ba):
    r = GLA_GATE_RANK
    o_code = 2 * GLA_QK + 2 * GLA_V
    o_gm = o_code + 2 * r
    o_sq = o_gm + 2 * GMLP_WIDTH
    o_sk = o_sq + SWA_Q
    o_sv = o_sk + SWA_KV
    code = jnp.concatenate([w[:, o_code:o_gm], jnp.zeros((w.shape[0], COL_CODE - 2 * r), w.dtype)], axis=1)
    w_tok = jnp.concatenate([w[:, 0:o_code], w[:, o_gm:o_sq], w[:, o_sk:o_sv], code], axis=1)
    w_t = jnp.concatenate([w[:, o_sq:o_sk], w[:, o_sv:], w[:, GLA_QK:2 * GLA_QK]], axis=1).T
    wa = jnp.zeros((COL_CODE, 2 * GLA_QK), F32)
    wa = wa.at[0:r, 0:GLA_QK].set(wa2[0]).at[r:2 * r, GLA_QK:].set(wa2[1])
    return w_tok.astype(BF16), w_t.astype(BF16), wa.astype(BF16), ba.reshape(1, 2 * GLA_QK)


def _gla_consts():
    a3, a3t_neg, mask_w, bd_c = _gla_fast_tables()
    amat, mask = _gla_tables()
    lane_head = np.arange(GLA_QK)[None, :] // GLA_DK
    row_head = np.arange(GLA_HEADS * GLA_CHUNK)[:, None] // GLA_CHUNK
    hq = (lane_head == row_head).astype(np.float32)
    vlane_head = np.arange(GLA_V)[None, :] // GLA_DV
    bdv = (vlane_head == row_head).astype(np.float32)
    bds = (vlane_head == (np.arange(GLA_QK)[:, None] // GLA_DK)).astype(np.float32)
    ind = (vlane_head == vlane_head.T).astype(np.float32) / GLA_DV
    return (jnp.asarray(a3, BF16), jnp.asarray(a3t_neg, BF16), jnp.asarray(mask_w, BF16),
            jnp.asarray(bd_c, F32), jnp.asarray(amat, BF16), jnp.asarray(mask, F32), jnp.asarray(hq, BF16),
            jnp.asarray(bdv, F32), jnp.asarray(bds, F32), jnp.asarray(ind, BF16))


def kernel(x, c, ctx, c_ctx, mod_w, mod_b, n1_pre, n1_post, n2_pre, n2_post, w_in, w_out, gla_wa2, gla_ba,
           gla_norm, gmlp_ln_g, gmlp_ln_b, gmlp_ws, gmlp_bs, gmlp_out_g, swa_sink, swa_out_g, ffn_w_gu,
           ffn_w_down):
    bsz, t_lat, d = x.shape
    t_ctx = ctx.shape[1]
    t_all = t_lat + t_ctx
    depth = mod_w.shape[0]
    assert d == D_MODEL and bsz < MOD_ROWS
    assert t_lat % 1024 == 0 and t_ctx % 256 == 0 and t_all % 768 == 0
    ctx_row = bsz
    tm_all = 768
    assert t_ctx == tm_all - _row_split(t_lat, tm_all)

    cc = jnp.zeros((MOD_ROWS, d), F32).at[0:bsz].set(c).at[ctx_row].set(c_ctx)
    mods = _modulation(cc, mod_w, mod_b)
    rope = _rope_tables(t_lat, t_ctx)
    gla_consts = _gla_consts()
    eye = jnp.eye(SWA_WINDOW, dtype=BF16)
    gmask = jnp.asarray((np.arange(GMLP_WIDTH)[None, :] // GMLP_GDIM ==
                         np.arange(GMLP_GROUPS * GMLP_CHUNK)[:, None] // GMLP_CHUNK).astype(np.float32))
    row = lambda v: v.reshape(1, -1)

    xs = jnp.concatenate([x, ctx], axis=1)
    for l in range(depth):
        last = l == depth - 1
        t_out = t_lat if last else t_all
        gla, lg3, gm, sk, sqt, svt, kt, lgt3 = _inproj(
            xs, mods[l], row(n1_pre[l]), _inproj_weights(w_in[l], gla_wa2[l], gla_ba[l]), rope,
            t_lat=t_lat, tm=tm_all, ctx_row=ctx_row)
        a_out = _gla(gla, lg3, kt, lgt3, gla_consts, row(jnp.tile(gla_norm[l], GLA_HEADS)),
                     t_lat=t_lat, t_out=t_out)
        ws_s = gmlp_ws[l].reshape(GMLP_GROUPS * GMLP_CHUNK, GMLP_CHUNK).astype(BF16)
        bs_t = jnp.repeat(gmlp_bs[l].T, GMLP_GDIM, axis=1)
        b_out = _gmlp(gm, row(gmlp_ln_g[l]), row(gmlp_ln_b[l]), ws_s, bs_t, gmask, row(gmlp_out_g[l]),
                      t_out=t_out, tm=256)
        sink_t = row(jnp.repeat(swa_sink[l], SWA_WINDOW))
        out_g_t = jnp.broadcast_to(swa_out_g[l][:, None], (SWA_Q, SWA_WINDOW))
        c_out = _swa(sqt, sk, svt, sink_t, out_g_t, eye, t_lat=t_lat, t_out=t_out)
        xs = _post(a_out, b_out, c_out, xs, mods[l], row(n1_post[l]), row(n2_pre[l]), row(n2_post[l]),
                   w_out[l].astype(BF16), ffn_w_gu[l].astype(BF16), ffn_w_down[l].astype(BF16),
                   t_lat=t_lat, t_out=t_out, tm=1024 if last else tm_all, ctx_row=ctx_row)
    return xs
```

```python
import functools

import numpy as np
import jax
import jax.numpy as jnp
from jax import lax
from jax.experimental import pallas as pl
from jax.experimental.pallas import tpu as pltpu

F32 = jnp.float32
BF16 = jnp.bfloat16

D_MODEL = 1024
GRID_W = 64
HEAD_DIM = 64
GLA_HEADS = 4
GLA_DK = 32
GLA_DV = 64
GLA_QK = GLA_HEADS * GLA_DK
GLA_V = GLA_HEADS * GLA_DV
GLA_GATE_RANK = 16
GLA_GATE_TAU = 16.0
GLA_CHUNK = 64
GMLP_GROUPS = 4
GMLP_GDIM = 64
GMLP_WIDTH = GMLP_GROUPS * GMLP_GDIM
GMLP_CHUNK = 128
SWA_HEADS = 8
SWA_KV_HEADS = 2
SWA_REP = SWA_HEADS // SWA_KV_HEADS
SWA_Q = SWA_HEADS * HEAD_DIM
SWA_KV = SWA_KV_HEADS * HEAD_DIM
SWA_WINDOW = 128
ROPE_AXIS_DIM = HEAD_DIM // 2
ROPE_THETA = 10000.0
MIX_WIDTH = GLA_V + GMLP_WIDTH + SWA_Q
D_FF = -(-8 * D_MODEL // (3 * 256)) * 256
N_MOD = 6
MOD_ROWS = 24

COL_GLA = 2 * GLA_QK + 2 * GLA_V
COL_GM = 2 * GMLP_WIDTH
COL_SK = SWA_KV
COL_CODE = 128
OFF_GM = COL_GLA
OFF_SK = OFF_GM + COL_GM
OFF_CODE = OFF_SK + COL_SK
ROW_SQ = 0
ROW_SV = ROW_SQ + SWA_Q
ROW_GK = ROW_SV + SWA_KV

LOG2E = 1.4426950408889634
SWA_Q_SCALE = HEAD_DIM ** -0.5 * LOG2E
SWA_KEY_TILE = 16
SWA_PART_HEADS = 2
SWA_STEP_BLOCKS = 2
SWA_DEN_ROWS = 16

VMEM_LIMIT = 56 * 1024 * 1024
GLA_LEVELS = (1, 2, 4, 8, 16, 32)


def _cparams(*sem):
    return pltpu.CompilerParams(dimension_semantics=sem, vmem_limit_bytes=VMEM_LIMIT)


def _const_spec(shape):
    nd = len(shape)
    return pl.BlockSpec(shape, lambda *_: (0,) * nd)


def _rms(x, g, eps=1e-6):
    return x * lax.rsqrt(jnp.mean(x * x, axis=-1, keepdims=True) + eps) * g


def _silu(x):
    return x * jax.nn.sigmoid(x)


def _row_split(t_lat, tm):
    return t_lat % tm if t_lat % tm else tm


def _mod_vectors(mod_ref, b, j, idx, *, t_lat, tm, ctx_row):
    lo = idx * D_MODEL
    lat = mod_ref[pl.ds(b, 1), lo:lo + D_MODEL]
    ctx = mod_ref[ctx_row:ctx_row + 1, lo:lo + D_MODEL]
    return lat, jnp.where(j * tm + _row_split(t_lat, tm) >= t_lat, ctx, lat)


STREAM_PIECE = 256


def _stream_specs(x, ctx, tm):
    rows, d = x.shape[1:]
    per_tile = tm // STREAM_PIECE
    last = rows // STREAM_PIECE - 1
    piece = lambda k: pl.BlockSpec((1, STREAM_PIECE, d), lambda b, j: (b, jnp.minimum(j * per_tile + k, last), 0))
    operands, specs = [x] * per_tile, [piece(k) for k in range(per_tile)]
    if ctx is not None:
        assert ctx.shape[1] == STREAM_PIECE
        operands.append(ctx)
        specs.append(pl.BlockSpec((1, STREAM_PIECE, d), lambda b, j: (b, 0, 0)))
    return operands, specs


def _stream_tile(refs, j, *, tm, t_lat, ctx_separate):
    pieces = [r[0] for r in refs[:tm // STREAM_PIECE]]
    if ctx_separate:
        pieces[-1] = jnp.where((j + 1) * tm - STREAM_PIECE >= t_lat, refs[-1][0], pieces[-1])
    return jnp.concatenate(pieces, axis=0)


def _by_rows(fn, split, *arrays):
    tm = arrays[0].shape[0]
    if split == tm:
        return fn(0, *arrays)
    return jnp.concatenate([fn(0, *[a[0:split] for a in arrays]), fn(1, *[a[split:tm] for a in arrays])], axis=0)


def _mod_kernel(c_ref, w_ref, b_ref, o_ref):
    s = _silu(c_ref[...]).astype(BF16)
    o_ref[0] = jnp.dot(s, w_ref[0].astype(BF16), preferred_element_type=F32) + b_ref[0]


def _modulation(cc, mod_w, mod_b):
    depth, d, n = mod_w.shape
    tn = n // 4
    return pl.pallas_call(
        _mod_kernel,
        grid=(depth, n // tn),
        in_specs=[pl.BlockSpec((MOD_ROWS, d), lambda l, j: (0, 0)),
                  pl.BlockSpec((1, d, tn), lambda l, j: (l, 0, j)),
                  pl.BlockSpec((1, 1, tn), lambda l, j: (l, 0, j))],
        out_specs=pl.BlockSpec((1, MOD_ROWS, tn), lambda l, j: (l, 0, j)),
        out_shape=jax.ShapeDtypeStruct((depth, MOD_ROWS, n), F32),
        compiler_params=_cparams("parallel", "parallel"),
        name="modulation",
    )(cc, mod_w, mod_b.reshape(depth, 1, n))


def _rope(z, cos, sin):
    lane = lax.broadcasted_iota(jnp.int32, z.shape, 1)
    first = (lane % 32) < 16
    rot = jnp.where(first, pltpu.roll(z, 128 - 16, 1), pltpu.roll(z, 16, 1))
    return z * cos + rot * sin


def _split3(x):
    p1 = x.astype(BF16)
    r1 = x - p1.astype(F32)
    p2 = r1.astype(BF16)
    p3 = (r1 - p2.astype(F32)).astype(BF16)
    return p1, p2, p3


def _log_gate(z):
    return (jnp.minimum(z, 0.0) - jnp.log(1.0 + jnp.exp(-jnp.abs(z)))) * (1.0 / GLA_GATE_TAU)


def _rope_t(z, cos_t, sin_t):
    parts = []
    for r0 in range(0, z.shape[0], 32):
        parts += [z[r0 + 16:r0 + 32], z[r0:r0 + 16]]
    rot = jnp.concatenate(parts, axis=0)
    reps = z.shape[0] // cos_t.shape[0]
    return z * jnp.concatenate([cos_t] * reps, axis=0) + rot * jnp.concatenate([sin_t] * reps, axis=0)


def _inproj_kernel(*refs, t_lat, tm, ctx_row, n_stream, ctx_separate):
    (mod_ref, g_ref, w_ref, wt_ref, wa_ref, ba_ref, cos_ref, sin_ref, cost_ref, sint_ref,
     gla_ref, lg_ref, gm_ref, sk_ref, sqt_ref, svt_ref, kt_ref, lgt_ref) = refs[n_stream:]
    b = pl.program_id(0)
    j = pl.program_id(1)
    x = _stream_tile(refs[:n_stream], j, tm=tm, t_lat=t_lat, ctx_separate=ctx_separate)
    mod = functools.partial(_mod_vectors, mod_ref, b, j, t_lat=t_lat, tm=tm, ctx_row=ctx_row)
    shift, scale, gain = mod(0), mod(1), g_ref[...]
    hf = _by_rows(lambda part, x: _rms(x, gain) * (1.0 + scale[part]) + shift[part], _row_split(t_lat, tm), x)
    h = hf.astype(BF16)
    h_t = hf.T.astype(BF16)

    def proj(off, n):
        return jnp.dot(h, w_ref[:, off:off + n], preferred_element_type=F32)

    def proj_t(off, n):
        return jnp.dot(wt_ref[off:off + n, :], h_t, preferred_element_type=F32)

    n2 = 2 * GLA_QK
    code = proj(OFF_CODE, COL_CODE).astype(BF16)
    lg = _log_gate(jnp.dot(code, wa_ref[...], preferred_element_type=F32) + ba_ref[...])
    for i, p in enumerate(_split3(lg)):
        lg_ref[0, :, i * n2:(i + 1) * n2] = p
    for i, p in enumerate(_split3(lg.T)):
        lgt_ref[0, i * n2:(i + 1) * n2, :] = p

    z = proj(0, COL_GLA)
    gla_ref[0, :, 0:GLA_QK] = (z[:, 0:GLA_QK] * (GLA_DK ** -0.5)).astype(BF16)
    gla_ref[0, :, GLA_QK:COL_GLA] = z[:, GLA_QK:COL_GLA].astype(BF16)
    kt_ref[0] = proj_t(ROW_GK, GLA_QK).astype(BF16)

    gm_ref[0] = proj(OFF_GM, COL_GM).astype(BF16)

    sk_ref[0] = _rope(proj(OFF_SK, COL_SK), cos_ref[...], sin_ref[...]).astype(BF16)
    sqt_ref[0] = (_rope_t(proj_t(ROW_SQ, SWA_Q), cost_ref[...], sint_ref[...]) * SWA_Q_SCALE).astype(BF16)
    svt_ref[0] = proj_t(ROW_SV, SWA_KV).astype(BF16)


def _inproj(x, ctx, mod, n_pre, weights, rope, *, t_lat, t_all, tm, ctx_row):
    bsz = x.shape[0]
    nt = t_all // tm
    stream, stream_specs = _stream_specs(x, ctx, tm)
    tok = lambda n: pl.BlockSpec((1, tm, n), lambda b, j: (b, j, 0))
    tok_t = lambda n: pl.BlockSpec((1, n, tm), lambda b, j: (b, 0, j))
    n2 = 2 * GLA_QK
    cols = (COL_GLA, 3 * n2, COL_GM, COL_SK)
    rows = (SWA_Q, SWA_KV, GLA_QK, 3 * n2)
    out_shape = [jax.ShapeDtypeStruct((bsz, t_all, n), BF16) for n in cols]
    out_shape += [jax.ShapeDtypeStruct((bsz, n, t_all), BF16) for n in rows]
    consts = (mod, n_pre) + tuple(weights)
    cos, sin, cos_t, sin_t = rope
    return pl.pallas_call(
        functools.partial(_inproj_kernel, t_lat=t_lat, tm=tm, ctx_row=ctx_row, n_stream=len(stream),
                          ctx_separate=ctx is not None),
        grid=(bsz, nt),
        in_specs=stream_specs + [_const_spec(a.shape) for a in consts] +
                 [pl.BlockSpec((tm, 128), lambda b, j: (j, 0)), pl.BlockSpec((tm, 128), lambda b, j: (j, 0)),
                  pl.BlockSpec((128, tm), lambda b, j: (0, j)), pl.BlockSpec((128, tm), lambda b, j: (0, j))],
        out_specs=[tok(n) for n in cols] + [tok_t(n) for n in rows],
        out_shape=out_shape,
        compiler_params=_cparams("parallel", "parallel"),
        name="inproj",
    )(*stream, *consts, cos, sin, cos_t, sin_t)


GLA_FAST_CHUNK = 128
GLA_FAST_UNROLL = 2
GLA_SAFE_LOG_DECAY = -40.0


def _gla_fast_tables():
    c = GLA_FAST_CHUNK
    t = np.arange(c)[:, None]
    r = np.arange(c)[None, :]
    incl = [(r <= t), (r >= t)]
    a3 = np.stack([np.tile(m, (1, 3)) for m in incl]).astype(np.float32)
    a3t_neg = -np.stack([np.tile(m.T, (3, 1)) for m in incl]).astype(np.float32)
    mask_w = np.stack([np.tile(m, (1, GLA_HEADS)) for m in incl]).astype(np.float32)
    half = (np.arange(128)[None, :] // GLA_DV) == ((np.arange(GLA_QK)[:, None] // GLA_DK) % 2)
    return a3, a3t_neg, mask_w, half.astype(np.float32)


def _gla_fast_chunks(jobs, a3, a3t_neg, mask_w, bd_c, states):
    c = GLA_FAST_CHUNK
    dk, hd = GLA_DK, GLA_QK
    cums = [(jnp.dot(a3[d], g3, preferred_element_type=F32),
             jnp.dot(gt3, a3t_neg[d], preferred_element_type=F32))
            for d, _, _, _, g3, gt3 in jobs]

    zr = lambda n: jnp.zeros((n, c), BF16)
    qes, dcols, khs, atts = [], [], [], []
    for (d, q, kt, _, _, _), (cum, ncum_t) in zip(jobs, cums):
        qe = (q.astype(F32) * jnp.exp(cum)).astype(BF16)
        ke_t = kt.astype(F32) * jnp.exp(ncum_t)
        edge = 0 if d else c - 1
        dcol = jnp.exp(-ncum_t[:, edge:edge + 1])
        khs.append((ke_t * dcol).astype(BF16))
        ke_t = ke_t.astype(BF16)
        cols = []
        for hh in range(GLA_HEADS):
            parts = ([zr(hh * dk)] if hh else []) + [ke_t[hh * dk:(hh + 1) * dk]]
            parts += [zr(hd - (hh + 1) * dk)] if hh < GLA_HEADS - 1 else []
            cols.append(jnp.concatenate(parts, axis=0))
        kbd = jnp.concatenate(cols, axis=1)
        atts.append(jnp.dot(qe, kbd, preferred_element_type=F32))
        qes.append(qe)
        dcols.append(dcol)

    kvs = [jnp.concatenate([jnp.dot(kh[0:hd // 2], v[:, 0:128], preferred_element_type=F32),
                            jnp.dot(kh[hd // 2:hd], v[:, 128:256], preferred_element_type=F32)], axis=0)
           for (_, _, _, v, _, _), kh in zip(jobs, khs)]

    lane = lax.broadcasted_iota(jnp.int32, (1, 128), 1)
    m_lo = (lane < GLA_DV).astype(BF16)
    m_hi = (lane >= GLA_DV).astype(BF16)
    z128 = jnp.zeros((c, 128), BF16)
    z64 = jnp.zeros((hd // 2, 128), BF16)
    states = list(states)
    outs = []
    for (d, _, _, v, _, _), qe, dcol, att, kv in zip(jobs, qes, dcols, atts, kvs):
        v_lo, v_hi = v[:, 0:128], v[:, 128:256]
        vbd = jnp.concatenate([jnp.concatenate([v_lo * m_lo, z128], axis=1),
                               jnp.concatenate([v_lo * m_hi, z128], axis=1),
                               jnp.concatenate([z128, v_hi * m_lo], axis=1),
                               jnp.concatenate([z128, v_hi * m_hi], axis=1)], axis=0)
        sb = states[d].astype(BF16)
        s_full = jnp.concatenate([jnp.concatenate([sb[0:hd // 2], z64], axis=1),
                                  jnp.concatenate([z64, sb[hd // 2:hd]], axis=1)], axis=0)
        lhs = jnp.concatenate([att.astype(BF16) * mask_w[d], qe], axis=1)
        outs.append(jnp.dot(lhs, jnp.concatenate([vbd, s_full], axis=0),
                            preferred_element_type=F32))
        states[d] = states[d] * dcol + kv * bd_c
    return outs, states


def _gla_tables():
    c = GLA_CHUNK
    t = np.arange(c)[:, None]
    r = np.arange(c)[None, :]
    sizes = [2 * b for b in GLA_LEVELS]
    same = lambda b: (t // b) == (r // b)
    a_f = [same(b) & (r <= t) for b in sizes] + [same(b) & (r > t) for b in sizes]
    a_b = [same(b) & (r >= t) for b in sizes] + [same(b) & (r < t) for b in sizes]
    m_f = [((t // b) % 2 == 1) & ((r // b) == (t // b) - 1) for b in GLA_LEVELS] + [t == r]
    m_b = [((t // b) % 2 == 0) & ((r // b) == (t // b) + 1) for b in GLA_LEVELS] + [t == r]
    amat = np.stack([np.tile(np.concatenate(a, 0), (1, 3)) for a in (a_f, a_b)]).astype(np.float32)
    mask = np.stack([np.stack([np.tile(m, (GLA_HEADS, 1)) for m in ms]) for ms in (m_f, m_b)])
    return amat, mask.astype(np.float32)


def _gla_chunk(q, k, v, g3, amat, masks, hmask_q, bd_v, bd_s, s_prev, *, backward):
    c = GLA_CHUNK
    nl = len(GLA_LEVELS)
    g = g3[0:c].astype(F32) + g3[c:2 * c].astype(F32) + g3[2 * c:3 * c].astype(F32)
    ps = jnp.dot(amat, g3, preferred_element_type=F32)
    blk = lambda i: ps[i * c:(i + 1) * c]
    q_exp = [g] + [blk(i) for i in range(nl - 1)]
    k_exp = [None] + [blk(nl + i) for i in range(nl - 1)]
    q_full = blk(nl - 1)
    k_full = blk(2 * nl - 1)

    att = None
    for lvl in range(nl + 1):
        if lvl < nl:
            qe = q * jnp.exp(q_exp[lvl])
            ke = k if k_exp[lvl] is None else k * jnp.exp(k_exp[lvl])
        else:
            qe, ke = q, k
        qbd = jnp.concatenate([qe.astype(BF16)] * GLA_HEADS, axis=0) * hmask_q
        p = lax.dot_general(qbd, ke.astype(BF16), (((1,), (1,)), ((), ())), preferred_element_type=F32)
        p = p * masks[lvl]
        att = p if att is None else att + p

    r = jnp.dot(att.astype(BF16), v, preferred_element_type=F32) * bd_v
    o = r[0:c]
    for hh in range(1, GLA_HEADS):
        o = o + r[hh * c:(hh + 1) * c]
    o = o + jnp.dot((q * jnp.exp(q_full)).astype(BF16), s_prev.astype(BF16), preferred_element_type=F32)

    tot = q_full[0:1] if backward else q_full[c - 1:c]
    dcol = jnp.transpose(jnp.broadcast_to(jnp.exp(tot), (GLA_QK, GLA_QK)))
    dcol = jnp.concatenate([dcol, dcol], axis=1)
    kv = lax.dot_general((k * jnp.exp(k_full)).astype(BF16), v, (((0,), (0,)), ((), ())),
                         preferred_element_type=F32)
    return o, s_prev * dcol + kv * bd_s


def _gla_kernel(gla_ref, lg_ref, kt_ref, lgt_ref, a3_ref, a3t_ref, mw_ref, bdc_ref, amat_ref, mask_ref,
                hq_ref, bdv_ref, bds_ref, ind_ref, norm_ref, out_ref, of_ref, ob_ref, *, t_lat, t_all, t_out):
    n2 = 2 * GLA_QK
    o_v = 2 * GLA_QK

    def pieces(rows, d):
        return jnp.concatenate([lg_ref[0, rows, i * n2 + d * GLA_QK:i * n2 + (d + 1) * GLA_QK]
                                for i in range(3)], axis=0)

    cf_ = GLA_FAST_CHUNK
    worst = None
    for ci in range(t_all // cf_):
        tot = jnp.sum(lg_ref[0, ci * cf_:(ci + 1) * cf_, 0:n2].astype(F32), axis=0, keepdims=True)
        worst = tot if worst is None else jnp.minimum(worst, tot)
    safe = jnp.min(worst) >= GLA_SAFE_LOG_DECAY

    @pl.when(safe)
    def _fast():
        c = GLA_FAST_CHUNK
        n_lat = t_lat // c
        n_all = t_all // c
        per_step = GLA_FAST_UNROLL
        assert n_all % per_step == 0

        def job(ci, d):
            rows = pl.ds(pl.multiple_of(ci * c, c), c)
            gt3 = jnp.concatenate([lgt_ref[0, i * n2 + d * GLA_QK:i * n2 + (d + 1) * GLA_QK, rows]
                                   for i in range(3)], axis=1)
            return rows, (d, gla_ref[0, rows, 0:GLA_QK], kt_ref[0, :, rows], gla_ref[0, rows, o_v:o_v + GLA_V],
                          pieces(rows, d), gt3)

        def step(i, carry):
            chunks = [(lax.rem(i * per_step + u + n_lat, n_all), 0) for u in range(per_step)]
            chunks += [(n_all - 1 - (i * per_step + u), 1) for u in range(per_step)]
            rows, jobs = zip(*[job(ci, d) for ci, d in chunks])
            outs, states = _gla_fast_chunks(jobs, a3_ref, a3t_ref, mw_ref, bdc_ref[...], carry)
            for (_, d), r, o in zip(chunks, rows, outs):
                (ob_ref if d else of_ref)[r, :] = o
            return tuple(states)

        zero = jnp.zeros((GLA_QK, 128), F32)
        lax.fori_loop(0, n_all // per_step, step, (zero, zero))

    @pl.when(jnp.logical_not(safe))
    def _robust():
        c = GLA_CHUNK
        n_lat = t_lat // c
        n_all = t_all // c
        hq = hq_ref[...]
        bdv = bdv_ref[...]
        bds = bds_ref[...]

        def one(ci, d, s):
            start = pl.multiple_of(ci * c, c)
            rows = pl.ds(start, c)
            q = gla_ref[0, rows, 0:GLA_QK].astype(F32)
            k = gla_ref[0, rows, GLA_QK:2 * GLA_QK].astype(F32)
            v = gla_ref[0, rows, o_v:o_v + GLA_V]
            o, s = _gla_chunk(q, k, v, pieces(rows, d), amat_ref[d],
                              [mask_ref[d, l] for l in range(len(GLA_LEVELS) + 1)], hq, bdv, bds, s,
                              backward=bool(d))
            (ob_ref if d else of_ref)[rows, :] = o
            return s

        def step(i, carry):
            s_f, s_b = carry
            return one(lax.rem(i + n_lat, n_all), 0, s_f), one(n_all - 1 - i, 1, s_b)

        zero = jnp.zeros((GLA_QK, GLA_V), F32)
        lax.fori_loop(0, n_all, step, (zero, zero))

    tr = 256
    ind = ind_ref[...]
    for r0 in range(0, t_out, tr):
        o = of_ref[r0:r0 + tr, :] + ob_ref[r0:r0 + tr, :]
        sq = o * o
        hi = sq.astype(BF16)
        lo = (sq - hi.astype(F32)).astype(BF16)
        ms = jnp.dot(hi, ind, preferred_element_type=F32) + jnp.dot(lo, ind, preferred_element_type=F32)
        gate = gla_ref[0, r0:r0 + tr, o_v + GLA_V:COL_GLA].astype(F32)
        y = o * lax.rsqrt(ms + 1e-6) * norm_ref[...] * _silu(gate)
        out_ref[0, r0:r0 + tr, :] = y.astype(BF16)


def _gla(gla, lg3, kt, lgt3, consts, norm_t, *, t_lat, t_out):
    bsz, t_all, _ = gla.shape
    n2 = 2 * GLA_QK
    return pl.pallas_call(
        functools.partial(_gla_kernel, t_lat=t_lat, t_all=t_all, t_out=t_out),
        grid=(bsz,),
        in_specs=[pl.BlockSpec((1, t_all, COL_GLA), lambda b: (b, 0, 0)),
                  pl.BlockSpec((1, t_all, 3 * n2), lambda b: (b, 0, 0)),
                  pl.BlockSpec((1, GLA_QK, t_all), lambda b: (b, 0, 0)),
                  pl.BlockSpec((1, 3 * n2, t_all), lambda b: (b, 0, 0))] +
                 [_const_spec(a.shape) for a in consts] + [_const_spec(norm_t.shape)],
        out_specs=pl.BlockSpec((1, t_out, GLA_V), lambda b: (b, 0, 0)),
        out_shape=jax.ShapeDtypeStruct((bsz, t_out, GLA_V), BF16),
        scratch_shapes=[pltpu.VMEM((t_all, GLA_V), F32), pltpu.VMEM((t_all, GLA_V), F32)],
        compiler_params=_cparams("parallel"),
        name="gla",
    )(gla, lg3, kt, lgt3, *consts, norm_t)


def _gmlp_kernel(z_ref, lng_ref, lnb_ref, ws_ref, bs_ref, gm_ref, og_ref, out_ref, *, tm):
    z = z_ref[0].astype(F32)
    zf = 0.5 * z * (1.0 + lax.erf(z * (2.0 ** -0.5)))
    u = zf[:, 0:GMLP_WIDTH]
    v = zf[:, GMLP_WIDTH:]
    mu = jnp.mean(v, axis=-1, keepdims=True)
    vc = v - mu
    v = vc * lax.rsqrt(jnp.mean(vc * vc, axis=-1, keepdims=True) + 1e-5) * lng_ref[...] + lnb_ref[...]
    vb = v.astype(BF16)
    ws = ws_ref[...]
    p = GMLP_CHUNK
    for ci in range(tm // p):
        r = jnp.dot(ws, vb[ci * p:(ci + 1) * p], preferred_element_type=F32) * gm_ref[...]
        mixed = bs_ref[...]
        for g in range(GMLP_GROUPS):
            mixed = mixed + r[g * p:(g + 1) * p]
        y = u[ci * p:(ci + 1) * p] * mixed
        out_ref[0, ci * p:(ci + 1) * p, :] = _rms(y, og_ref[...]).astype(BF16)


def _gmlp(gm, ln_g, ln_b, ws_s, bs_t, gmask, out_g, *, t_out, tm):
    bsz = gm.shape[0]
    return pl.pallas_call(
        functools.partial(_gmlp_kernel, tm=tm),
        grid=(bsz, t_out // tm),
        in_specs=[pl.BlockSpec((1, tm, COL_GM), lambda b, j: (b, j, 0)),
                  _const_spec(ln_g.shape), _const_spec(ln_b.shape), _const_spec(ws_s.shape),
                  _const_spec(bs_t.shape), _const_spec(gmask.shape), _const_spec(out_g.shape)],
        out_specs=pl.BlockSpec((1, tm, GMLP_WIDTH), lambda b, j: (b, j, 0)),
        out_shape=jax.ShapeDtypeStruct((bsz, t_out, GMLP_WIDTH), BF16),
        compiler_params=_cparams("parallel", "parallel"),
        name="gmlp",
    )(gm, ln_g, ln_b, ws_s, bs_t, gmask, out_g)


def _swa_attend(qt_ref, k_ref, vt_ref, sink_ref, og_ref, eye_ref, out_ref, s_ref, p_ref, u, key_rows, biases):
    w = SWA_WINDOW
    d = HEAD_DIM
    kt = SWA_KEY_TILE
    ph = SWA_PART_HEADS
    pw = ph * w
    parts = range(SWA_HEADS // ph)
    group = lambda part: part * ph // SWA_REP
    s_ref = s_ref.at[u]
    p_ref = p_ref.at[u]
    cols = slice(u * w, (u + 1) * w)
    keys = jnp.concatenate([k_ref[0, pl.ds(start, size), :] for start, size in key_rows], axis=0)
    nk = keys.shape[0]

    zero = jnp.zeros((d, pw), BF16)
    for part in parts:
        q = jnp.concatenate([qt_ref[0, h * d:(h + 1) * d, cols] for h in range(part * ph, (part + 1) * ph)], axis=1)
        qbd = jnp.concatenate([q if g == group(part) else zero for g in range(SWA_KV_HEADS)], axis=0)
        s_ref[part, 0:nk, :] = jnp.dot(keys, qbd, preferred_element_type=F32)

    def tiles(part):
        off = 0
        for (_, size), bias in zip(key_rows, biases):
            for r0 in range(0, size, kt):
                t = s_ref[part, off + r0:off + r0 + kt, :]
                if bias is not None:
                    t = t + jnp.concatenate([bias[r0:r0 + kt]] * ph, axis=1)
                yield off + r0, t
            off += size

    sinks, tops = [], []
    for part in parts:
        sink = sink_ref[:, part * pw:(part + 1) * pw] * LOG2E
        best = None
        for _, t in tiles(part):
            best = t if best is None else jnp.maximum(best, t)
        top = jnp.maximum(jnp.max(best, axis=0, keepdims=True), sink)
        for r0, t in tiles(part):
            p_ref[part, r0:r0 + kt, :] = jnp.exp2(t - top).astype(BF16)
        sinks.append(sink)
        tops.append(top)

    ones = jnp.ones((SWA_DEN_ROWS, nk), BF16)
    vts = [jnp.concatenate([vt_ref[0, g * d:(g + 1) * d, pl.ds(start, size)] for start, size in key_rows] , axis=1)
           for g in range(SWA_KV_HEADS)]
    heads = []
    for part in parts:
        o_ext = jnp.dot(jnp.concatenate([vts[group(part)], ones], axis=0), p_ref[part, 0:nk, :],
                        preferred_element_type=F32)
        scaled = o_ext[0:d] / (o_ext[d:d + 1] + jnp.exp2(sinks[part] - tops[part]))
        heads += [scaled[:, r * w:(r + 1) * w] for r in range(ph)]
    o_t = jnp.concatenate(heads, axis=0)
    y_t = o_t * lax.rsqrt(jnp.mean(o_t * o_t, axis=0, keepdims=True) + 1e-6) * og_ref[...]
    out = lax.dot_general(eye_ref[...], y_t.astype(BF16), (((1,), (1,)), ((), ())), preferred_element_type=F32)
    out_ref[0, cols, :] = out.astype(BF16)


def _swa_kernel(qt_ref, k_ref, vt_ref, sink_ref, og_ref, eye_ref, out_ref, s_ref, p_ref, *, t_lat, t_all, t_out):
    w = SWA_WINDOW
    first = pl.program_id(1) * SWA_STEP_BLOCKS
    n_lat = t_lat // w
    assert n_lat % SWA_STEP_BLOCKS == 0
    ctx_rows = (t_lat, t_all - t_lat)
    attend = functools.partial(_swa_attend, qt_ref, k_ref, vt_ref, sink_ref, og_ref, eye_ref, out_ref, s_ref, p_ref)

    @pl.when(first < n_lat)
    def _latent():
        blk = lambda i: (pl.multiple_of(i * w, w), w)
        sk = lax.broadcasted_iota(jnp.int32, (w, w), 0)
        tq = lax.broadcasted_iota(jnp.int32, (w, w), 1)
        neg = jnp.full((w, w), -jnp.inf, F32)
        zero = jnp.zeros((w, w), F32)
        for u in range(SWA_STEP_BLOCKS):
            n = first + u
            b_prev = jnp.where((sk >= tq) & (n >= 1), zero, neg)
            b_next = jnp.where((sk <= tq) & (n < n_lat - 1), zero, neg)
            attend(u, [ctx_rows, blk(jnp.maximum(n - 1, 0)), blk(n), blk(jnp.minimum(n + 1, n_lat - 1))],
                   [None, b_prev, None, b_next])

    if t_out > t_lat:
        @pl.when(first >= n_lat)
        def _context():
            for u in range(SWA_STEP_BLOCKS):
                attend(u, [ctx_rows], [None])


def _swa(sqt, sk, svt, sink_t, out_g, eye, *, t_lat, t_out):
    bsz, t_all, _ = sk.shape
    w = SWA_WINDOW
    n_keys = t_all - t_lat + 3 * w
    sb = SWA_STEP_BLOCKS
    n_parts = SWA_HEADS // SWA_PART_HEADS
    return pl.pallas_call(
        functools.partial(_swa_kernel, t_lat=t_lat, t_all=t_all, t_out=t_out),
        grid=(bsz, t_out // (sb * w)),
        in_specs=[pl.BlockSpec((1, SWA_Q, sb * w), lambda b, n: (b, 0, n)),
                  pl.BlockSpec((1, t_all, SWA_KV), lambda b, n: (b, 0, 0)),
                  pl.BlockSpec((1, SWA_KV, t_all), lambda b, n: (b, 0, 0)),
                  _const_spec(sink_t.shape), _const_spec(out_g.shape), _const_spec(eye.shape)],
        out_specs=pl.BlockSpec((1, sb * w, SWA_Q), lambda b, n: (b, n, 0)),
        out_shape=jax.ShapeDtypeStruct((bsz, t_out, SWA_Q), BF16),
        scratch_shapes=[pltpu.VMEM((sb, n_parts, n_keys, SWA_PART_HEADS * w), F32),
                        pltpu.VMEM((sb, n_parts, n_keys, SWA_PART_HEADS * w), BF16)],
        compiler_params=_cparams("parallel", "parallel"),
        name="swa",
    )(sqt, sk, svt, sink_t, out_g, eye)


FFN_CHUNKS = ((0, 1536), (1536, 1280))


def _post_kernel(*refs, t_lat, tm, ctx_row, n_stream, ctx_separate):
    a_ref, b_ref, c_ref, mod_ref, n1_ref, n2a_ref, n2b_ref, wo_ref, wgu_ref, wd_ref, out_ref = refs[n_stream:]
    b = pl.program_id(0)
    j = pl.program_id(1)
    x = _stream_tile(refs[:n_stream], j, tm=tm, t_lat=t_lat, ctx_separate=ctx_separate)
    mod = functools.partial(_mod_vectors, mod_ref, b, j, t_lat=t_lat, tm=tm, ctx_row=ctx_row)
    split = _row_split(t_lat, tm)
    gate1, shift2, scale2, gate2 = mod(2), mod(3), mod(4), mod(5)
    n1, n2a, n2b = n1_ref[...], n2a_ref[...], n2b_ref[...]

    cat = jnp.concatenate([a_ref[0], b_ref[0], c_ref[0]], axis=-1)
    y = jnp.dot(cat, wo_ref[...], preferred_element_type=F32)
    x1 = _by_rows(lambda p, x, y: x + gate1[p] * _rms(y, n1), split, x, y)
    h = _by_rows(lambda p, x: (_rms(x, n2a) * (1.0 + scale2[p]) + shift2[p]).astype(BF16), split, x1)
    f = None
    for off, n in FFN_CHUNKS:
        gt = jnp.dot(h, wgu_ref[:, off:off + n], preferred_element_type=F32)
        up = jnp.dot(h, wgu_ref[:, D_FF + off:D_FF + off + n], preferred_element_type=F32)
        act = (_silu(gt) * up).astype(BF16)
        part = jnp.dot(act, wd_ref[off:off + n, :], preferred_element_type=F32)
        f = part if f is None else f + part
    out_ref[0] = _by_rows(lambda p, x, f: x + gate2[p] * _rms(f, n2b), split, x1, f)


def _post(a, bo, c, x, ctx, mod, n1_post, n2_pre, n2_post, wo, wgu, wd, *, t_lat, t_out, tm, ctx_row):
    bsz = x.shape[0]
    stream, stream_specs = _stream_specs(x, ctx, tm)
    tok = lambda n: pl.BlockSpec((1, tm, n), lambda b, j: (b, j, 0))
    once = lambda arr: pl.BlockSpec(arr.shape, lambda b, j: (0,) * arr.ndim, pipeline_mode=pl.Buffered(1))
    return pl.pallas_call(
        functools.partial(_post_kernel, t_lat=t_lat, tm=tm, ctx_row=ctx_row, n_stream=len(stream),
                          ctx_separate=ctx is not None),
        grid=(bsz, t_out // tm),
        in_specs=stream_specs + [tok(GLA_V), tok(GMLP_WIDTH), tok(SWA_Q), _const_spec(mod.shape),
                                 _const_spec(n1_post.shape), _const_spec(n2_pre.shape), _const_spec(n2_post.shape),
                                 once(wo), once(wgu), once(wd)],
        out_specs=tok(D_MODEL),
        out_shape=jax.ShapeDtypeStruct((bsz, t_out, D_MODEL), F32),
        compiler_params=_cparams("parallel", "parallel"),
        name="post",
    )(*stream, a, bo, c, mod, n1_post, n2_pre, n2_post, wo, wgu, wd)


def _rope_tables(t_lat, t_ctx):
    rows = t_lat // GRID_W
    row = jnp.repeat(jnp.arange(rows), GRID_W).astype(F32)
    col = jnp.tile(jnp.arange(GRID_W), rows).astype(F32)
    inv_freq = jnp.power(ROPE_THETA, -jnp.arange(0, ROPE_AXIS_DIM, 2, dtype=F32) / ROPE_AXIS_DIM)
    ang_row = row[:, None] * inv_freq[None, :]
    ang_col = col[:, None] * inv_freq[None, :]
    ang = jnp.concatenate([ang_row, ang_row, ang_col, ang_col], axis=-1)
    sign = jnp.tile(jnp.concatenate([-jnp.ones((16,), F32), jnp.ones((16,), F32)]), 2)
    cos = jnp.tile(jnp.concatenate([jnp.cos(ang), jnp.ones((t_ctx, HEAD_DIM), F32)], axis=0), (1, 2))
    sin = jnp.tile(jnp.concatenate([jnp.sin(ang) * sign, jnp.zeros((t_ctx, HEAD_DIM), F32)], axis=0), (1, 2))
    return cos, sin, cos.T, sin.T


def _inproj_weights(w, wa2, ba):
    r = GLA_GATE_RANK
    o_code = 2 * GLA_QK + 2 * GLA_V
    o_gm = o_code + 2 * r
    o_sq = o_gm + 2 * GMLP_WIDTH
    o_sk = o_sq + SWA_Q
    o_sv = o_sk + SWA_KV
    code = jnp.concatenate([w[:, o_code:o_gm], jnp.zeros((w.shape[0], COL_CODE - 2 * r), w.dtype)], axis=1)
    w_tok = jnp.concatenate([w[:, 0:o_code], w[:, o_gm:o_sq], w[:, o_sk:o_sv], code], axis=1)
    w_t = jnp.concatenate([w[:, o_sq:o_sk], w[:, o_sv:], w[:, GLA_QK:2 * GLA_QK]], axis=1).T
    wa = jnp.zeros((COL_CODE, 2 * GLA_QK), F32)
    wa = wa.at[0:r, 0:GLA_QK].set(wa2[0]).at[r:2 * r, GLA_QK:].set(wa2[1])
    return w_tok.astype(BF16), w_t.astype(BF16), wa.astype(BF16), ba.reshape(1, 2 * GLA_QK)


def _gla_consts():
    a3, a3t_neg, mask_w, bd_c = _gla_fast_tables()
    amat, mask = _gla_tables()
    lane_head = np.arange(GLA_QK)[None, :] // GLA_DK
    row_head = np.arange(GLA_HEADS * GLA_CHUNK)[:, None] // GLA_CHUNK
    hq = (lane_head == row_head).astype(np.float32)
    vlane_head = np.arange(GLA_V)[None, :] // GLA_DV
    bdv = (vlane_head == row_head).astype(np.float32)
    bds = (vlane_head == (np.arange(GLA_QK)[:, None] // GLA_DK)).astype(np.float32)
    ind = (vlane_head == vlane_head.T).astype(np.float32) / GLA_DV
    return (jnp.asarray(a3, BF16), jnp.asarray(a3t_neg, BF16), jnp.asarray(mask_w, BF16),
            jnp.asarray(bd_c, F32), jnp.asarray(amat, BF16), jnp.asarray(mask, F32), jnp.asarray(hq, BF16),
            jnp.asarray(bdv, F32), jnp.asarray(bds, F32), jnp.asarray(ind, BF16))


def kernel(x, c, ctx, c_ctx, mod_w, mod_b, n1_pre, n1_post, n2_pre, n2_post, w_in, w_out, gla_wa2, gla_ba,
           gla_norm, gmlp_ln_g, gmlp_ln_b, gmlp_ws, gmlp_bs, gmlp_out_g, swa_sink, swa_out_g, ffn_w_gu,
           ffn_w_down):
    bsz, t_lat, d = x.shape
    t_ctx = ctx.shape[1]
    t_all = t_lat + t_ctx
    depth = mod_w.shape[0]
    assert d == D_MODEL and bsz < MOD_ROWS
    assert t_lat % 1024 == 0 and t_ctx % 256 == 0 and t_all % 768 == 0
    ctx_row = bsz
    tm_all = 768
    assert t_ctx == tm_all - _row_split(t_lat, tm_all)

    cc = jnp.zeros((MOD_ROWS, d), F32).at[0:bsz].set(c).at[ctx_row].set(c_ctx)
    mods = _modulation(cc, mod_w, mod_b)
    rope = _rope_tables(t_lat, t_ctx)
    gla_consts = _gla_consts()
    eye = jnp.eye(SWA_WINDOW, dtype=BF16)
    gmask = jnp.asarray((np.arange(GMLP_WIDTH)[None, :] // GMLP_GDIM ==
                         np.arange(GMLP_GROUPS * GMLP_CHUNK)[:, None] // GMLP_CHUNK).astype(np.float32))
    row = lambda v: v.reshape(1, -1)

    assert t_ctx == STREAM_PIECE
    xs, xs_ctx = x, ctx
    for l in range(depth):
        last = l == depth - 1
        t_out = t_lat if last else t_all
        gla, lg3, gm, sk, sqt, svt, kt, lgt3 = _inproj(
            xs, xs_ctx, mods[l], row(n1_pre[l]), _inproj_weights(w_in[l], gla_wa2[l], gla_ba[l]), rope,
            t_lat=t_lat, t_all=t_all, tm=tm_all, ctx_row=ctx_row)
        a_out = _gla(gla, lg3, kt, lgt3, gla_consts, row(jnp.tile(gla_norm[l], GLA_HEADS)),
                     t_lat=t_lat, t_out=t_out)
        ws_s = gmlp_ws[l].reshape(GMLP_GROUPS * GMLP_CHUNK, GMLP_CHUNK).astype(BF16)
        bs_t = jnp.repeat(gmlp_bs[l].T, GMLP_GDIM, axis=1)
        b_out = _gmlp(gm, row(gmlp_ln_g[l]), row(gmlp_ln_b[l]), ws_s, bs_t, gmask, row(gmlp_out_g[l]),
                      t_out=t_out, tm=1024 if last else tm_all)
        sink_t = row(jnp.repeat(swa_sink[l], SWA_WINDOW))
        out_g_t = jnp.broadcast_to(swa_out_g[l][:, None], (SWA_Q, SWA_WINDOW))
        c_out = _swa(sqt, sk, svt, sink_t, out_g_t, eye, t_lat=t_lat, t_out=t_out)
        xs = _post(a_out, b_out, c_out, xs, xs_ctx, mods[l], row(n1_post[l]), row(n2_pre[l]), row(n2_post[l]),
                   w_out[l].astype(BF16), ffn_w_gu[l].astype(BF16), ffn_w_down[l].astype(BF16),
                   t_lat=t_lat, t_out=t_out, tm=1024 if last else tm_all, ctx_row=ctx_row)
        xs_ctx = None
    return xs
```

```python
import functools

import numpy as np
import jax
import jax.numpy as jnp
from jax import lax
from jax.experimental import pallas as pl
from jax.experimental.pallas import tpu as pltpu

F32 = jnp.float32
BF16 = jnp.bfloat16

D_MODEL = 1024
GRID_W = 64
HEAD_DIM = 64
GLA_HEADS = 4
GLA_DK = 32
GLA_DV = 64
GLA_QK = GLA_HEADS * GLA_DK
GLA_V = GLA_HEADS * GLA_DV
GLA_GATE_RANK = 16
GLA_GATE_TAU = 16.0
GLA_CHUNK = 64
GMLP_GROUPS = 4
GMLP_GDIM = 64
GMLP_WIDTH = GMLP_GROUPS * GMLP_GDIM
GMLP_CHUNK = 128
SWA_HEADS = 8
SWA_KV_HEADS = 2
SWA_REP = SWA_HEADS // SWA_KV_HEADS
SWA_Q = SWA_HEADS * HEAD_DIM
SWA_KV = SWA_KV_HEADS * HEAD_DIM
SWA_WINDOW = 128
ROPE_AXIS_DIM = HEAD_DIM // 2
ROPE_THETA = 10000.0
MIX_WIDTH = GLA_V + GMLP_WIDTH + SWA_Q
D_FF = -(-8 * D_MODEL // (3 * 256)) * 256
N_MOD = 6
MOD_ROWS = 24

COL_GLA = 2 * GLA_QK + 2 * GLA_V
COL_GM = 2 * GMLP_WIDTH
COL_SK = SWA_KV
COL_CODE = 128
OFF_GM = COL_GLA
OFF_SK = OFF_GM + COL_GM
OFF_CODE = OFF_SK + COL_SK
ROW_SQ = 0
ROW_SV = ROW_SQ + SWA_Q
ROW_GK = ROW_SV + SWA_KV

LOG2E = 1.4426950408889634
SWA_Q_SCALE = HEAD_DIM ** -0.5 * LOG2E
SWA_KEY_TILE = 32
SWA_PART_HEADS = 2
SWA_STEP_BLOCKS = (4, 3)
SWA_DEN_ROWS = 16

VMEM_LIMIT = 56 * 1024 * 1024
GLA_LEVELS = (1, 2, 4, 8, 16, 32)


def _cparams(*sem):
    return pltpu.CompilerParams(dimension_semantics=sem, vmem_limit_bytes=VMEM_LIMIT)


def _const_spec(shape):
    nd = len(shape)
    return pl.BlockSpec(shape, lambda *_: (0,) * nd)


def _rms(x, g, eps=1e-6):
    return x * lax.rsqrt(jnp.mean(x * x, axis=-1, keepdims=True) + eps) * g


def _silu(x):
    return x * jax.nn.sigmoid(x)


def _row_split(t_lat, tm):
    return t_lat % tm if t_lat % tm else tm


def _mod_vectors(mod_ref, b, j, idx, *, t_lat, tm, ctx_row):
    lo = idx * D_MODEL
    lat = mod_ref[pl.ds(b, 1), lo:lo + D_MODEL]
    ctx = mod_ref[ctx_row:ctx_row + 1, lo:lo + D_MODEL]
    return lat, jnp.where(j * tm + _row_split(t_lat, tm) >= t_lat, ctx, lat)


STREAM_PIECE = 256


def _stream_specs(x, ctx, tm, tile=lambda b, j: (b, j)):
    rows, d = x.shape[1:]
    per_tile = tm // STREAM_PIECE
    last = rows // STREAM_PIECE - 1

    def piece(k):
        def index(b, j):
            tb, tj = tile(b, j)
            return tb, jnp.minimum(tj * per_tile + k, last), 0
        return pl.BlockSpec((1, STREAM_PIECE, d), index)

    operands, specs = [x] * per_tile, [piece(k) for k in range(per_tile)]
    if ctx is not None:
        assert ctx.shape[1] == STREAM_PIECE
        operands.append(ctx)
        specs.append(pl.BlockSpec((1, STREAM_PIECE, d), lambda b, j: (tile(b, j)[0], 0, 0)))
    return operands, specs


def _stream_tile(refs, j, *, tm, t_lat, ctx_separate):
    pieces = [r[0] for r in refs[:tm // STREAM_PIECE]]
    if ctx_separate:
        pieces[-1] = jnp.where((j + 1) * tm - STREAM_PIECE >= t_lat, refs[-1][0], pieces[-1])
    return jnp.concatenate(pieces, axis=0)


def _by_rows(fn, split, *arrays):
    tm = arrays[0].shape[0]
    if split == tm:
        return fn(0, *arrays)
    return jnp.concatenate([fn(0, *[a[0:split] for a in arrays]), fn(1, *[a[split:tm] for a in arrays])], axis=0)


def _mod_kernel(c_ref, w_ref, b_ref, o_ref):
    s = _silu(c_ref[...]).astype(BF16)
    o_ref[0] = jnp.dot(s, w_ref[0].astype(BF16), preferred_element_type=F32) + b_ref[0]


def _modulation(cc, mod_w, mod_b):
    depth, d, n = mod_w.shape
    tn = n // 4
    return pl.pallas_call(
        _mod_kernel,
        grid=(depth, n // tn),
        in_specs=[pl.BlockSpec((MOD_ROWS, d), lambda l, j: (0, 0)),
                  pl.BlockSpec((1, d, tn), lambda l, j: (l, 0, j)),
                  pl.BlockSpec((1, 1, tn), lambda l, j: (l, 0, j))],
        out_specs=pl.BlockSpec((1, MOD_ROWS, tn), lambda l, j: (l, 0, j)),
        out_shape=jax.ShapeDtypeStruct((depth, MOD_ROWS, n), F32),
        compiler_params=_cparams("parallel", "parallel"),
        name="modulation",
    )(cc, mod_w, mod_b.reshape(depth, 1, n))


def _rope(z, cos, sin):
    lane = lax.broadcasted_iota(jnp.int32, z.shape, 1)
    first = (lane % 32) < 16
    rot = jnp.where(first, pltpu.roll(z, 128 - 16, 1), pltpu.roll(z, 16, 1))
    return z * cos + rot * sin


def _split3(x):
    p1 = x.astype(BF16)
    r1 = x - p1.astype(F32)
    p2 = r1.astype(BF16)
    p3 = (r1 - p2.astype(F32)).astype(BF16)
    return p1, p2, p3


def _log_gate(z):
    return (jnp.minimum(z, 0.0) - jnp.log(1.0 + jnp.exp(-jnp.abs(z)))) * (1.0 / GLA_GATE_TAU)


def _rope_t(z, cos_t, sin_t):
    parts = []
    for r0 in range(0, z.shape[0], 32):
        parts += [z[r0 + 16:r0 + 32], z[r0:r0 + 16]]
    rot = jnp.concatenate(parts, axis=0)
    reps = z.shape[0] // cos_t.shape[0]
    return z * jnp.concatenate([cos_t] * reps, axis=0) + rot * jnp.concatenate([sin_t] * reps, axis=0)


def _inproj_kernel(*refs, t_lat, tm, ctx_row, n_stream, ctx_separate):
    (mod_ref, g_ref, w_ref, wt_ref, wa_ref, ba_ref, cos_ref, sin_ref, cost_ref, sint_ref,
     gla_ref, lg_ref, gm_ref, sk_ref, sqt_ref, svt_ref, kt_ref, lgt_ref, h_buf, ht_buf) = refs[2 * n_stream:]
    b = pl.program_id(0)
    j = pl.program_id(1)
    nt = pl.num_programs(1)
    step = b * nt + j
    slot = lax.rem(step, 2)

    def prologue(stream_refs, tb, tj, dst):
        x = _stream_tile(stream_refs, tj, tm=tm, t_lat=t_lat, ctx_separate=ctx_separate)
        mod = functools.partial(_mod_vectors, mod_ref, tb, tj, t_lat=t_lat, tm=tm, ctx_row=ctx_row)
        shift, scale, gain = mod(0), mod(1), g_ref[...]
        hf = _by_rows(lambda part, x: _rms(x, gain) * (1.0 + scale[part]) + shift[part], _row_split(t_lat, tm), x)
        h_buf[dst] = hf.astype(BF16)
        ht_buf[dst] = hf.T.astype(BF16)

    @pl.when(step == 0)
    def _first():
        prologue(refs[n_stream:2 * n_stream], 0, 0, 0)

    def project(cur):
        def proj(off, n):
            return jnp.dot(h_buf[cur], w_ref[:, off:off + n], preferred_element_type=F32)

        def proj_t(off, n):
            return jnp.dot(wt_ref[off:off + n, :], ht_buf[cur], preferred_element_type=F32)

        code = proj(OFF_CODE, COL_CODE).astype(BF16)
        z = proj(0, COL_GLA)
        gla_ref[0, :, 0:GLA_QK] = (z[:, 0:GLA_QK] * (GLA_DK ** -0.5)).astype(BF16)
        gla_ref[0, :, GLA_QK:COL_GLA] = z[:, GLA_QK:COL_GLA].astype(BF16)

        n2 = 2 * GLA_QK
        lg = _log_gate(jnp.dot(code, wa_ref[...], preferred_element_type=F32) + ba_ref[...])
        for i, p in enumerate(_split3(lg)):
            lg_ref[0, :, i * n2:(i + 1) * n2] = p
        for i, p in enumerate(_split3(lg.T)):
            lgt_ref[0, i * n2:(i + 1) * n2, :] = p

        gm_ref[0] = proj(OFF_GM, COL_GM).astype(BF16)
        nb, nj = _next_tile(b, j, pl.num_programs(0), nt)
        prologue(refs[:n_stream], nb, nj, 1 - cur)
        kt_ref[0] = proj_t(ROW_GK, GLA_QK).astype(BF16)

        sk_ref[0] = _rope(proj(OFF_SK, COL_SK), cos_ref[...], sin_ref[...]).astype(BF16)
        sqt_ref[0] = (_rope_t(proj_t(ROW_SQ, SWA_Q), cost_ref[...], sint_ref[...]) * SWA_Q_SCALE).astype(BF16)
        svt_ref[0] = proj_t(ROW_SV, SWA_KV).astype(BF16)

    for cur in range(2):
        pl.when(slot == cur)(functools.partial(project, cur))


def _next_tile(b, j, nb, nt):
    wrap = j + 1 == nt
    last = jnp.logical_and(wrap, b + 1 == nb)
    return jnp.where(wrap & jnp.logical_not(last), b + 1, b), jnp.where(last, j, jnp.where(wrap, 0, j + 1))


def _inproj(x, ctx, mod, n_pre, weights, rope, *, t_lat, t_all, tm, ctx_row):
    bsz, _, d = x.shape
    nt = t_all // tm
    nxt, nxt_specs = _stream_specs(x, ctx, tm, functools.partial(_next_tile, nb=bsz, nt=nt))
    first, first_specs = _stream_specs(x, ctx, tm, lambda b, j: (0, 0))
    stream, stream_specs = nxt + first, nxt_specs + first_specs
    tok = lambda n: pl.BlockSpec((1, tm, n), lambda b, j: (b, j, 0))
    tok_t = lambda n: pl.BlockSpec((1, n, tm), lambda b, j: (b, 0, j))
    n2 = 2 * GLA_QK
    cols = (COL_GLA, 3 * n2, COL_GM, COL_SK)
    rows = (SWA_Q, SWA_KV, GLA_QK, 3 * n2)
    out_shape = [jax.ShapeDtypeStruct((bsz, t_all, n), BF16) for n in cols]
    out_shape += [jax.ShapeDtypeStruct((bsz, n, t_all), BF16) for n in rows]
    consts = (mod, n_pre) + tuple(weights)
    cos, sin, cos_t, sin_t = rope
    return pl.pallas_call(
        functools.partial(_inproj_kernel, t_lat=t_lat, tm=tm, ctx_row=ctx_row, n_stream=len(nxt),
                          ctx_separate=ctx is not None),
        grid=(bsz, nt),
        in_specs=stream_specs + [_const_spec(a.shape) for a in consts] +
                 [pl.BlockSpec((tm, 128), lambda b, j: (j, 0)), pl.BlockSpec((tm, 128), lambda b, j: (j, 0)),
                  pl.BlockSpec((128, tm), lambda b, j: (0, j)), pl.BlockSpec((128, tm), lambda b, j: (0, j))],
        out_specs=[tok(n) for n in cols] + [tok_t(n) for n in rows],
        out_shape=out_shape,
        scratch_shapes=[pltpu.VMEM((2, tm, d), BF16), pltpu.VMEM((2, d, tm), BF16)],
        compiler_params=_cparams("arbitrary", "arbitrary"),
        name="inproj",
    )(*stream, *consts, cos, sin, cos_t, sin_t)


GLA_FAST_CHUNK = 128
GLA_FAST_UNROLL = 2
GLA_SAFE_LOG_DECAY = -40.0


def _gla_fast_tables():
    c = GLA_FAST_CHUNK
    t = np.arange(c)[:, None]
    r = np.arange(c)[None, :]
    incl = [(r <= t), (r >= t)]
    a3 = np.stack([np.tile(m, (1, 3)) for m in incl]).astype(np.float32)
    a3t_neg = -np.stack([np.tile(m.T, (3, 1)) for m in incl]).astype(np.float32)
    mask_w = np.stack([np.tile(m, (1, GLA_HEADS)) for m in incl]).astype(np.float32)
    half = (np.arange(128)[None, :] // GLA_DV) == ((np.arange(GLA_QK)[:, None] // GLA_DK) % 2)
    return a3, a3t_neg, mask_w, half.astype(np.float32)


def _gla_fast_chunks(jobs, a3, a3t_neg, mask_w, bd_c, states):
    c = GLA_FAST_CHUNK
    dk, hd = GLA_DK, GLA_QK
    cums = [(jnp.dot(a3[d], g3, preferred_element_type=F32),
             jnp.dot(gt3, a3t_neg[d], preferred_element_type=F32))
            for d, _, _, _, g3, gt3 in jobs]

    zr = lambda n: jnp.zeros((n, c), BF16)
    qes, dcols, khs, atts = [], [], [], []
    for (d, q, kt, _, _, _), (cum, ncum_t) in zip(jobs, cums):
        qe = (q.astype(F32) * jnp.exp(cum)).astype(BF16)
        ke_t = kt.astype(F32) * jnp.exp(ncum_t)
        edge = 0 if d else c - 1
        dcol = jnp.exp(-ncum_t[:, edge:edge + 1])
        khs.append((ke_t * dcol).astype(BF16))
        ke_t = ke_t.astype(BF16)
        cols = []
        for hh in range(GLA_HEADS):
            parts = ([zr(hh * dk)] if hh else []) + [ke_t[hh * dk:(hh + 1) * dk]]
            parts += [zr(hd - (hh + 1) * dk)] if hh < GLA_HEADS - 1 else []
            cols.append(jnp.concatenate(parts, axis=0))
        kbd = jnp.concatenate(cols, axis=1)
        atts.append(jnp.dot(qe, kbd, preferred_element_type=F32))
        qes.append(qe)
        dcols.append(dcol)

    kvs = [jnp.concatenate([jnp.dot(kh[0:hd // 2], v[:, 0:128], preferred_element_type=F32),
                            jnp.dot(kh[hd // 2:hd], v[:, 128:256], preferred_element_type=F32)], axis=0)
           for (_, _, _, v, _, _), kh in zip(jobs, khs)]

    lane = lax.broadcasted_iota(jnp.int32, (1, 128), 1)
    m_lo = (lane < GLA_DV).astype(BF16)
    m_hi = (lane >= GLA_DV).astype(BF16)
    z128 = jnp.zeros((c, 128), BF16)
    z64 = jnp.zeros((hd // 2, 128), BF16)
    states = list(states)
    outs = []
    for (d, _, _, v, _, _), qe, dcol, att, kv in zip(jobs, qes, dcols, atts, kvs):
        v_lo, v_hi = v[:, 0:128], v[:, 128:256]
        vbd = jnp.concatenate([jnp.concatenate([v_lo * m_lo, z128], axis=1),
                               jnp.concatenate([v_lo * m_hi, z128], axis=1),
                               jnp.concatenate([z128, v_hi * m_lo], axis=1),
                               jnp.concatenate([z128, v_hi * m_hi], axis=1)], axis=0)
        sb = states[d].astype(BF16)
        s_full = jnp.concatenate([jnp.concatenate([sb[0:hd // 2], z64], axis=1),
                                  jnp.concatenate([z64, sb[hd // 2:hd]], axis=1)], axis=0)
        lhs = jnp.concatenate([att.astype(BF16) * mask_w[d], qe], axis=1)
        outs.append(jnp.dot(lhs, jnp.concatenate([vbd, s_full], axis=0),
                            preferred_element_type=F32))
        states[d] = states[d] * dcol + kv * bd_c
    return outs, states


def _gla_tables():
    c = GLA_CHUNK
    t = np.arange(c)[:, None]
    r = np.arange(c)[None, :]
    sizes = [2 * b for b in GLA_LEVELS]
    same = lambda b: (t // b) == (r // b)
    a_f = [same(b) & (r <= t) for b in sizes] + [same(b) & (r > t) for b in sizes]
    a_b = [same(b) & (r >= t) for b in sizes] + [same(b) & (r < t) for b in sizes]
    m_f = [((t // b) % 2 == 1) & ((r // b) == (t // b) - 1) for b in GLA_LEVELS] + [t == r]
    m_b = [((t // b) % 2 == 0) & ((r // b) == (t // b) + 1) for b in GLA_LEVELS] + [t == r]
    amat = np.stack([np.tile(np.concatenate(a, 0), (1, 3)) for a in (a_f, a_b)]).astype(np.float32)
    mask = np.stack([np.stack([np.tile(m, (GLA_HEADS, 1)) for m in ms]) for ms in (m_f, m_b)])
    return amat, mask.astype(np.float32)


def _gla_chunk(q, k, v, g3, amat, masks, hmask_q, bd_v, bd_s, s_prev, *, backward):
    c = GLA_CHUNK
    nl = len(GLA_LEVELS)
    g = g3[0:c].astype(F32) + g3[c:2 * c].astype(F32) + g3[2 * c:3 * c].astype(F32)
    ps = jnp.dot(amat, g3, preferred_element_type=F32)
    blk = lambda i: ps[i * c:(i + 1) * c]
    q_exp = [g] + [blk(i) for i in range(nl - 1)]
    k_exp = [None] + [blk(nl + i) for i in range(nl - 1)]
    q_full = blk(nl - 1)
    k_full = blk(2 * nl - 1)

    att = None
    for lvl in range(nl + 1):
        if lvl < nl:
            qe = q * jnp.exp(q_exp[lvl])
            ke = k if k_exp[lvl] is None else k * jnp.exp(k_exp[lvl])
        else:
            qe, ke = q, k
        qbd = jnp.concatenate([qe.astype(BF16)] * GLA_HEADS, axis=0) * hmask_q
        p = lax.dot_general(qbd, ke.astype(BF16), (((1,), (1,)), ((), ())), preferred_element_type=F32)
        p = p * masks[lvl]
        att = p if att is None else att + p

    r = jnp.dot(att.astype(BF16), v, preferred_element_type=F32) * bd_v
    o = r[0:c]
    for hh in range(1, GLA_HEADS):
        o = o + r[hh * c:(hh + 1) * c]
    o = o + jnp.dot((q * jnp.exp(q_full)).astype(BF16), s_prev.astype(BF16), preferred_element_type=F32)

    tot = q_full[0:1] if backward else q_full[c - 1:c]
    dcol = jnp.transpose(jnp.broadcast_to(jnp.exp(tot), (GLA_QK, GLA_QK)))
    dcol = jnp.concatenate([dcol, dcol], axis=1)
    kv = lax.dot_general((k * jnp.exp(k_full)).astype(BF16), v, (((0,), (0,)), ((), ())),
                         preferred_element_type=F32)
    return o, s_prev * dcol + kv * bd_s


def _gla_kernel(gla_ref, lg_ref, kt_ref, lgt_ref, a3_ref, a3t_ref, mw_ref, bdc_ref, amat_ref, mask_ref,
                hq_ref, bdv_ref, bds_ref, ind_ref, norm_ref, out_ref, of_ref, ob_ref, *, t_lat, t_all, t_out):
    n2 = 2 * GLA_QK
    o_v = 2 * GLA_QK

    def pieces(rows, d):
        return jnp.concatenate([lg_ref[0, rows, i * n2 + d * GLA_QK:i * n2 + (d + 1) * GLA_QK]
                                for i in range(3)], axis=0)

    cf_ = GLA_FAST_CHUNK
    worst = None
    for ci in range(t_all // cf_):
        tot = jnp.sum(lg_ref[0, ci * cf_:(ci + 1) * cf_, 0:n2].astype(F32), axis=0, keepdims=True)
        worst = tot if worst is None else jnp.minimum(worst, tot)
    safe = jnp.min(worst) >= GLA_SAFE_LOG_DECAY

    @pl.when(safe)
    def _fast():
        c = GLA_FAST_CHUNK
        n_lat = t_lat // c
        n_all = t_all // c
        per_step = GLA_FAST_UNROLL
        assert n_all % per_step == 0

        def job(ci, d):
            rows = pl.ds(pl.multiple_of(ci * c, c), c)
            gt3 = jnp.concatenate([lgt_ref[0, i * n2 + d * GLA_QK:i * n2 + (d + 1) * GLA_QK, rows]
                                   for i in range(3)], axis=1)
            return rows, (d, gla_ref[0, rows, 0:GLA_QK], kt_ref[0, :, rows], gla_ref[0, rows, o_v:o_v + GLA_V],
                          pieces(rows, d), gt3)

        def step(i, carry):
            chunks = [(lax.rem(i * per_step + u + n_lat, n_all), 0) for u in range(per_step)]
            chunks += [(n_all - 1 - (i * per_step + u), 1) for u in range(per_step)]
            rows, jobs = zip(*[job(ci, d) for ci, d in chunks])
            outs, states = _gla_fast_chunks(jobs, a3_ref, a3t_ref, mw_ref, bdc_ref[...], carry)
            for (_, d), r, o in zip(chunks, rows, outs):
                (ob_ref if d else of_ref)[r, :] = o
            return tuple(states)

        zero = jnp.zeros((GLA_QK, 128), F32)
        lax.fori_loop(0, n_all // per_step, step, (zero, zero))

    @pl.when(jnp.logical_not(safe))
    def _robust():
        c = GLA_CHUNK
        n_lat = t_lat // c
        n_all = t_all // c
        hq = hq_ref[...]
        bdv = bdv_ref[...]
        bds = bds_ref[...]

        def one(ci, d, s):
            start = pl.multiple_of(ci * c, c)
            rows = pl.ds(start, c)
            q = gla_ref[0, rows, 0:GLA_QK].astype(F32)
            k = gla_ref[0, rows, GLA_QK:2 * GLA_QK].astype(F32)
            v = gla_ref[0, rows, o_v:o_v + GLA_V]
            o, s = _gla_chunk(q, k, v, pieces(rows, d), amat_ref[d],
                              [mask_ref[d, l] for l in range(len(GLA_LEVELS) + 1)], hq, bdv, bds, s,
                              backward=bool(d))
            (ob_ref if d else of_ref)[rows, :] = o
            return s

        def step(i, carry):
            s_f, s_b = carry
            return one(lax.rem(i + n_lat, n_all), 0, s_f), one(n_all - 1 - i, 1, s_b)

        zero = jnp.zeros((GLA_QK, GLA_V), F32)
        lax.fori_loop(0, n_all, step, (zero, zero))

    tr = 256
    ind = ind_ref[...]
    for r0 in range(0, t_out, tr):
        o = of_ref[r0:r0 + tr, :] + ob_ref[r0:r0 + tr, :]
        sq = o * o
        hi = sq.astype(BF16)
        lo = (sq - hi.astype(F32)).astype(BF16)
        ms = jnp.dot(hi, ind, preferred_element_type=F32) + jnp.dot(lo, ind, preferred_element_type=F32)
        gate = gla_ref[0, r0:r0 + tr, o_v + GLA_V:COL_GLA].astype(F32)
        y = o * lax.rsqrt(ms + 1e-6) * norm_ref[...] * _silu(gate)
        out_ref[0, r0:r0 + tr, :] = y.astype(BF16)


def _gla(gla, lg3, kt, lgt3, consts, norm_t, *, t_lat, t_out):
    bsz, t_all, _ = gla.shape
    n2 = 2 * GLA_QK
    return pl.pallas_call(
        functools.partial(_gla_kernel, t_lat=t_lat, t_all=t_all, t_out=t_out),
        grid=(bsz,),
        in_specs=[pl.BlockSpec((1, t_all, COL_GLA), lambda b: (b, 0, 0)),
                  pl.BlockSpec((1, t_all, 3 * n2), lambda b: (b, 0, 0)),
                  pl.BlockSpec((1, GLA_QK, t_all), lambda b: (b, 0, 0)),
                  pl.BlockSpec((1, 3 * n2, t_all), lambda b: (b, 0, 0))] +
                 [_const_spec(a.shape) for a in consts] + [_const_spec(norm_t.shape)],
        out_specs=pl.BlockSpec((1, t_out, GLA_V), lambda b: (b, 0, 0)),
        out_shape=jax.ShapeDtypeStruct((bsz, t_out, GLA_V), BF16),
        scratch_shapes=[pltpu.VMEM((t_all, GLA_V), F32), pltpu.VMEM((t_all, GLA_V), F32)],
        compiler_params=_cparams("parallel"),
        name="gla",
    )(gla, lg3, kt, lgt3, *consts, norm_t)


def _gmlp_kernel(z_ref, lng_ref, lnb_ref, ws_ref, bs_ref, gm_ref, og_ref, out_ref, *, tm):
    z = z_ref[0].astype(F32)
    zf = 0.5 * z * (1.0 + lax.erf(z * (2.0 ** -0.5)))
    u = zf[:, 0:GMLP_WIDTH]
    v = zf[:, GMLP_WIDTH:]
    mu = jnp.mean(v, axis=-1, keepdims=True)
    vc = v - mu
    v = vc * lax.rsqrt(jnp.mean(vc * vc, axis=-1, keepdims=True) + 1e-5) * lng_ref[...] + lnb_ref[...]
    vb = v.astype(BF16)
    ws = ws_ref[...]
    p = GMLP_CHUNK
    for ci in range(tm // p):
        r = jnp.dot(ws, vb[ci * p:(ci + 1) * p], preferred_element_type=F32) * gm_ref[...]
        mixed = bs_ref[...]
        for g in range(GMLP_GROUPS):
            mixed = mixed + r[g * p:(g + 1) * p]
        y = u[ci * p:(ci + 1) * p] * mixed
        out_ref[0, ci * p:(ci + 1) * p, :] = _rms(y, og_ref[...]).astype(BF16)


def _gmlp(gm, ln_g, ln_b, ws_s, bs_t, gmask, out_g, *, t_out, tm):
    bsz = gm.shape[0]
    return pl.pallas_call(
        functools.partial(_gmlp_kernel, tm=tm),
        grid=(bsz, t_out // tm),
        in_specs=[pl.BlockSpec((1, tm, COL_GM), lambda b, j: (b, j, 0)),
                  _const_spec(ln_g.shape), _const_spec(ln_b.shape), _const_spec(ws_s.shape),
                  _const_spec(bs_t.shape), _const_spec(gmask.shape), _const_spec(out_g.shape)],
        out_specs=pl.BlockSpec((1, tm, GMLP_WIDTH), lambda b, j: (b, j, 0)),
        out_shape=jax.ShapeDtypeStruct((bsz, t_out, GMLP_WIDTH), BF16),
        compiler_params=_cparams("parallel", "parallel"),
        name="gmlp",
    )(gm, ln_g, ln_b, ws_s, bs_t, gmask, out_g)


def _swa_attend(qt_ref, k_ref, vt_ref, sink_ref, og_ref, eye_ref, out_ref, s_ref, p_ref, u, key_rows, biases):
    w = SWA_WINDOW
    d = HEAD_DIM
    kt = SWA_KEY_TILE
    ph = SWA_PART_HEADS
    pw = ph * w
    parts = range(SWA_HEADS // ph)
    group = lambda part: part * ph // SWA_REP
    s_ref = s_ref.at[u]
    p_ref = p_ref.at[u]
    cols = slice(u * w, (u + 1) * w)
    keys = jnp.concatenate([k_ref[0, pl.ds(start, size), :] for start, size in key_rows], axis=0)
    nk = keys.shape[0]

    zero = jnp.zeros((d, pw), BF16)
    sinks, tops = [], []
    for part in parts:
        q = jnp.concatenate([qt_ref[0, h * d:(h + 1) * d, cols] for h in range(part * ph, (part + 1) * ph)], axis=1)
        qbd = jnp.concatenate([q if g == group(part) else zero for g in range(SWA_KV_HEADS)], axis=0)
        best, off = None, 0
        for (_, size), bias in zip(key_rows, biases):
            blk = jnp.dot(keys[off:off + size], qbd, preferred_element_type=F32)
            if bias is not None:
                blk = blk + jnp.concatenate([bias] * ph, axis=1)
            s_ref[part, off:off + size, :] = blk
            m8 = jnp.max(blk.reshape(size // 8, 8, pw), axis=0)
            best = m8 if best is None else jnp.maximum(best, m8)
            off += size
        sink = sink_ref[:, part * pw:(part + 1) * pw] * LOG2E
        sinks.append(sink)
        tops.append(jnp.maximum(jnp.max(best, axis=0, keepdims=True), sink))
    yield

    for part in parts:
        for r0 in range(0, nk, kt):
            p_ref[part, r0:r0 + kt, :] = jnp.exp2(s_ref[part, r0:r0 + kt, :] - tops[part]).astype(BF16)
    yield

    ones = jnp.ones((SWA_DEN_ROWS, nk), BF16)
    vts = [jnp.concatenate([vt_ref[0, g * d:(g + 1) * d, pl.ds(start, size)] for start, size in key_rows] , axis=1)
           for g in range(SWA_KV_HEADS)]
    heads = []
    for part in parts:
        o_ext = jnp.dot(jnp.concatenate([vts[group(part)], ones], axis=0), p_ref[part, 0:nk, :],
                        preferred_element_type=F32)
        scaled = o_ext[0:d] / (o_ext[d:d + 1] + jnp.exp2(sinks[part] - tops[part]))
        heads += [scaled[:, r * w:(r + 1) * w] for r in range(ph)]
    o_t = jnp.concatenate(heads, axis=0)
    yield

    y_t = o_t * lax.rsqrt(jnp.mean(o_t * o_t, axis=0, keepdims=True) + 1e-6) * og_ref[...]
    out = lax.dot_general(eye_ref[...], y_t.astype(BF16), (((1,), (1,)), ((), ())), preferred_element_type=F32)
    out_ref[0, cols, :] = out.astype(BF16)


def _run_stages(blocks):
    for t in range(len(blocks) + 2):
        for u in (t, t - 1, t - 2, t - 1):
            if 0 <= u < len(blocks):
                next(blocks[u], None)


def _swa_kernel(qt_ref, k_ref, vt_ref, sink_ref, og_ref, eye_ref, out_ref, s_ref, p_ref, *, t_lat, t_all, t_out, sb):
    w = SWA_WINDOW
    step = pl.program_id(1)
    n_lat = t_lat // w
    ctx_rows = (t_lat, t_all - t_lat)
    attend = functools.partial(_swa_attend, qt_ref, k_ref, vt_ref, sink_ref, og_ref, eye_ref, out_ref, s_ref, p_ref)

    def run(latent):
        blk = lambda i: (pl.multiple_of(i * w, w), w)
        sk = lax.broadcasted_iota(jnp.int32, (w, w), 0)
        tq = lax.broadcasted_iota(jnp.int32, (w, w), 1)
        neg = jnp.full((w, w), -jnp.inf, F32)
        zero = jnp.zeros((w, w), F32)
        blocks = []
        for u, is_latent in enumerate(latent):
            if not is_latent:
                blocks.append(attend(u, [ctx_rows], [None]))
                continue
            n = step * sb + u
            b_prev = jnp.where((sk >= tq) & (n >= 1), zero, neg)
            b_next = jnp.where((sk <= tq) & (n < n_lat - 1), zero, neg)
            blocks.append(attend(u, [ctx_rows, blk(jnp.maximum(n - 1, 0)), blk(n), blk(jnp.minimum(n + 1, n_lat - 1))],
                                 [None, b_prev, None, b_next]))
        _run_stages(blocks)

    n_steps = t_out // (sb * w)
    makeup = [tuple(s * sb + u < n_lat for u in range(sb)) for s in range(n_steps)]
    for kind in sorted(set(makeup), reverse=True):
        steps = [s for s in range(n_steps) if makeup[s] == kind]
        assert steps == list(range(steps[0], steps[-1] + 1))
        pl.when((step >= steps[0]) & (step <= steps[-1]))(functools.partial(run, kind))


def _swa(sqt, sk, svt, sink_t, out_g, eye, *, t_lat, t_out, sb):
    bsz, t_all, _ = sk.shape
    w = SWA_WINDOW
    n_keys = t_all - t_lat + 3 * w
    n_parts = SWA_HEADS // SWA_PART_HEADS
    assert t_out % (sb * w) == 0
    return pl.pallas_call(
        functools.partial(_swa_kernel, t_lat=t_lat, t_all=t_all, t_out=t_out, sb=sb),
        grid=(bsz, t_out // (sb * w)),
        in_specs=[pl.BlockSpec((1, SWA_Q, sb * w), lambda b, n: (b, 0, n)),
                  pl.BlockSpec((1, t_all, SWA_KV), lambda b, n: (b, 0, 0)),
                  pl.BlockSpec((1, SWA_KV, t_all), lambda b, n: (b, 0, 0)),
                  _const_spec(sink_t.shape), _const_spec(out_g.shape), _const_spec(eye.shape)],
        out_specs=pl.BlockSpec((1, sb * w, SWA_Q), lambda b, n: (b, n, 0)),
        out_shape=jax.ShapeDtypeStruct((bsz, t_out, SWA_Q), BF16),
        scratch_shapes=[pltpu.VMEM((sb, n_parts, n_keys, SWA_PART_HEADS * w), F32),
                        pltpu.VMEM((sb, n_parts, n_keys, SWA_PART_HEADS * w), BF16)],
        compiler_params=_cparams("parallel", "parallel"),
        name="swa",
    )(sqt, sk, svt, sink_t, out_g, eye)


FFN_CHUNKS = ((0, 1536), (1536, 1280))


def _post_kernel(*refs, t_lat, tm, ctx_row, n_stream, ctx_separate):
    a_ref, b_ref, c_ref, mod_ref, n1_ref, n2a_ref, n2b_ref, wo_ref, wgu_ref, wd_ref, out_ref = refs[n_stream:]
    b = pl.program_id(0)
    j = pl.program_id(1)
    x = _stream_tile(refs[:n_stream], j, tm=tm, t_lat=t_lat, ctx_separate=ctx_separate)
    mod = functools.partial(_mod_vectors, mod_ref, b, j, t_lat=t_lat, tm=tm, ctx_row=ctx_row)
    split = _row_split(t_lat, tm)
    gate1, shift2, scale2, gate2 = mod(2), mod(3), mod(4), mod(5)
    n1, n2a, n2b = n1_ref[...], n2a_ref[...], n2b_ref[...]

    cat = jnp.concatenate([a_ref[0], b_ref[0], c_ref[0]], axis=-1)
    y = jnp.dot(cat, wo_ref[...], preferred_element_type=F32)
    x1 = _by_rows(lambda p, x, y: x + gate1[p] * _rms(y, n1), split, x, y)
    h = _by_rows(lambda p, x: (_rms(x, n2a) * (1.0 + scale2[p]) + shift2[p]).astype(BF16), split, x1)
    f = None
    for off, n in FFN_CHUNKS:
        gt = jnp.dot(h, wgu_ref[:, off:off + n], preferred_element_type=F32)
        up = jnp.dot(h, wgu_ref[:, D_FF + off:D_FF + off + n], preferred_element_type=F32)
        act = (_silu(gt) * up).astype(BF16)
        part = jnp.dot(act, wd_ref[off:off + n, :], preferred_element_type=F32)
        f = part if f is None else f + part
    out_ref[0] = _by_rows(lambda p, x, f: x + gate2[p] * _rms(f, n2b), split, x1, f)


def _post(a, bo, c, x, ctx, mod, n1_post, n2_pre, n2_post, wo, wgu, wd, *, t_lat, t_out, tm, ctx_row):
    bsz = x.shape[0]
    stream, stream_specs = _stream_specs(x, ctx, tm)
    tok = lambda n: pl.BlockSpec((1, tm, n), lambda b, j: (b, j, 0))
    once = lambda arr: pl.BlockSpec(arr.shape, lambda b, j: (0,) * arr.ndim, pipeline_mode=pl.Buffered(1))
    return pl.pallas_call(
        functools.partial(_post_kernel, t_lat=t_lat, tm=tm, ctx_row=ctx_row, n_stream=len(stream),
                          ctx_separate=ctx is not None),
        grid=(bsz, t_out // tm),
        in_specs=stream_specs + [tok(GLA_V), tok(GMLP_WIDTH), tok(SWA_Q), _const_spec(mod.shape),
                                 _const_spec(n1_post.shape), _const_spec(n2_pre.shape), _const_spec(n2_post.shape),
                                 once(wo), once(wgu), once(wd)],
        out_specs=tok(D_MODEL),
        out_shape=jax.ShapeDtypeStruct((bsz, t_out, D_MODEL), F32),
        compiler_params=_cparams("parallel", "parallel"),
        name="post",
    )(*stream, a, bo, c, mod, n1_post, n2_pre, n2_post, wo, wgu, wd)


def _rope_tables(t_lat, t_ctx):
    rows = t_lat // GRID_W
    row = jnp.repeat(jnp.arange(rows), GRID_W).astype(F32)
    col = jnp.tile(jnp.arange(GRID_W), rows).astype(F32)
    inv_freq = jnp.power(ROPE_THETA, -jnp.arange(0, ROPE_AXIS_DIM, 2, dtype=F32) / ROPE_AXIS_DIM)
    ang_row = row[:, None] * inv_freq[None, :]
    ang_col = col[:, None] * inv_freq[None, :]
    ang = jnp.concatenate([ang_row, ang_row, ang_col, ang_col], axis=-1)
    sign = jnp.tile(jnp.concatenate([-jnp.ones((16,), F32), jnp.ones((16,), F32)]), 2)
    cos = jnp.tile(jnp.concatenate([jnp.cos(ang), jnp.ones((t_ctx, HEAD_DIM), F32)], axis=0), (1, 2))
    sin = jnp.tile(jnp.concatenate([jnp.sin(ang) * sign, jnp.zeros((t_ctx, HEAD_DIM), F32)], axis=0), (1, 2))
    return cos, sin, cos.T, sin.T


def _inproj_weights(w, wa2, ba):
    r = GLA_GATE_RANK
    o_code = 2 * GLA_QK + 2 * GLA_V
    o_gm = o_code + 2 * r
    o_sq = o_gm + 2 * GMLP_WIDTH
    o_sk = o_sq + SWA_Q
    o_sv = o_sk + SWA_KV
    code = jnp.concatenate([w[:, o_code:o_gm], jnp.zeros((w.shape[0], COL_CODE - 2 * r), w.dtype)], axis=1)
    w_tok = jnp.concatenate([w[:, 0:o_code], w[:, o_gm:o_sq], w[:, o_sk:o_sv], code], axis=1)
    w_t = jnp.concatenate([w[:, o_sq:o_sk], w[:, o_sv:], w[:, GLA_QK:2 * GLA_QK]], axis=1).T
    wa = jnp.zeros((COL_CODE, 2 * GLA_QK), F32)
    wa = wa.at[0:r, 0:GLA_QK].set(wa2[0]).at[r:2 * r, GLA_QK:].set(wa2[1])
    return w_tok.astype(BF16), w_t.astype(BF16), wa.astype(BF16), ba.reshape(1, 2 * GLA_QK)


def _gla_consts():
    a3, a3t_neg, mask_w, bd_c = _gla_fast_tables()
    amat, mask = _gla_tables()
    lane_head = np.arange(GLA_QK)[None, :] // GLA_DK
    row_head = np.arange(GLA_HEADS * GLA_CHUNK)[:, None] // GLA_CHUNK
    hq = (lane_head == row_head).astype(np.float32)
    vlane_head = np.arange(GLA_V)[None, :] // GLA_DV
    bdv = (vlane_head == row_head).astype(np.float32)
    bds = (vlane_head == (np.arange(GLA_QK)[:, None] // GLA_DK)).astype(np.float32)
    ind = (vlane_head == vlane_head.T).astype(np.float32) / GLA_DV
    return (jnp.asarray(a3, BF16), jnp.asarray(a3t_neg, BF16), jnp.asarray(mask_w, BF16),
            jnp.asarray(bd_c, F32), jnp.asarray(amat, BF16), jnp.asarray(mask, F32), jnp.asarray(hq, BF16),
            jnp.asarray(bdv, F32), jnp.asarray(bds, F32), jnp.asarray(ind, BF16))


def kernel(x, c, ctx, c_ctx, mod_w, mod_b, n1_pre, n1_post, n2_pre, n2_post, w_in, w_out, gla_wa2, gla_ba,
           gla_norm, gmlp_ln_g, gmlp_ln_b, gmlp_ws, gmlp_bs, gmlp_out_g, swa_sink, swa_out_g, ffn_w_gu,
           ffn_w_down):
    bsz, t_lat, d = x.shape
    t_ctx = ctx.shape[1]
    t_all = t_lat + t_ctx
    depth = mod_w.shape[0]
    assert d == D_MODEL and bsz < MOD_ROWS
    assert t_lat % 1024 == 0 and t_ctx % 256 == 0 and t_all % 768 == 0
    ctx_row = bsz
    tm_all = 768
    assert t_ctx == tm_all - _row_split(t_lat, tm_all)

    cc = jnp.zeros((MOD_ROWS, d), F32).at[0:bsz].set(c).at[ctx_row].set(c_ctx)
    mods = _modulation(cc, mod_w, mod_b)
    rope = _rope_tables(t_lat, t_ctx)
    gla_consts = _gla_consts()
    eye = jnp.eye(SWA_WINDOW, dtype=BF16)
    gmask = jnp.asarray((np.arange(GMLP_WIDTH)[None, :] // GMLP_GDIM ==
                         np.arange(GMLP_GROUPS * GMLP_CHUNK)[:, None] // GMLP_CHUNK).astype(np.float32))
    row = lambda v: v.reshape(1, -1)

    assert t_ctx == STREAM_PIECE
    xs, xs_ctx = x, ctx
    for l in range(depth):
        last = l == depth - 1
        t_out = t_lat if last else t_all
        gla, lg3, gm, sk, sqt, svt, kt, lgt3 = _inproj(
            xs, xs_ctx, mods[l], row(n1_pre[l]), _inproj_weights(w_in[l], gla_wa2[l], gla_ba[l]), rope,
            t_lat=t_lat, t_all=t_all, tm=tm_all, ctx_row=ctx_row)
        a_out = _gla(gla, lg3, kt, lgt3, gla_consts, row(jnp.tile(gla_norm[l], GLA_HEADS)),
                     t_lat=t_lat, t_out=t_out)
        ws_s = gmlp_ws[l].reshape(GMLP_GROUPS * GMLP_CHUNK, GMLP_CHUNK).astype(BF16)
        bs_t = jnp.repeat(gmlp_bs[l].T, GMLP_GDIM, axis=1)
        b_out = _gmlp(gm, row(gmlp_ln_g[l]), row(gmlp_ln_b[l]), ws_s, bs_t, gmask, row(gmlp_out_g[l]),
                      t_out=t_out, tm=1024 if last else tm_all)
        sink_t = row(jnp.repeat(swa_sink[l], SWA_WINDOW))
        out_g_t = jnp.broadcast_to(swa_out_g[l][:, None], (SWA_Q, SWA_WINDOW))
        sb = next(n for n in SWA_STEP_BLOCKS if (t_out // SWA_WINDOW) % n == 0)
        c_out = _swa(sqt, sk, svt, sink_t, out_g_t, eye, t_lat=t_lat, t_out=t_out, sb=sb)
        xs = _post(a_out, b_out, c_out, xs, xs_ctx, mods[l], row(n1_post[l]), row(n2_pre[l]), row(n2_post[l]),
                   w_out[l].astype(BF16), ffn_w_gu[l].astype(BF16), ffn_w_down[l].astype(BF16),
                   t_lat=t_lat, t_out=t_out, tm=1024 if last else tm_all, ctx_row=ctx_row)
        xs_ctx = None
    return xs
```

```python
import functools

import numpy as np
import jax
import jax.numpy as jnp
from jax import lax
from jax.experimental import pallas as pl
from jax.experimental.pallas import tpu as pltpu

F32 = jnp.float32
BF16 = jnp.bfloat16

D_MODEL = 1024
GRID_W = 64
HEAD_DIM = 64
GLA_HEADS = 4
GLA_DK = 32
GLA_DV = 64
GLA_QK = GLA_HEADS * GLA_DK
GLA_V = GLA_HEADS * GLA_DV
GLA_GATE_RANK = 16
GLA_GATE_TAU = 16.0
GLA_CHUNK = 64
GMLP_GROUPS = 4
GMLP_GDIM = 64
GMLP_WIDTH = GMLP_GROUPS * GMLP_GDIM
GMLP_CHUNK = 128
SWA_HEADS = 8
SWA_KV_HEADS = 2
SWA_REP = SWA_HEADS // SWA_KV_HEADS
SWA_Q = SWA_HEADS * HEAD_DIM
SWA_KV = SWA_KV_HEADS * HEAD_DIM
SWA_WINDOW = 128
ROPE_AXIS_DIM = HEAD_DIM // 2
ROPE_THETA = 10000.0
MIX_WIDTH = GLA_V + GMLP_WIDTH + SWA_Q
D_FF = -(-8 * D_MODEL // (3 * 256)) * 256
N_MOD = 6
MOD_ROWS = 24

COL_GLA = 2 * GLA_QK + 2 * GLA_V
COL_GM = 2 * GMLP_WIDTH
COL_SK = SWA_KV
COL_CODE = 128
OFF_GM = COL_GLA
OFF_SK = OFF_GM + COL_GM
OFF_CODE = OFF_SK + COL_SK
ROW_SQ = 0
ROW_SV = ROW_SQ + SWA_Q
ROW_GK = ROW_SV + SWA_KV

LOG2E = 1.4426950408889634
SWA_Q_SCALE = HEAD_DIM ** -0.5 * LOG2E
SWA_KEY_TILE = 32
SWA_PART_HEADS = 2
SWA_STEP_BLOCKS = (4, 3)
SWA_DEN_ROWS = 16

VMEM_LIMIT = 58 * 1024 * 1024
GLA_LEVELS = (1, 2, 4, 8, 16, 32)


def _cparams(*sem):
    return pltpu.CompilerParams(dimension_semantics=sem, vmem_limit_bytes=VMEM_LIMIT)


def _const_spec(shape):
    nd = len(shape)
    return pl.BlockSpec(shape, lambda *_: (0,) * nd)


def _rms(x, g, eps=1e-6):
    return x * lax.rsqrt(jnp.mean(x * x, axis=-1, keepdims=True) + eps) * g


def _silu(x):
    return x * jax.nn.sigmoid(x)


def _row_split(t_lat, tm):
    return t_lat % tm if t_lat % tm else tm


def _mod_vectors(mod_ref, b, j, idx, *, t_lat, tm, ctx_row):
    lo = idx * D_MODEL
    lat = mod_ref[pl.ds(b, 1), lo:lo + D_MODEL]
    ctx = mod_ref[ctx_row:ctx_row + 1, lo:lo + D_MODEL]
    return lat, jnp.where(j * tm + _row_split(t_lat, tm) >= t_lat, ctx, lat)


STREAM_PIECE = 256


def _stream_specs(x, ctx, tm, tile=lambda b, j: (b, j)):
    rows, d = x.shape[1:]
    per_tile = tm // STREAM_PIECE
    last = rows // STREAM_PIECE - 1

    def piece(k):
        def index(b, j):
            tb, tj = tile(b, j)
            return tb, jnp.minimum(tj * per_tile + k, last), 0
        return pl.BlockSpec((1, STREAM_PIECE, d), index)

    operands, specs = [x] * per_tile, [piece(k) for k in range(per_tile)]
    if ctx is not None:
        assert ctx.shape[1] == STREAM_PIECE
        operands.append(ctx)
        specs.append(pl.BlockSpec((1, STREAM_PIECE, d), lambda b, j: (tile(b, j)[0], 0, 0)))
    return operands, specs


def _stream_tile(refs, j, *, tm, t_lat, ctx_separate):
    pieces = [r[0] for r in refs[:tm // STREAM_PIECE]]
    if ctx_separate:
        pieces[-1] = jnp.where((j + 1) * tm - STREAM_PIECE >= t_lat, refs[-1][0], pieces[-1])
    return jnp.concatenate(pieces, axis=0)


def _by_rows(fn, split, *arrays):
    tm = arrays[0].shape[0]
    if split == tm:
        return fn(0, *arrays)
    return jnp.concatenate([fn(0, *[a[0:split] for a in arrays]), fn(1, *[a[split:tm] for a in arrays])], axis=0)


def _mod_kernel(c_ref, w_ref, b_ref, o_ref):
    s = _silu(c_ref[...]).astype(BF16)
    o_ref[0] = jnp.dot(s, w_ref[0].astype(BF16), preferred_element_type=F32) + b_ref[0]


def _modulation(cc, mod_w, mod_b):
    depth, d, n = mod_w.shape
    tn = n // 4
    return pl.pallas_call(
        _mod_kernel,
        grid=(depth, n // tn),
        in_specs=[pl.BlockSpec((MOD_ROWS, d), lambda l, j: (0, 0)),
                  pl.BlockSpec((1, d, tn), lambda l, j: (l, 0, j)),
                  pl.BlockSpec((1, 1, tn), lambda l, j: (l, 0, j))],
        out_specs=pl.BlockSpec((1, MOD_ROWS, tn), lambda l, j: (l, 0, j)),
        out_shape=jax.ShapeDtypeStruct((depth, MOD_ROWS, n), F32),
        compiler_params=_cparams("parallel", "parallel"),
        name="modulation",
    )(cc, mod_w, mod_b.reshape(depth, 1, n))


def _rope(z, cos, sin):
    lane = lax.broadcasted_iota(jnp.int32, z.shape, 1)
    first = (lane % 32) < 16
    rot = jnp.where(first, pltpu.roll(z, 128 - 16, 1), pltpu.roll(z, 16, 1))
    return z * cos + rot * sin


def _split3(x):
    p1 = x.astype(BF16)
    r1 = x - p1.astype(F32)
    p2 = r1.astype(BF16)
    p3 = (r1 - p2.astype(F32)).astype(BF16)
    return p1, p2, p3


def _log_gate(z):
    return (jnp.minimum(z, 0.0) - jnp.log(1.0 + jnp.exp(-jnp.abs(z)))) * (1.0 / GLA_GATE_TAU)


def _rope_t(z, cos_t, sin_t):
    parts = []
    for r0 in range(0, z.shape[0], 32):
        parts += [z[r0 + 16:r0 + 32], z[r0:r0 + 16]]
    rot = jnp.concatenate(parts, axis=0)
    reps = z.shape[0] // cos_t.shape[0]
    return z * jnp.concatenate([cos_t] * reps, axis=0) + rot * jnp.concatenate([sin_t] * reps, axis=0)


def _inproj_kernel(*refs, t_lat, tm, ctx_row, n_stream, ctx_separate):
    (mod_ref, g_ref, w_ref, wt_ref, wa_ref, ba_ref, cos_ref, sin_ref, cost_ref, sint_ref,
     gla_ref, lg_ref, gm_ref, sk_ref, sqt_ref, svt_ref, kt_ref, lgt_ref, h_buf, ht_buf) = refs[2 * n_stream:]
    b = pl.program_id(0)
    j = pl.program_id(1)
    nt = pl.num_programs(1)
    step = b * nt + j
    slot = lax.rem(step, 2)

    def prologue(stream_refs, tb, tj, dst):
        x = _stream_tile(stream_refs, tj, tm=tm, t_lat=t_lat, ctx_separate=ctx_separate)
        mod = functools.partial(_mod_vectors, mod_ref, tb, tj, t_lat=t_lat, tm=tm, ctx_row=ctx_row)
        shift, scale, gain = mod(0), mod(1), g_ref[...]
        gains = [gain * (1.0 + s) for s in scale]
        hf = _by_rows(lambda part, x: _rms(x, gains[part]) + shift[part], _row_split(t_lat, tm), x)
        h_buf[dst] = hf.astype(BF16)
        ht_buf[dst] = hf.T.astype(BF16)

    @pl.when(step == 0)
    def _first():
        prologue(refs[n_stream:2 * n_stream], 0, 0, 0)

    def project(cur):
        def proj(off, n):
            return jnp.dot(h_buf[cur], w_ref[:, off:off + n], preferred_element_type=F32)

        def proj_t(off, n):
            return jnp.dot(wt_ref[off:off + n, :], ht_buf[cur], preferred_element_type=F32)

        code = proj(OFF_CODE, COL_CODE).astype(BF16)
        z = proj(0, COL_GLA)
        gla_ref[0, :, 0:GLA_QK] = (z[:, 0:GLA_QK] * (GLA_DK ** -0.5)).astype(BF16)
        gla_ref[0, :, GLA_QK:COL_GLA] = z[:, GLA_QK:COL_GLA].astype(BF16)

        n2 = 2 * GLA_QK
        lg = _log_gate(jnp.dot(code, wa_ref[...], preferred_element_type=F32) + ba_ref[...])
        for i, p in enumerate(_split3(lg)):
            lg_ref[0, :, i * n2:(i + 1) * n2] = p
        for i, p in enumerate(_split3(lg.T)):
            lgt_ref[0, i * n2:(i + 1) * n2, :] = p

        gm_ref[0] = proj(OFF_GM, COL_GM).astype(BF16)
        nb, nj = _next_tile(b, j, pl.num_programs(0), nt)
        prologue(refs[:n_stream], nb, nj, 1 - cur)
        kt_ref[0] = proj_t(ROW_GK, GLA_QK).astype(BF16)

        sk_ref[0] = _rope(proj(OFF_SK, COL_SK), cos_ref[...], sin_ref[...]).astype(BF16)
        sqt_ref[0] = (_rope_t(proj_t(ROW_SQ, SWA_Q), cost_ref[...], sint_ref[...]) * SWA_Q_SCALE).astype(BF16)
        svt_ref[0] = proj_t(ROW_SV, SWA_KV).astype(BF16)

    for cur in range(2):
        pl.when(slot == cur)(functools.partial(project, cur))


def _next_tile(b, j, nb, nt):
    wrap = j + 1 == nt
    last = jnp.logical_and(wrap, b + 1 == nb)
    return jnp.where(wrap & jnp.logical_not(last), b + 1, b), jnp.where(last, j, jnp.where(wrap, 0, j + 1))


def _inproj(x, ctx, mod, n_pre, weights, rope, *, t_lat, t_all, tm, ctx_row):
    bsz, _, d = x.shape
    nt = t_all // tm
    nxt, nxt_specs = _stream_specs(x, ctx, tm, functools.partial(_next_tile, nb=bsz, nt=nt))
    first, first_specs = _stream_specs(x, ctx, tm, lambda b, j: (0, 0))
    stream, stream_specs = nxt + first, nxt_specs + first_specs
    tok = lambda n: pl.BlockSpec((1, tm, n), lambda b, j: (b, j, 0))
    tok_t = lambda n: pl.BlockSpec((1, n, tm), lambda b, j: (b, 0, j))
    n2 = 2 * GLA_QK
    cols = (COL_GLA, 3 * n2, COL_GM, COL_SK)
    rows = (SWA_Q, SWA_KV, GLA_QK, 3 * n2)
    out_shape = [jax.ShapeDtypeStruct((bsz, t_all, n), BF16) for n in cols]
    out_shape += [jax.ShapeDtypeStruct((bsz, n, t_all), BF16) for n in rows]
    consts = (mod, n_pre) + tuple(weights)
    cos, sin, cos_t, sin_t = rope
    return pl.pallas_call(
        functools.partial(_inproj_kernel, t_lat=t_lat, tm=tm, ctx_row=ctx_row, n_stream=len(nxt),
                          ctx_separate=ctx is not None),
        grid=(bsz, nt),
        in_specs=stream_specs + [_const_spec(a.shape) for a in consts] +
                 [pl.BlockSpec((tm, 128), lambda b, j: (j, 0)), pl.BlockSpec((tm, 128), lambda b, j: (j, 0)),
                  pl.BlockSpec((128, tm), lambda b, j: (0, j)), pl.BlockSpec((128, tm), lambda b, j: (0, j))],
        out_specs=[tok(n) for n in cols] + [tok_t(n) for n in rows],
        out_shape=out_shape,
        scratch_shapes=[pltpu.VMEM((2, tm, d), BF16), pltpu.VMEM((2, d, tm), BF16)],
        compiler_params=_cparams("arbitrary", "arbitrary"),
        name="inproj",
    )(*stream, *consts, cos, sin, cos_t, sin_t)


GLA_FAST_CHUNK = 128
GLA_FAST_UNROLL = 2
GLA_SAFE_LOG_DECAY = -40.0


def _gla_fast_tables():
    c = GLA_FAST_CHUNK
    t = np.arange(c)[:, None]
    r = np.arange(c)[None, :]
    incl = [(r <= t), (r >= t)]
    a3 = np.stack([np.tile(m, (1, 3)) for m in incl]).astype(np.float32)
    a3t_neg = -np.stack([np.tile(m.T, (3, 1)) for m in incl]).astype(np.float32)
    mask_w = np.stack([np.tile(m, (1, GLA_HEADS)) for m in incl]).astype(np.float32)
    half = (np.arange(128)[None, :] // GLA_DV) == ((np.arange(GLA_QK)[:, None] // GLA_DK) % 2)
    return a3, a3t_neg, mask_w, half.astype(np.float32)


def _gla_fast_chunks(jobs, a3, a3t_neg, mask_w, bd_c, states):
    c = GLA_FAST_CHUNK
    dk, hd = GLA_DK, GLA_QK
    cums = [(jnp.dot(a3[d], g3, preferred_element_type=F32),
             jnp.dot(gt3, a3t_neg[d], preferred_element_type=F32))
            for d, _, _, _, g3, gt3 in jobs]

    zr = lambda n: jnp.zeros((n, c), BF16)
    qes, dcols, khs, atts = [], [], [], []
    for (d, q, kt, _, _, _), (cum, ncum_t) in zip(jobs, cums):
        qe = (q.astype(F32) * jnp.exp(cum)).astype(BF16)
        ke_t = kt.astype(F32) * jnp.exp(ncum_t)
        edge = 0 if d else c - 1
        dcol = jnp.exp(-ncum_t[:, edge:edge + 1])
        khs.append((ke_t * dcol).astype(BF16))
        ke_t = ke_t.astype(BF16)
        cols = []
        for hh in range(GLA_HEADS):
            parts = ([zr(hh * dk)] if hh else []) + [ke_t[hh * dk:(hh + 1) * dk]]
            parts += [zr(hd - (hh + 1) * dk)] if hh < GLA_HEADS - 1 else []
            cols.append(jnp.concatenate(parts, axis=0))
        kbd = jnp.concatenate(cols, axis=1)
        atts.append(jnp.dot(qe, kbd, preferred_element_type=F32))
        qes.append(qe)
        dcols.append(dcol)

    kvs = [jnp.concatenate([jnp.dot(kh[0:hd // 2], v[:, 0:128], preferred_element_type=F32),
                            jnp.dot(kh[hd // 2:hd], v[:, 128:256], preferred_element_type=F32)], axis=0)
           for (_, _, _, v, _, _), kh in zip(jobs, khs)]

    lane = lax.broadcasted_iota(jnp.int32, (1, 128), 1)
    m_lo = (lane < GLA_DV).astype(BF16)
    m_hi = (lane >= GLA_DV).astype(BF16)
    z128 = jnp.zeros((c, 128), BF16)
    z64 = jnp.zeros((hd // 2, 128), BF16)
    states = list(states)
    outs = []
    for (d, _, _, v, _, _), qe, dcol, att, kv in zip(jobs, qes, dcols, atts, kvs):
        v_lo, v_hi = v[:, 0:128], v[:, 128:256]
        vbd = jnp.concatenate([jnp.concatenate([v_lo * m_lo, z128], axis=1),
                               jnp.concatenate([v_lo * m_hi, z128], axis=1),
                               jnp.concatenate([z128, v_hi * m_lo], axis=1),
                               jnp.concatenate([z128, v_hi * m_hi], axis=1)], axis=0)
        sb = states[d].astype(BF16)
        s_full = jnp.concatenate([jnp.concatenate([sb[0:hd // 2], z64], axis=1),
                                  jnp.concatenate([z64, sb[hd // 2:hd]], axis=1)], axis=0)
        lhs = jnp.concatenate([att.astype(BF16) * mask_w[d], qe], axis=1)
        outs.append(jnp.dot(lhs, jnp.concatenate([vbd, s_full], axis=0),
                            preferred_element_type=F32))
        states[d] = states[d] * dcol + kv * bd_c
    return outs, states


def _gla_tables():
    c = GLA_CHUNK
    t = np.arange(c)[:, None]
    r = np.arange(c)[None, :]
    sizes = [2 * b for b in GLA_LEVELS]
    same = lambda b: (t // b) == (r // b)
    a_f = [same(b) & (r <= t) for b in sizes] + [same(b) & (r > t) for b in sizes]
    a_b = [same(b) & (r >= t) for b in sizes] + [same(b) & (r < t) for b in sizes]
    m_f = [((t // b) % 2 == 1) & ((r // b) == (t // b) - 1) for b in GLA_LEVELS] + [t == r]
    m_b = [((t // b) % 2 == 0) & ((r // b) == (t // b) + 1) for b in GLA_LEVELS] + [t == r]
    amat = np.stack([np.tile(np.concatenate(a, 0), (1, 3)) for a in (a_f, a_b)]).astype(np.float32)
    mask = np.stack([np.stack([np.tile(m, (GLA_HEADS, 1)) for m in ms]) for ms in (m_f, m_b)])
    return amat, mask.astype(np.float32)


def _gla_chunk(q, k, v, g3, amat, masks, hmask_q, bd_v, bd_s, s_prev, *, backward):
    c = GLA_CHUNK
    nl = len(GLA_LEVELS)
    g = g3[0:c].astype(F32) + g3[c:2 * c].astype(F32) + g3[2 * c:3 * c].astype(F32)
    ps = jnp.dot(amat, g3, preferred_element_type=F32)
    blk = lambda i: ps[i * c:(i + 1) * c]
    q_exp = [g] + [blk(i) for i in range(nl - 1)]
    k_exp = [None] + [blk(nl + i) for i in range(nl - 1)]
    q_full = blk(nl - 1)
    k_full = blk(2 * nl - 1)

    att = None
    for lvl in range(nl + 1):
        if lvl < nl:
            qe = q * jnp.exp(q_exp[lvl])
            ke = k if k_exp[lvl] is None else k * jnp.exp(k_exp[lvl])
        else:
            qe, ke = q, k
        qbd = jnp.concatenate([qe.astype(BF16)] * GLA_HEADS, axis=0) * hmask_q
        p = lax.dot_general(qbd, ke.astype(BF16), (((1,), (1,)), ((), ())), preferred_element_type=F32)
        p = p * masks[lvl]
        att = p if att is None else att + p

    r = jnp.dot(att.astype(BF16), v, preferred_element_type=F32) * bd_v
    o = r[0:c]
    for hh in range(1, GLA_HEADS):
        o = o + r[hh * c:(hh + 1) * c]
    o = o + jnp.dot((q * jnp.exp(q_full)).astype(BF16), s_prev.astype(BF16), preferred_element_type=F32)

    tot = q_full[0:1] if backward else q_full[c - 1:c]
    dcol = jnp.transpose(jnp.broadcast_to(jnp.exp(tot), (GLA_QK, GLA_QK)))
    dcol = jnp.concatenate([dcol, dcol], axis=1)
    kv = lax.dot_general((k * jnp.exp(k_full)).astype(BF16), v, (((0,), (0,)), ((), ())),
                         preferred_element_type=F32)
    return o, s_prev * dcol + kv * bd_s


def _gla_kernel(gla_ref, lg_ref, kt_ref, lgt_ref, a3_ref, a3t_ref, mw_ref, bdc_ref, amat_ref, mask_ref,
                hq_ref, bdv_ref, bds_ref, ind_ref, norm_ref, out_ref, of_ref, ob_ref, *, t_lat, t_all, t_out):
    n2 = 2 * GLA_QK
    o_v = 2 * GLA_QK

    def pieces(rows, d):
        return jnp.concatenate([lg_ref[0, rows, i * n2 + d * GLA_QK:i * n2 + (d + 1) * GLA_QK]
                                for i in range(3)], axis=0)

    cf_ = GLA_FAST_CHUNK
    worst = None
    for ci in range(t_all // cf_):
        tot = jnp.sum(lg_ref[0, ci * cf_:(ci + 1) * cf_, 0:n2].astype(F32), axis=0, keepdims=True)
        worst = tot if worst is None else jnp.minimum(worst, tot)
    safe = jnp.min(worst) >= GLA_SAFE_LOG_DECAY

    @pl.when(safe)
    def _fast():
        c = GLA_FAST_CHUNK
        n_lat = t_lat // c
        n_all = t_all // c
        per_step = GLA_FAST_UNROLL
        assert n_all % per_step == 0

        def job(ci, d):
            rows = pl.ds(pl.multiple_of(ci * c, c), c)
            gt3 = jnp.concatenate([lgt_ref[0, i * n2 + d * GLA_QK:i * n2 + (d + 1) * GLA_QK, rows]
                                   for i in range(3)], axis=1)
            return rows, (d, gla_ref[0, rows, 0:GLA_QK], kt_ref[0, :, rows], gla_ref[0, rows, o_v:o_v + GLA_V],
                          pieces(rows, d), gt3)

        def step(i, carry):
            chunks = [(lax.rem(i * per_step + u + n_lat, n_all), 0) for u in range(per_step)]
            chunks += [(n_all - 1 - (i * per_step + u), 1) for u in range(per_step)]
            rows, jobs = zip(*[job(ci, d) for ci, d in chunks])
            outs, states = _gla_fast_chunks(jobs, a3_ref, a3t_ref, mw_ref, bdc_ref[...], carry)
            for (_, d), r, o in zip(chunks, rows, outs):
                (ob_ref if d else of_ref)[r, :] = o
            return tuple(states)

        zero = jnp.zeros((GLA_QK, 128), F32)
        lax.fori_loop(0, n_all // per_step, step, (zero, zero))

    @pl.when(jnp.logical_not(safe))
    def _robust():
        c = GLA_CHUNK
        n_lat = t_lat // c
        n_all = t_all // c
        hq = hq_ref[...]
        bdv = bdv_ref[...]
        bds = bds_ref[...]

        def one(ci, d, s):
            start = pl.multiple_of(ci * c, c)
            rows = pl.ds(start, c)
            q = gla_ref[0, rows, 0:GLA_QK].astype(F32)
            k = gla_ref[0, rows, GLA_QK:2 * GLA_QK].astype(F32)
            v = gla_ref[0, rows, o_v:o_v + GLA_V]
            o, s = _gla_chunk(q, k, v, pieces(rows, d), amat_ref[d],
                              [mask_ref[d, l] for l in range(len(GLA_LEVELS) + 1)], hq, bdv, bds, s,
                              backward=bool(d))
            (ob_ref if d else of_ref)[rows, :] = o
            return s

        def step(i, carry):
            s_f, s_b = carry
            return one(lax.rem(i + n_lat, n_all), 0, s_f), one(n_all - 1 - i, 1, s_b)

        zero = jnp.zeros((GLA_QK, GLA_V), F32)
        lax.fori_loop(0, n_all, step, (zero, zero))

    tr = 256
    ind = ind_ref[...]
    for r0 in range(0, t_out, tr):
        o = of_ref[r0:r0 + tr, :] + ob_ref[r0:r0 + tr, :]
        sq = o * o
        hi = sq.astype(BF16)
        lo = (sq - hi.astype(F32)).astype(BF16)
        ms = jnp.dot(hi, ind, preferred_element_type=F32) + jnp.dot(lo, ind, preferred_element_type=F32)
        gate = gla_ref[0, r0:r0 + tr, o_v + GLA_V:COL_GLA].astype(F32)
        y = o * lax.rsqrt(ms + 1e-6) * norm_ref[...] * _silu(gate)
        out_ref[0, r0:r0 + tr, :] = y.astype(BF16)


def _gla(gla, lg3, kt, lgt3, consts, norm_t, *, t_lat, t_out):
    bsz, t_all, _ = gla.shape
    n2 = 2 * GLA_QK
    return pl.pallas_call(
        functools.partial(_gla_kernel, t_lat=t_lat, t_all=t_all, t_out=t_out),
        grid=(bsz,),
        in_specs=[pl.BlockSpec((1, t_all, COL_GLA), lambda b: (b, 0, 0)),
                  pl.BlockSpec((1, t_all, 3 * n2), lambda b: (b, 0, 0)),
                  pl.BlockSpec((1, GLA_QK, t_all), lambda b: (b, 0, 0)),
                  pl.BlockSpec((1, 3 * n2, t_all), lambda b: (b, 0, 0))] +
                 [_const_spec(a.shape) for a in consts] + [_const_spec(norm_t.shape)],
        out_specs=pl.BlockSpec((1, t_out, GLA_V), lambda b: (b, 0, 0)),
        out_shape=jax.ShapeDtypeStruct((bsz, t_out, GLA_V), BF16),
        scratch_shapes=[pltpu.VMEM((t_all, GLA_V), F32), pltpu.VMEM((t_all, GLA_V), F32)],
        compiler_params=_cparams("parallel"),
        name="gla",
    )(gla, lg3, kt, lgt3, *consts, norm_t)


def _gmlp_tile(z_ref, params, dst):
    lng_ref, lnb_ref, ws_ref, bs_ref, gmask_ref, og_ref = params
    z = z_ref[0].astype(F32)
    zf = 0.5 * z * (1.0 + lax.erf(z * (2.0 ** -0.5)))
    u = zf[:, 0:GMLP_WIDTH]
    v = zf[:, GMLP_WIDTH:]
    mu = jnp.mean(v, axis=-1, keepdims=True)
    vc = v - mu
    v = vc * lax.rsqrt(jnp.mean(vc * vc, axis=-1, keepdims=True) + 1e-5) * lng_ref[...] + lnb_ref[...]
    vb = v.astype(BF16)
    ws = ws_ref[...]
    p = GMLP_CHUNK
    for ci in range(z.shape[0] // p):
        r = jnp.dot(ws, vb[ci * p:(ci + 1) * p], preferred_element_type=F32) * gmask_ref[...]
        mixed = bs_ref[...]
        for g in range(GMLP_GROUPS):
            mixed = mixed + r[g * p:(g + 1) * p]
        y = u[ci * p:(ci + 1) * p] * mixed
        dst[ci * p:(ci + 1) * p, :] = _rms(y, og_ref[...]).astype(BF16)


def _swa_attend(qt_ref, k_ref, vt_ref, sink_ref, og_ref, eye_ref, out_ref, s_ref, p_ref, u, key_rows, biases):
    w = SWA_WINDOW
    d = HEAD_DIM
    kt = SWA_KEY_TILE
    ph = SWA_PART_HEADS
    pw = ph * w
    parts = range(SWA_HEADS // ph)
    group = lambda part: part * ph // SWA_REP
    s_ref = s_ref.at[u]
    p_ref = p_ref.at[u]
    cols = slice(u * w, (u + 1) * w)
    keys = jnp.concatenate([k_ref[0, pl.ds(start, size), :] for start, size in key_rows], axis=0)
    nk = keys.shape[0]

    zero = jnp.zeros((d, pw), BF16)
    sinks, tops = [], []
    for part in parts:
        q = jnp.concatenate([qt_ref[0, h * d:(h + 1) * d, cols] for h in range(part * ph, (part + 1) * ph)], axis=1)
        qbd = jnp.concatenate([q if g == group(part) else zero for g in range(SWA_KV_HEADS)], axis=0)
        best, off = None, 0
        for (_, size), bias in zip(key_rows, biases):
            blk = jnp.dot(keys[off:off + size], qbd, preferred_element_type=F32)
            if bias is not None:
                blk = blk + jnp.concatenate([bias] * ph, axis=1)
            s_ref[part, off:off + size, :] = blk
            m8 = jnp.max(blk.reshape(size // 8, 8, pw), axis=0)
            best = m8 if best is None else jnp.maximum(best, m8)
            off += size
        sink = sink_ref[:, part * pw:(part + 1) * pw] * LOG2E
        sinks.append(sink)
        tops.append(jnp.maximum(jnp.max(best, axis=0, keepdims=True), sink))
    yield

    for part in parts:
        for r0 in range(0, nk, kt):
            p_ref[part, r0:r0 + kt, :] = jnp.exp2(s_ref[part, r0:r0 + kt, :] - tops[part]).astype(BF16)
    yield

    ones = jnp.ones((SWA_DEN_ROWS, nk), BF16)
    vts = [jnp.concatenate([vt_ref[0, g * d:(g + 1) * d, pl.ds(start, size)] for start, size in key_rows] , axis=1)
           for g in range(SWA_KV_HEADS)]
    heads = []
    for part in parts:
        o_ext = jnp.dot(jnp.concatenate([vts[group(part)], ones], axis=0), p_ref[part, 0:nk, :],
                        preferred_element_type=F32)
        scaled = o_ext[0:d] / (o_ext[d:d + 1] + jnp.exp2(sinks[part] - tops[part]))
        heads += [scaled[:, r * w:(r + 1) * w] for r in range(ph)]
    o_t = jnp.concatenate(heads, axis=0)
    yield

    y_t = o_t * lax.rsqrt(jnp.mean(o_t * o_t, axis=0, keepdims=True) + 1e-6) * og_ref[...]
    out = lax.dot_general(eye_ref[...], y_t.astype(BF16), (((1,), (1,)), ((), ())), preferred_element_type=F32)
    out_ref[0, cols, :] = out.astype(BF16)


def _run_stages(blocks):
    for t in range(len(blocks) + 2):
        for u in (t, t - 1, t - 2, t - 1):
            if 0 <= u < len(blocks):
                next(blocks[u], None)


def _swa_kernel(qt_ref, k_ref, vt_ref, sink_ref, og_ref, eye_ref, out_ref, s_ref, p_ref, *, t_lat, t_all, t_out, sb):
    w = SWA_WINDOW
    step = pl.program_id(1)
    n_lat = t_lat // w
    ctx_rows = (t_lat, t_all - t_lat)
    attend = functools.partial(_swa_attend, qt_ref, k_ref, vt_ref, sink_ref, og_ref, eye_ref, out_ref, s_ref, p_ref)

    def run(latent):
        blk = lambda i: (pl.multiple_of(i * w, w), w)
        sk = lax.broadcasted_iota(jnp.int32, (w, w), 0)
        tq = lax.broadcasted_iota(jnp.int32, (w, w), 1)
        neg = jnp.full((w, w), -jnp.inf, F32)
        zero = jnp.zeros((w, w), F32)
        blocks = []
        for u, is_latent in enumerate(latent):
            if not is_latent:
                blocks.append(attend(u, [ctx_rows], [None]))
                continue
            n = step * sb + u
            b_prev = jnp.where((sk >= tq) & (n >= 1), zero, neg)
            b_next = jnp.where((sk <= tq) & (n < n_lat - 1), zero, neg)
            blocks.append(attend(u, [ctx_rows, blk(jnp.maximum(n - 1, 0)), blk(n), blk(jnp.minimum(n + 1, n_lat - 1))],
                                 [None, b_prev, None, b_next]))
        _run_stages(blocks)

    n_steps = t_out // (sb * w)
    makeup = [tuple(s * sb + u < n_lat for u in range(sb)) for s in range(n_steps)]
    for kind in sorted(set(makeup), reverse=True):
        steps = [s for s in range(n_steps) if makeup[s] == kind]
        assert steps == list(range(steps[0], steps[-1] + 1))
        pl.when((step >= steps[0]) & (step <= steps[-1]))(functools.partial(run, kind))


def _swa(sqt, sk, svt, sink_t, out_g, eye, *, t_lat, t_out, sb):
    bsz, t_all, _ = sk.shape
    w = SWA_WINDOW
    n_keys = t_all - t_lat + 3 * w
    n_parts = SWA_HEADS // SWA_PART_HEADS
    assert t_out % (sb * w) == 0
    return pl.pallas_call(
        functools.partial(_swa_kernel, t_lat=t_lat, t_all=t_all, t_out=t_out, sb=sb),
        grid=(bsz, t_out // (sb * w)),
        in_specs=[pl.BlockSpec((1, SWA_Q, sb * w), lambda b, n: (b, 0, n)),
                  pl.BlockSpec((1, t_all, SWA_KV), lambda b, n: (b, 0, 0)),
                  pl.BlockSpec((1, SWA_KV, t_all), lambda b, n: (b, 0, 0)),
                  _const_spec(sink_t.shape), _const_spec(out_g.shape), _const_spec(eye.shape)],
        out_specs=pl.BlockSpec((1, sb * w, SWA_Q), lambda b, n: (b, n, 0)),
        out_shape=jax.ShapeDtypeStruct((bsz, t_out, SWA_Q), BF16),
        scratch_shapes=[pltpu.VMEM((sb, n_parts, n_keys, SWA_PART_HEADS * w), F32),
                        pltpu.VMEM((sb, n_parts, n_keys, SWA_PART_HEADS * w), BF16)],
        compiler_params=_cparams("parallel", "parallel"),
        name="swa",
    )(sqt, sk, svt, sink_t, out_g, eye)


FFN_CHUNKS = ((0, 1536), (1536, 1280))


def _post_kernel(*refs, t_lat, tm, ctx_row, n_stream, ctx_separate):
    (a_ref, c_ref, gm_next_ref, gm_first_ref, mod_ref, n1_ref, n2a_ref, n2b_ref, wo_ref, wgu_ref, wd_ref,
     *gmlp_params, out_ref, b_buf) = refs[n_stream:]
    b = pl.program_id(0)
    j = pl.program_id(1)
    step = b * pl.num_programs(1) + j

    @pl.when(step == 0)
    def _first():
        _gmlp_tile(gm_first_ref, gmlp_params, b_buf.at[0])

    def tile(cur):
        x = _stream_tile(refs[:n_stream], j, tm=tm, t_lat=t_lat, ctx_separate=ctx_separate)
        mod = functools.partial(_mod_vectors, mod_ref, b, j, t_lat=t_lat, tm=tm, ctx_row=ctx_row)
        split = _row_split(t_lat, tm)
        gate1, shift2, scale2, gate2 = mod(2), mod(3), mod(4), mod(5)
        g1 = [n1_ref[...] * g for g in gate1]
        g2a = [n2a_ref[...] * (1.0 + s) for s in scale2]
        g2b = [n2b_ref[...] * g for g in gate2]

        cat = jnp.concatenate([a_ref[0], b_buf[cur], c_ref[0]], axis=-1)
        y = jnp.dot(cat, wo_ref[...], preferred_element_type=F32)
        x1 = _by_rows(lambda p, x, y: x + _rms(y, g1[p]), split, x, y)
        h = _by_rows(lambda p, x: (_rms(x, g2a[p]) + shift2[p]).astype(BF16), split, x1)
        f = None
        for i, (off, n) in enumerate(FFN_CHUNKS):
            gt = jnp.dot(h, wgu_ref[:, off:off + n], preferred_element_type=F32)
            up = jnp.dot(h, wgu_ref[:, D_FF + off:D_FF + off + n], preferred_element_type=F32)
            if i == 0:
                _gmlp_tile(gm_next_ref, gmlp_params, b_buf.at[1 - cur])
            act = (_silu(gt) * up).astype(BF16)
            part = jnp.dot(act, wd_ref[off:off + n, :], preferred_element_type=F32)
            f = part if f is None else f + part
        out_ref[0] = _by_rows(lambda p, x, f: x + _rms(f, g2b[p]), split, x1, f)

    for cur in range(2):
        pl.when(lax.rem(step, 2) == cur)(functools.partial(tile, cur))


def _post(a, gm, c, x, ctx, mod, n1_post, n2_pre, n2_post, wo, wgu, wd, gmlp_params, *, t_lat, t_out, tm, ctx_row):
    bsz = x.shape[0]
    nt = t_out // tm
    stream, stream_specs = _stream_specs(x, ctx, tm)
    tok = lambda n: pl.BlockSpec((1, tm, n), lambda b, j: (b, j, 0))
    once = lambda arr: pl.BlockSpec(arr.shape, lambda b, j: (0,) * arr.ndim, pipeline_mode=pl.Buffered(1))
    gm_next = pl.BlockSpec((1, tm, COL_GM), lambda b, j: (*_next_tile(b, j, bsz, nt), 0))
    gm_first = pl.BlockSpec((1, tm, COL_GM), lambda b, j: (0, 0, 0), pipeline_mode=pl.Buffered(1))
    return pl.pallas_call(
        functools.partial(_post_kernel, t_lat=t_lat, tm=tm, ctx_row=ctx_row, n_stream=len(stream),
                          ctx_separate=ctx is not None),
        grid=(bsz, nt),
        in_specs=stream_specs + [tok(GLA_V), tok(SWA_Q), gm_next, gm_first, _const_spec(mod.shape),
                                 _const_spec(n1_post.shape), _const_spec(n2_pre.shape), _const_spec(n2_post.shape),
                                 once(wo), once(wgu), once(wd)] + [_const_spec(p.shape) for p in gmlp_params],
        out_specs=tok(D_MODEL),
        out_shape=jax.ShapeDtypeStruct((bsz, t_out, D_MODEL), F32),
        scratch_shapes=[pltpu.VMEM((2, tm, GMLP_WIDTH), BF16)],
        compiler_params=_cparams("arbitrary", "arbitrary"),
        name="post",
    )(*stream, a, c, gm, gm, mod, n1_post, n2_pre, n2_post, wo, wgu, wd, *gmlp_params)


def _rope_tables(t_lat, t_ctx):
    rows = t_lat // GRID_W
    row = jnp.repeat(jnp.arange(rows), GRID_W).astype(F32)
    col = jnp.tile(jnp.arange(GRID_W), rows).astype(F32)
    inv_freq = jnp.power(ROPE_THETA, -jnp.arange(0, ROPE_AXIS_DIM, 2, dtype=F32) / ROPE_AXIS_DIM)
    ang_row = row[:, None] * inv_freq[None, :]
    ang_col = col[:, None] * inv_freq[None, :]
    ang = jnp.concatenate([ang_row, ang_row, ang_col, ang_col], axis=-1)
    sign = jnp.tile(jnp.concatenate([-jnp.ones((16,), F32), jnp.ones((16,), F32)]), 2)
    cos = jnp.tile(jnp.concatenate([jnp.cos(ang), jnp.ones((t_ctx, HEAD_DIM), F32)], axis=0), (1, 2))
    sin = jnp.tile(jnp.concatenate([jnp.sin(ang) * sign, jnp.zeros((t_ctx, HEAD_DIM), F32)], axis=0), (1, 2))
    return cos, sin, cos.T, sin.T


def _inproj_weights(w, wa2, ba):
    r = GLA_GATE_RANK
    o_code = 2 * GLA_QK + 2 * GLA_V
    o_gm = o_code + 2 * r
    o_sq = o_gm + 2 * GMLP_WIDTH
    o_sk = o_sq + SWA_Q
    o_sv = o_sk + SWA_KV
    code = jnp.concatenate([w[:, o_code:o_gm], jnp.zeros((w.shape[0], COL_CODE - 2 * r), w.dtype)], axis=1)
    w_tok = jnp.concatenate([w[:, 0:o_code], w[:, o_gm:o_sq], w[:, o_sk:o_sv], code], axis=1)
    w_t = jnp.concatenate([w[:, o_sq:o_sk], w[:, o_sv:], w[:, GLA_QK:2 * GLA_QK]], axis=1).T
    wa = jnp.zeros((COL_CODE, 2 * GLA_QK), F32)
    wa = wa.at[0:r, 0:GLA_QK].set(wa2[0]).at[r:2 * r, GLA_QK:].set(wa2[1])
    return w_tok.astype(BF16), w_t.astype(BF16), wa.astype(BF16), ba.reshape(1, 2 * GLA_QK)


def _gla_consts():
    a3, a3t_neg, mask_w, bd_c = _gla_fast_tables()
    amat, mask = _gla_tables()
    lane_head = np.arange(GLA_QK)[None, :] // GLA_DK
    row_head = np.arange(GLA_HEADS * GLA_CHUNK)[:, None] // GLA_CHUNK
    hq = (lane_head == row_head).astype(np.float32)
    vlane_head = np.arange(GLA_V)[None, :] // GLA_DV
    bdv = (vlane_head == row_head).astype(np.float32)
    bds = (vlane_head == (np.arange(GLA_QK)[:, None] // GLA_DK)).astype(np.float32)
    ind = (vlane_head == vlane_head.T).astype(np.float32) / GLA_DV
    return (jnp.asarray(a3, BF16), jnp.asarray(a3t_neg, BF16), jnp.asarray(mask_w, BF16),
            jnp.asarray(bd_c, F32), jnp.asarray(amat, BF16), jnp.asarray(mask, F32), jnp.asarray(hq, BF16),
            jnp.asarray(bdv, F32), jnp.asarray(bds, F32), jnp.asarray(ind, BF16))


def kernel(x, c, ctx, c_ctx, mod_w, mod_b, n1_pre, n1_post, n2_pre, n2_post, w_in, w_out, gla_wa2, gla_ba,
           gla_norm, gmlp_ln_g, gmlp_ln_b, gmlp_ws, gmlp_bs, gmlp_out_g, swa_sink, swa_out_g, ffn_w_gu,
           ffn_w_down):
    bsz, t_lat, d = x.shape
    t_ctx = ctx.shape[1]
    t_all = t_lat + t_ctx
    depth = mod_w.shape[0]
    assert d == D_MODEL and bsz < MOD_ROWS
    assert t_lat % 1024 == 0 and t_ctx % 256 == 0 and t_all % 768 == 0
    ctx_row = bsz
    tm_all = 768
    assert t_ctx == tm_all - _row_split(t_lat, tm_all)

    cc = jnp.zeros((MOD_ROWS, d), F32).at[0:bsz].set(c).at[ctx_row].set(c_ctx)
    mods = _modulation(cc, mod_w, mod_b)
    rope = _rope_tables(t_lat, t_ctx)
    gla_consts = _gla_consts()
    eye = jnp.eye(SWA_WINDOW, dtype=BF16)
    gmask = jnp.asarray((np.arange(GMLP_WIDTH)[None, :] // GMLP_GDIM ==
                         np.arange(GMLP_GROUPS * GMLP_CHUNK)[:, None] // GMLP_CHUNK).astype(np.float32))
    row = lambda v: v.reshape(1, -1)

    assert t_ctx == STREAM_PIECE
    xs, xs_ctx = x, ctx
    for l in range(depth):
        last = l == depth - 1
        t_out = t_lat if last else t_all
        gla, lg3, gm, sk, sqt, svt, kt, lgt3 = _inproj(
            xs, xs_ctx, mods[l], row(n1_pre[l]), _inproj_weights(w_in[l], gla_wa2[l], gla_ba[l]), rope,
            t_lat=t_lat, t_all=t_all, tm=tm_all, ctx_row=ctx_row)
        a_out = _gla(gla, lg3, kt, lgt3, gla_consts, row(jnp.tile(gla_norm[l], GLA_HEADS)),
                     t_lat=t_lat, t_out=t_out)
        ws_s = gmlp_ws[l].reshape(GMLP_GROUPS * GMLP_CHUNK, GMLP_CHUNK).astype(BF16)
        bs_t = jnp.repeat(gmlp_bs[l].T, GMLP_GDIM, axis=1)
        gmlp_params = (row(gmlp_ln_g[l]), row(gmlp_ln_b[l]), ws_s, bs_t, gmask, row(gmlp_out_g[l]))
        sink_t = row(jnp.repeat(swa_sink[l], SWA_WINDOW))
        out_g_t = jnp.broadcast_to(swa_out_g[l][:, None], (SWA_Q, SWA_WINDOW))
        sb = next(n for n in SWA_STEP_BLOCKS if (t_out // SWA_WINDOW) % n == 0)
        c_out = _swa(sqt, sk, svt, sink_t, out_g_t, eye, t_lat=t_lat, t_out=t_out, sb=sb)
        xs = _post(a_out, gm, c_out, xs, xs_ctx, mods[l], row(n1_post[l]), row(n2_pre[l]), row(n2_post[l]),
                   w_out[l].astype(BF16), ffn_w_gu[l].astype(BF16), ffn_w_down[l].astype(BF16), gmlp_params,
                   t_lat=t_lat, t_out=t_out, tm=1024 if last else tm_all, ctx_row=ctx_row)
        xs_ctx = None
    return xs
```

```python
import functools

import numpy as np
import jax
import jax.numpy as jnp
from jax import lax
from jax.experimental import pallas as pl
from jax.experimental.pallas import tpu as pltpu

F32 = jnp.float32
BF16 = jnp.bfloat16

D_MODEL = 1024
GRID_W = 64
HEAD_DIM = 64
GLA_HEADS = 4
GLA_DK = 32
GLA_DV = 64
GLA_QK = GLA_HEADS * GLA_DK
GLA_V = GLA_HEADS * GLA_DV
GLA_GATE_RANK = 16
GLA_GATE_TAU = 16.0
GLA_CHUNK = 64
GMLP_GROUPS = 4
GMLP_GDIM = 64
GMLP_WIDTH = GMLP_GROUPS * GMLP_GDIM
GMLP_CHUNK = 128
SWA_HEADS = 8
SWA_KV_HEADS = 2
SWA_REP = SWA_HEADS // SWA_KV_HEADS
SWA_Q = SWA_HEADS * HEAD_DIM
SWA_KV = SWA_KV_HEADS * HEAD_DIM
SWA_WINDOW = 128
ROPE_AXIS_DIM = HEAD_DIM // 2
ROPE_THETA = 10000.0
MIX_WIDTH = GLA_V + GMLP_WIDTH + SWA_Q
D_FF = -(-8 * D_MODEL // (3 * 256)) * 256
N_MOD = 6
MOD_ROWS = 24

COL_GLA = 2 * GLA_QK + 2 * GLA_V
COL_GM = 2 * GMLP_WIDTH
COL_SK = SWA_KV
COL_CODE = 128
OFF_GM = COL_GLA
OFF_SK = OFF_GM + COL_GM
OFF_CODE = OFF_SK + COL_SK
ROW_SQ = 0
ROW_SV = ROW_SQ + SWA_Q
ROW_GK = ROW_SV + SWA_KV

LOG2E = 1.4426950408889634
SWA_Q_SCALE = HEAD_DIM ** -0.5 * LOG2E
SWA_KEY_TILE = 32
SWA_PART_HEADS = 2
SWA_STEP_BLOCKS = (4, 3)
SWA_DEN_ROWS = 16

VMEM_LIMIT = 58 * 1024 * 1024
GLA_LEVELS = (1, 2, 4, 8, 16, 32)


def _cparams(*sem):
    return pltpu.CompilerParams(dimension_semantics=sem, vmem_limit_bytes=VMEM_LIMIT)


def _const_spec(shape):
    nd = len(shape)
    return pl.BlockSpec(shape, lambda *_: (0,) * nd)


def _rms(x, g, eps=1e-6):
    return x * lax.rsqrt(jnp.mean(x * x, axis=-1, keepdims=True) + eps) * g


def _silu(x):
    return x * jax.nn.sigmoid(x)


def _row_split(t_lat, tm):
    return t_lat % tm if t_lat % tm else tm


def _mod_vectors(mod_ref, b, j, idx, *, t_lat, tm, ctx_row):
    lo = idx * D_MODEL
    lat = mod_ref[pl.ds(b, 1), lo:lo + D_MODEL]
    ctx = mod_ref[ctx_row:ctx_row + 1, lo:lo + D_MODEL]
    return lat, jnp.where(j * tm + _row_split(t_lat, tm) >= t_lat, ctx, lat)


STREAM_PIECE = 256


def _stream_specs(x, ctx, tm, tile=lambda b, j: (b, j), **spec_args):
    rows, d = x.shape[1:]
    per_tile = tm // STREAM_PIECE
    last = rows // STREAM_PIECE - 1

    def piece(k):
        def index(b, j):
            tb, tj = tile(b, j)
            return tb, jnp.minimum(tj * per_tile + k, last), 0
        return pl.BlockSpec((1, STREAM_PIECE, d), index, **spec_args)

    operands, specs = [x] * per_tile, [piece(k) for k in range(per_tile)]
    if ctx is not None:
        assert ctx.shape[1] == STREAM_PIECE
        operands.append(ctx)
        specs.append(pl.BlockSpec((1, STREAM_PIECE, d), lambda b, j: (tile(b, j)[0], 0, 0)))
    return operands, specs


def _stream_tile(refs, j, *, tm, t_lat, ctx_separate):
    pieces = [r[0] for r in refs[:tm // STREAM_PIECE]]
    if ctx_separate:
        pieces[-1] = jnp.where((j + 1) * tm - STREAM_PIECE >= t_lat, refs[-1][0], pieces[-1])
    return jnp.concatenate(pieces, axis=0)


def _by_rows(fn, split, *arrays):
    tm = arrays[0].shape[0]
    if split == tm:
        return fn(0, *arrays)
    return jnp.concatenate([fn(0, *[a[0:split] for a in arrays]), fn(1, *[a[split:tm] for a in arrays])], axis=0)


def _mod_kernel(c_ref, w_ref, b_ref, o_ref):
    s = _silu(c_ref[...]).astype(BF16)
    o_ref[0] = jnp.dot(s, w_ref[0].astype(BF16), preferred_element_type=F32) + b_ref[0]


def _modulation(cc, mod_w, mod_b):
    depth, d, n = mod_w.shape
    tn = n // 4
    return pl.pallas_call(
        _mod_kernel,
        grid=(depth, n // tn),
        in_specs=[pl.BlockSpec((MOD_ROWS, d), lambda l, j: (0, 0)),
                  pl.BlockSpec((1, d, tn), lambda l, j: (l, 0, j)),
                  pl.BlockSpec((1, 1, tn), lambda l, j: (l, 0, j))],
        out_specs=pl.BlockSpec((1, MOD_ROWS, tn), lambda l, j: (l, 0, j)),
        out_shape=jax.ShapeDtypeStruct((depth, MOD_ROWS, n), F32),
        compiler_params=_cparams("parallel", "parallel"),
        name="modulation",
    )(cc, mod_w, mod_b.reshape(depth, 1, n))


def _rope(z, cos, sin):
    lane = lax.broadcasted_iota(jnp.int32, z.shape, 1)
    first = (lane % 32) < 16
    rot = jnp.where(first, pltpu.roll(z, 128 - 16, 1), pltpu.roll(z, 16, 1))
    return z * cos + rot * sin


def _split3(x):
    p1 = x.astype(BF16)
    r1 = x - p1.astype(F32)
    p2 = r1.astype(BF16)
    p3 = (r1 - p2.astype(F32)).astype(BF16)
    return p1, p2, p3


def _log_gate(z):
    return (jnp.minimum(z, 0.0) - jnp.log(1.0 + jnp.exp(-jnp.abs(z)))) * (1.0 / GLA_GATE_TAU)


def _rope_t(z, cos_t, sin_t):
    parts = []
    for r0 in range(0, z.shape[0], 32):
        parts += [z[r0 + 16:r0 + 32], z[r0:r0 + 16]]
    rot = jnp.concatenate(parts, axis=0)
    reps = z.shape[0] // cos_t.shape[0]
    return z * jnp.concatenate([cos_t] * reps, axis=0) + rot * jnp.concatenate([sin_t] * reps, axis=0)


def _inproj_kernel(*refs, t_lat, tm, ctx_row, n_stream, ctx_separate):
    (mod_ref, g_ref, w_ref, wt_ref, wa_ref, ba_ref, cos_ref, sin_ref, cost_ref, sint_ref,
     gla_ref, lg_ref, gm_ref, sk_ref, sqt_ref, svt_ref, kt_ref, lgt_ref, h_buf, ht_buf) = refs[2 * n_stream:]
    b = pl.program_id(0)
    j = pl.program_id(1)
    nt = pl.num_programs(1)
    step = b * nt + j
    slot = lax.rem(step, 2)

    def prologue(stream_refs, tb, tj, dst):
        x = _stream_tile(stream_refs, tj, tm=tm, t_lat=t_lat, ctx_separate=ctx_separate)
        mod = functools.partial(_mod_vectors, mod_ref, tb, tj, t_lat=t_lat, tm=tm, ctx_row=ctx_row)
        shift, scale, gain = mod(0), mod(1), g_ref[...]
        gains = [gain * (1.0 + s) for s in scale]
        hf = _by_rows(lambda part, x: _rms(x, gains[part]) + shift[part], _row_split(t_lat, tm), x)
        h_buf[dst] = hf.astype(BF16)
        ht_buf[dst] = hf.T.astype(BF16)

    @pl.when(step == 0)
    def _first():
        prologue(refs[n_stream:2 * n_stream], 0, 0, 0)

    def project(cur):
        def proj(off, n):
            return jnp.dot(h_buf[cur], w_ref[:, off:off + n], preferred_element_type=F32)

        def proj_t(off, n):
            return jnp.dot(wt_ref[off:off + n, :], ht_buf[cur], preferred_element_type=F32)

        code = proj(OFF_CODE, COL_CODE).astype(BF16)
        z = proj(0, COL_GLA)
        gla_ref[0, :, 0:GLA_QK] = (z[:, 0:GLA_QK] * (GLA_DK ** -0.5)).astype(BF16)
        gla_ref[0, :, GLA_QK:COL_GLA] = z[:, GLA_QK:COL_GLA].astype(BF16)

        n2 = 2 * GLA_QK
        lg = _log_gate(jnp.dot(code, wa_ref[...], preferred_element_type=F32) + ba_ref[...])
        for i, p in enumerate(_split3(lg)):
            lg_ref[0, :, i * n2:(i + 1) * n2] = p
        for i, p in enumerate(_split3(lg.T)):
            lgt_ref[0, i * n2:(i + 1) * n2, :] = p

        gm_ref[0] = proj(OFF_GM, COL_GM).astype(BF16)
        nb, nj = _next_tile(b, j, pl.num_programs(0), nt)
        prologue(refs[:n_stream], nb, nj, 1 - cur)
        kt_ref[0] = proj_t(ROW_GK, GLA_QK).astype(BF16)

        sk_ref[0] = _rope(proj(OFF_SK, COL_SK), cos_ref[...], sin_ref[...]).astype(BF16)
        sqt_ref[0] = (_rope_t(proj_t(ROW_SQ, SWA_Q), cost_ref[...], sint_ref[...]) * SWA_Q_SCALE).astype(BF16)
        svt_ref[0] = proj_t(ROW_SV, SWA_KV).astype(BF16)

    for cur in range(2):
        pl.when(slot == cur)(functools.partial(project, cur))


def _next_tile(b, j, nb, nt):
    wrap = j + 1 == nt
    last = jnp.logical_and(wrap, b + 1 == nb)
    return jnp.where(wrap & jnp.logical_not(last), b + 1, b), jnp.where(last, j, jnp.where(wrap, 0, j + 1))


def _inproj(x, ctx, mod, n_pre, weights, rope, *, t_lat, t_all, tm, ctx_row):
    bsz, _, d = x.shape
    nt = t_all // tm
    nxt, nxt_specs = _stream_specs(x, ctx, tm, functools.partial(_next_tile, nb=bsz, nt=nt))
    first, first_specs = _stream_specs(x, ctx, tm, lambda b, j: (0, 0))
    stream, stream_specs = nxt + first, nxt_specs + first_specs
    tok = lambda n: pl.BlockSpec((1, tm, n), lambda b, j: (b, j, 0))
    tok_t = lambda n: pl.BlockSpec((1, n, tm), lambda b, j: (b, 0, j))
    n2 = 2 * GLA_QK
    cols = (COL_GLA, 3 * n2, COL_GM, COL_SK)
    rows = (SWA_Q, SWA_KV, GLA_QK, 3 * n2)
    out_shape = [jax.ShapeDtypeStruct((bsz, t_all, n), BF16) for n in cols]
    out_shape += [jax.ShapeDtypeStruct((bsz, n, t_all), BF16) for n in rows]
    consts = (mod, n_pre) + tuple(weights)
    cos, sin, cos_t, sin_t = rope
    return pl.pallas_call(
        functools.partial(_inproj_kernel, t_lat=t_lat, tm=tm, ctx_row=ctx_row, n_stream=len(nxt),
                          ctx_separate=ctx is not None),
        grid=(bsz, nt),
        in_specs=stream_specs + [_const_spec(a.shape) for a in consts] +
                 [pl.BlockSpec((tm, 128), lambda b, j: (j, 0)), pl.BlockSpec((tm, 128), lambda b, j: (j, 0)),
                  pl.BlockSpec((128, tm), lambda b, j: (0, j)), pl.BlockSpec((128, tm), lambda b, j: (0, j))],
        out_specs=[tok(n) for n in cols] + [tok_t(n) for n in rows],
        out_shape=out_shape,
        scratch_shapes=[pltpu.VMEM((2, tm, d), BF16), pltpu.VMEM((2, d, tm), BF16)],
        compiler_params=_cparams("arbitrary", "arbitrary"),
        name="inproj",
    )(*stream, *consts, cos, sin, cos_t, sin_t)


GLA_FAST_CHUNK = 128
GLA_FAST_UNROLL = 2
GLA_SAFE_LOG_DECAY = -40.0


def _gla_fast_tables():
    c = GLA_FAST_CHUNK
    t = np.arange(c)[:, None]
    r = np.arange(c)[None, :]
    incl = [(r <= t), (r >= t)]
    a3 = np.stack([np.tile(m, (1, 3)) for m in incl]).astype(np.float32)
    a3t_neg = -np.stack([np.tile(m.T, (3, 1)) for m in incl]).astype(np.float32)
    mask_w = np.stack([np.tile(m, (1, GLA_HEADS)) for m in incl]).astype(np.float32)
    half = (np.arange(128)[None, :] // GLA_DV) == ((np.arange(GLA_QK)[:, None] // GLA_DK) % 2)
    return a3, a3t_neg, mask_w, half.astype(np.float32)


def _gla_fast_chunks(jobs, a3, a3t_neg, mask_w, bd_c, states):
    c = GLA_FAST_CHUNK
    dk, hd = GLA_DK, GLA_QK
    cums = [(jnp.dot(a3[d], g3, preferred_element_type=F32),
             jnp.dot(gt3, a3t_neg[d], preferred_element_type=F32))
            for d, _, _, _, g3, gt3 in jobs]

    zr = lambda n: jnp.zeros((n, c), BF16)
    qes, dcols, khs, atts = [], [], [], []
    for (d, q, kt, _, _, _), (cum, ncum_t) in zip(jobs, cums):
        qe = (q.astype(F32) * jnp.exp(cum)).astype(BF16)
        ke_t = kt.astype(F32) * jnp.exp(ncum_t)
        edge = 0 if d else c - 1
        dcol = jnp.exp(-ncum_t[:, edge:edge + 1])
        khs.append((ke_t * dcol).astype(BF16))
        ke_t = ke_t.astype(BF16)
        cols = []
        for hh in range(GLA_HEADS):
            parts = ([zr(hh * dk)] if hh else []) + [ke_t[hh * dk:(hh + 1) * dk]]
            parts += [zr(hd - (hh + 1) * dk)] if hh < GLA_HEADS - 1 else []
            cols.append(jnp.concatenate(parts, axis=0))
        kbd = jnp.concatenate(cols, axis=1)
        atts.append(jnp.dot(qe, kbd, preferred_element_type=F32))
        qes.append(qe)
        dcols.append(dcol)

    kvs = [jnp.concatenate([jnp.dot(kh[0:hd // 2], v[:, 0:128], preferred_element_type=F32),
                            jnp.dot(kh[hd // 2:hd], v[:, 128:256], preferred_element_type=F32)], axis=0)
           for (_, _, _, v, _, _), kh in zip(jobs, khs)]

    lane = lax.broadcasted_iota(jnp.int32, (1, 128), 1)
    m_lo = (lane < GLA_DV).astype(BF16)
    m_hi = (lane >= GLA_DV).astype(BF16)
    z128 = jnp.zeros((c, 128), BF16)
    z64 = jnp.zeros((hd // 2, 128), BF16)
    states = list(states)
    outs = []
    for (d, _, _, v, _, _), qe, dcol, att, kv in zip(jobs, qes, dcols, atts, kvs):
        v_lo, v_hi = v[:, 0:128], v[:, 128:256]
        vbd = jnp.concatenate([jnp.concatenate([v_lo * m_lo, z128], axis=1),
                               jnp.concatenate([v_lo * m_hi, z128], axis=1),
                               jnp.concatenate([z128, v_hi * m_lo], axis=1),
                               jnp.concatenate([z128, v_hi * m_hi], axis=1)], axis=0)
        sb = states[d].astype(BF16)
        s_full = jnp.concatenate([jnp.concatenate([sb[0:hd // 2], z64], axis=1),
                                  jnp.concatenate([z64, sb[hd // 2:hd]], axis=1)], axis=0)
        lhs = jnp.concatenate([att.astype(BF16) * mask_w[d], qe], axis=1)
        outs.append(jnp.dot(lhs, jnp.concatenate([vbd, s_full], axis=0),
                            preferred_element_type=F32))
        states[d] = states[d] * dcol + kv * bd_c
    return outs, states


def _gla_tables():
    c = GLA_CHUNK
    t = np.arange(c)[:, None]
    r = np.arange(c)[None, :]
    sizes = [2 * b for b in GLA_LEVELS]
    same = lambda b: (t // b) == (r // b)
    a_f = [same(b) & (r <= t) for b in sizes] + [same(b) & (r > t) for b in sizes]
    a_b = [same(b) & (r >= t) for b in sizes] + [same(b) & (r < t) for b in sizes]
    m_f = [((t // b) % 2 == 1) & ((r // b) == (t // b) - 1) for b in GLA_LEVELS] + [t == r]
    m_b = [((t // b) % 2 == 0) & ((r // b) == (t // b) + 1) for b in GLA_LEVELS] + [t == r]
    amat = np.stack([np.tile(np.concatenate(a, 0), (1, 3)) for a in (a_f, a_b)]).astype(np.float32)
    mask = np.stack([np.stack([np.tile(m, (GLA_HEADS, 1)) for m in ms]) for ms in (m_f, m_b)])
    return amat, mask.astype(np.float32)


def _gla_chunk(q, k, v, g3, amat, masks, hmask_q, bd_v, bd_s, s_prev, *, backward):
    c = GLA_CHUNK
    nl = len(GLA_LEVELS)
    g = g3[0:c].astype(F32) + g3[c:2 * c].astype(F32) + g3[2 * c:3 * c].astype(F32)
    ps = jnp.dot(amat, g3, preferred_element_type=F32)
    blk = lambda i: ps[i * c:(i + 1) * c]
    q_exp = [g] + [blk(i) for i in range(nl - 1)]
    k_exp = [None] + [blk(nl + i) for i in range(nl - 1)]
    q_full = blk(nl - 1)
    k_full = blk(2 * nl - 1)

    att = None
    for lvl in range(nl + 1):
        if lvl < nl:
            qe = q * jnp.exp(q_exp[lvl])
            ke = k if k_exp[lvl] is None else k * jnp.exp(k_exp[lvl])
        else:
            qe, ke = q, k
        qbd = jnp.concatenate([qe.astype(BF16)] * GLA_HEADS, axis=0) * hmask_q
        p = lax.dot_general(qbd, ke.astype(BF16), (((1,), (1,)), ((), ())), preferred_element_type=F32)
        p = p * masks[lvl]
        att = p if att is None else att + p

    r = jnp.dot(att.astype(BF16), v, preferred_element_type=F32) * bd_v
    o = r[0:c]
    for hh in range(1, GLA_HEADS):
        o = o + r[hh * c:(hh + 1) * c]
    o = o + jnp.dot((q * jnp.exp(q_full)).astype(BF16), s_prev.astype(BF16), preferred_element_type=F32)

    tot = q_full[0:1] if backward else q_full[c - 1:c]
    dcol = jnp.transpose(jnp.broadcast_to(jnp.exp(tot), (GLA_QK, GLA_QK)))
    dcol = jnp.concatenate([dcol, dcol], axis=1)
    kv = lax.dot_general((k * jnp.exp(k_full)).astype(BF16), v, (((0,), (0,)), ((), ())),
                         preferred_element_type=F32)
    return o, s_prev * dcol + kv * bd_s


def _gla_kernel(gla_ref, lg_ref, kt_ref, lgt_ref, a3_ref, a3t_ref, mw_ref, bdc_ref, amat_ref, mask_ref,
                hq_ref, bdv_ref, bds_ref, ind_ref, norm_ref, out_ref, of_ref, ob_ref, *, t_lat, t_all, t_out):
    n2 = 2 * GLA_QK
    o_v = 2 * GLA_QK

    def pieces(rows, d):
        return jnp.concatenate([lg_ref[0, rows, i * n2 + d * GLA_QK:i * n2 + (d + 1) * GLA_QK]
                                for i in range(3)], axis=0)

    cf_ = GLA_FAST_CHUNK
    worst = None
    for ci in range(t_all // cf_):
        tot = jnp.sum(lg_ref[0, ci * cf_:(ci + 1) * cf_, 0:n2].astype(F32), axis=0, keepdims=True)
        worst = tot if worst is None else jnp.minimum(worst, tot)
    safe = jnp.min(worst) >= GLA_SAFE_LOG_DECAY

    @pl.when(safe)
    def _fast():
        c = GLA_FAST_CHUNK
        n_lat = t_lat // c
        n_all = t_all // c
        per_step = GLA_FAST_UNROLL
        assert n_all % per_step == 0

        def job(ci, d):
            rows = pl.ds(pl.multiple_of(ci * c, c), c)
            gt3 = jnp.concatenate([lgt_ref[0, i * n2 + d * GLA_QK:i * n2 + (d + 1) * GLA_QK, rows]
                                   for i in range(3)], axis=1)
            return rows, (d, gla_ref[0, rows, 0:GLA_QK], kt_ref[0, :, rows], gla_ref[0, rows, o_v:o_v + GLA_V],
                          pieces(rows, d), gt3)

        def step(i, carry):
            chunks = [(lax.rem(i * per_step + u + n_lat, n_all), 0) for u in range(per_step)]
            chunks += [(n_all - 1 - (i * per_step + u), 1) for u in range(per_step)]
            rows, jobs = zip(*[job(ci, d) for ci, d in chunks])
            outs, states = _gla_fast_chunks(jobs, a3_ref, a3t_ref, mw_ref, bdc_ref[...], carry)
            for (_, d), r, o in zip(chunks, rows, outs):
                (ob_ref if d else of_ref)[r, :] = o
            return tuple(states)

        zero = jnp.zeros((GLA_QK, 128), F32)
        lax.fori_loop(0, n_all // per_step, step, (zero, zero))

    @pl.when(jnp.logical_not(safe))
    def _robust():
        c = GLA_CHUNK
        n_lat = t_lat // c
        n_all = t_all // c
        hq = hq_ref[...]
        bdv = bdv_ref[...]
        bds = bds_ref[...]

        def one(ci, d, s):
            start = pl.multiple_of(ci * c, c)
            rows = pl.ds(start, c)
            q = gla_ref[0, rows, 0:GLA_QK].astype(F32)
            k = gla_ref[0, rows, GLA_QK:2 * GLA_QK].astype(F32)
            v = gla_ref[0, rows, o_v:o_v + GLA_V]
            o, s = _gla_chunk(q, k, v, pieces(rows, d), amat_ref[d],
                              [mask_ref[d, l] for l in range(len(GLA_LEVELS) + 1)], hq, bdv, bds, s,
                              backward=bool(d))
            (ob_ref if d else of_ref)[rows, :] = o
            return s

        def step(i, carry):
            s_f, s_b = carry
            return one(lax.rem(i + n_lat, n_all), 0, s_f), one(n_all - 1 - i, 1, s_b)

        zero = jnp.zeros((GLA_QK, GLA_V), F32)
        lax.fori_loop(0, n_all, step, (zero, zero))

    tr = 256
    ind = ind_ref[...]
    for r0 in range(0, t_out, tr):
        o = of_ref[r0:r0 + tr, :] + ob_ref[r0:r0 + tr, :]
        sq = o * o
        hi = sq.astype(BF16)
        lo = (sq - hi.astype(F32)).astype(BF16)
        ms = jnp.dot(hi, ind, preferred_element_type=F32) + jnp.dot(lo, ind, preferred_element_type=F32)
        gate = gla_ref[0, r0:r0 + tr, o_v + GLA_V:COL_GLA].astype(F32)
        y = o * lax.rsqrt(ms + 1e-6) * norm_ref[...] * _silu(gate)
        out_ref[0, r0:r0 + tr, :] = y.astype(BF16)


def _gla(gla, lg3, kt, lgt3, consts, norm_t, *, t_lat, t_out):
    bsz, t_all, _ = gla.shape
    n2 = 2 * GLA_QK
    return pl.pallas_call(
        functools.partial(_gla_kernel, t_lat=t_lat, t_all=t_all, t_out=t_out),
        grid=(bsz,),
        in_specs=[pl.BlockSpec((1, t_all, COL_GLA), lambda b: (b, 0, 0)),
                  pl.BlockSpec((1, t_all, 3 * n2), lambda b: (b, 0, 0)),
                  pl.BlockSpec((1, GLA_QK, t_all), lambda b: (b, 0, 0)),
                  pl.BlockSpec((1, 3 * n2, t_all), lambda b: (b, 0, 0))] +
                 [_const_spec(a.shape) for a in consts] + [_const_spec(norm_t.shape)],
        out_specs=pl.BlockSpec((1, t_out, GLA_V), lambda b: (b, 0, 0)),
        out_shape=jax.ShapeDtypeStruct((bsz, t_out, GLA_V), BF16),
        scratch_shapes=[pltpu.VMEM((t_all, GLA_V), F32), pltpu.VMEM((t_all, GLA_V), F32)],
        compiler_params=_cparams("parallel"),
        name="gla",
    )(gla, lg3, kt, lgt3, *consts, norm_t)


def _gmlp_tile(z_refs, params, dst):
    rows = z_refs[0].shape[1]
    for k, z_ref in enumerate(z_refs):
        _gmlp_piece(z_ref, params, dst.at[k * rows:(k + 1) * rows])


def _gmlp_piece(z_ref, params, dst):
    lng_ref, lnb_ref, ws_ref, bs_ref, gmask_ref, og_ref = params
    z = z_ref[0].astype(F32)
    zf = 0.5 * z * (1.0 + lax.erf(z * (2.0 ** -0.5)))
    u = zf[:, 0:GMLP_WIDTH]
    v = zf[:, GMLP_WIDTH:]
    mu = jnp.mean(v, axis=-1, keepdims=True)
    vc = v - mu
    v = vc * lax.rsqrt(jnp.mean(vc * vc, axis=-1, keepdims=True) + 1e-5) * lng_ref[...] + lnb_ref[...]
    vb = v.astype(BF16)
    ws = ws_ref[...]
    p = GMLP_CHUNK
    for ci in range(z.shape[0] // p):
        r = jnp.dot(ws, vb[ci * p:(ci + 1) * p], preferred_element_type=F32) * gmask_ref[...]
        mixed = bs_ref[...]
        for g in range(GMLP_GROUPS):
            mixed = mixed + r[g * p:(g + 1) * p]
        y = u[ci * p:(ci + 1) * p] * mixed
        dst[ci * p:(ci + 1) * p, :] = _rms(y, og_ref[...]).astype(BF16)


def _swa_attend(qt_ref, k_ref, vt_ref, sink_ref, og_ref, eye_ref, out_ref, s_ref, p_ref, u, key_rows, biases):
    w = SWA_WINDOW
    d = HEAD_DIM
    kt = SWA_KEY_TILE
    ph = SWA_PART_HEADS
    pw = ph * w
    parts = range(SWA_HEADS // ph)
    group = lambda part: part * ph // SWA_REP
    s_ref = s_ref.at[u]
    p_ref = p_ref.at[u]
    cols = slice(u * w, (u + 1) * w)
    keys = jnp.concatenate([k_ref[0, pl.ds(start, size), :] for start, size in key_rows], axis=0)
    nk = keys.shape[0]

    zero = jnp.zeros((d, pw), BF16)
    sinks, tops = [], []
    for part in parts:
        q = jnp.concatenate([qt_ref[0, h * d:(h + 1) * d, cols] for h in range(part * ph, (part + 1) * ph)], axis=1)
        qbd = jnp.concatenate([q if g == group(part) else zero for g in range(SWA_KV_HEADS)], axis=0)
        best, off = None, 0
        for (_, size), bias in zip(key_rows, biases):
            blk = jnp.dot(keys[off:off + size], qbd, preferred_element_type=F32)
            if bias is not None:
                blk = blk + jnp.concatenate([bias] * ph, axis=1)
            s_ref[part, off:off + size, :] = blk
            m8 = jnp.max(blk.reshape(size // 8, 8, pw), axis=0)
            best = m8 if best is None else jnp.maximum(best, m8)
            off += size
        sink = sink_ref[:, part * pw:(part + 1) * pw] * LOG2E
        sinks.append(sink)
        tops.append(jnp.maximum(jnp.max(best, axis=0, keepdims=True), sink))
    yield

    for part in parts:
        for r0 in range(0, nk, kt):
            p_ref[part, r0:r0 + kt, :] = jnp.exp2(s_ref[part, r0:r0 + kt, :] - tops[part]).astype(BF16)
    yield

    ones = jnp.ones((SWA_DEN_ROWS, nk), BF16)
    vts = [jnp.concatenate([vt_ref[0, g * d:(g + 1) * d, pl.ds(start, size)] for start, size in key_rows] , axis=1)
           for g in range(SWA_KV_HEADS)]
    heads = []
    for part in parts:
        o_ext = jnp.dot(jnp.concatenate([vts[group(part)], ones], axis=0), p_ref[part, 0:nk, :],
                        preferred_element_type=F32)
        scaled = o_ext[0:d] / (o_ext[d:d + 1] + jnp.exp2(sinks[part] - tops[part]))
        heads += [scaled[:, r * w:(r + 1) * w] for r in range(ph)]
    o_t = jnp.concatenate(heads, axis=0)
    yield

    y_t = o_t * lax.rsqrt(jnp.mean(o_t * o_t, axis=0, keepdims=True) + 1e-6) * og_ref[...]
    out = lax.dot_general(eye_ref[...], y_t.astype(BF16), (((1,), (1,)), ((), ())), preferred_element_type=F32)
    out_ref[0, cols, :] = out.astype(BF16)


def _run_stages(blocks):
    for t in range(len(blocks) + 2):
        for u in (t, t - 1, t - 2, t - 1):
            if 0 <= u < len(blocks):
                next(blocks[u], None)


def _swa_kernel(qt_ref, k_ref, vt_ref, sink_ref, og_ref, eye_ref, out_ref, s_ref, p_ref, *, t_lat, t_all, t_out, sb):
    w = SWA_WINDOW
    step = pl.program_id(1)
    n_lat = t_lat // w
    ctx_rows = (t_lat, t_all - t_lat)
    attend = functools.partial(_swa_attend, qt_ref, k_ref, vt_ref, sink_ref, og_ref, eye_ref, out_ref, s_ref, p_ref)

    def run(latent):
        blk = lambda i: (pl.multiple_of(i * w, w), w)
        sk = lax.broadcasted_iota(jnp.int32, (w, w), 0)
        tq = lax.broadcasted_iota(jnp.int32, (w, w), 1)
        neg = jnp.full((w, w), -jnp.inf, F32)
        zero = jnp.zeros((w, w), F32)
        blocks = []
        for u, is_latent in enumerate(latent):
            if not is_latent:
                blocks.append(attend(u, [ctx_rows], [None]))
                continue
            n = step * sb + u
            b_prev = jnp.where((sk >= tq) & (n >= 1), zero, neg)
            b_next = jnp.where((sk <= tq) & (n < n_lat - 1), zero, neg)
            blocks.append(attend(u, [ctx_rows, blk(jnp.maximum(n - 1, 0)), blk(n), blk(jnp.minimum(n + 1, n_lat - 1))],
                                 [None, b_prev, None, b_next]))
        _run_stages(blocks)

    n_steps = t_out // (sb * w)
    makeup = [tuple(s * sb + u < n_lat for u in range(sb)) for s in range(n_steps)]
    for kind in sorted(set(makeup), reverse=True):
        steps = [s for s in range(n_steps) if makeup[s] == kind]
        assert steps == list(range(steps[0], steps[-1] + 1))
        pl.when((step >= steps[0]) & (step <= steps[-1]))(functools.partial(run, kind))


def _swa(sqt, sk, svt, sink_t, out_g, eye, *, t_lat, t_out, sb):
    bsz, t_all, _ = sk.shape
    w = SWA_WINDOW
    n_keys = t_all - t_lat + 3 * w
    n_parts = SWA_HEADS // SWA_PART_HEADS
    assert t_out % (sb * w) == 0
    return pl.pallas_call(
        functools.partial(_swa_kernel, t_lat=t_lat, t_all=t_all, t_out=t_out, sb=sb),
        grid=(bsz, t_out // (sb * w)),
        in_specs=[pl.BlockSpec((1, SWA_Q, sb * w), lambda b, n: (b, 0, n)),
                  pl.BlockSpec((1, t_all, SWA_KV), lambda b, n: (b, 0, 0)),
                  pl.BlockSpec((1, SWA_KV, t_all), lambda b, n: (b, 0, 0)),
                  _const_spec(sink_t.shape), _const_spec(out_g.shape), _const_spec(eye.shape)],
        out_specs=pl.BlockSpec((1, sb * w, SWA_Q), lambda b, n: (b, n, 0)),
        out_shape=jax.ShapeDtypeStruct((bsz, t_out, SWA_Q), BF16),
        scratch_shapes=[pltpu.VMEM((sb, n_parts, n_keys, SWA_PART_HEADS * w), F32),
                        pltpu.VMEM((sb, n_parts, n_keys, SWA_PART_HEADS * w), BF16)],
        compiler_params=_cparams("parallel", "parallel"),
        name="swa",
    )(sqt, sk, svt, sink_t, out_g, eye)


FFN_CHUNKS = ((0, 1536), (1536, 1280))


def _post_kernel(*refs, t_lat, tm, ctx_row, n_stream, ctx_separate):
    n_gm = tm // STREAM_PIECE
    gm_next, gm_first = refs[n_stream:n_stream + n_gm], refs[n_stream + n_gm:n_stream + 2 * n_gm]
    (a_ref, c_ref, mod_ref, n1_ref, n2a_ref, n2b_ref, wo_ref, wgu_ref, wd_ref,
     *gmlp_params, out_ref, b_buf) = refs[n_stream + 2 * n_gm:]
    b = pl.program_id(0)
    j = pl.program_id(1)
    step = b * pl.num_programs(1) + j

    @pl.when(step == 0)
    def _first():
        _gmlp_tile(gm_first, gmlp_params, b_buf.at[0])

    def tile(cur):
        x = _stream_tile(refs[:n_stream], j, tm=tm, t_lat=t_lat, ctx_separate=ctx_separate)
        mod = functools.partial(_mod_vectors, mod_ref, b, j, t_lat=t_lat, tm=tm, ctx_row=ctx_row)
        split = _row_split(t_lat, tm)
        gate1, shift2, scale2, gate2 = mod(2), mod(3), mod(4), mod(5)
        g1 = [n1_ref[...] * g for g in gate1]
        g2a = [n2a_ref[...] * (1.0 + s) for s in scale2]
        g2b = [n2b_ref[...] * g for g in gate2]

        cat = jnp.concatenate([a_ref[0], b_buf[cur], c_ref[0]], axis=-1)
        y = jnp.dot(cat, wo_ref[...], preferred_element_type=F32)
        x1 = _by_rows(lambda p, x, y: x + _rms(y, g1[p]), split, x, y)
        h = _by_rows(lambda p, x: (_rms(x, g2a[p]) + shift2[p]).astype(BF16), split, x1)
        f = None
        for i, (off, n) in enumerate(FFN_CHUNKS):
            gt = jnp.dot(h, wgu_ref[:, off:off + n], preferred_element_type=F32)
            up = jnp.dot(h, wgu_ref[:, D_FF + off:D_FF + off + n], preferred_element_type=F32)
            if i == 0:
                _gmlp_tile(gm_next, gmlp_params, b_buf.at[1 - cur])
            act = (_silu(gt) * up).astype(BF16)
            part = jnp.dot(act, wd_ref[off:off + n, :], preferred_element_type=F32)
            f = part if f is None else f + part
        out_ref[0] = _by_rows(lambda p, x, f: x + _rms(f, g2b[p]), split, x1, f)

    for cur in range(2):
        pl.when(lax.rem(step, 2) == cur)(functools.partial(tile, cur))


def _post(a, gm, c, x, ctx, mod, n1_post, n2_pre, n2_post, wo, wgu, wd, gmlp_params, *, t_lat, t_out, tm, ctx_row):
    bsz = x.shape[0]
    nt = t_out // tm
    stream, stream_specs = _stream_specs(x, ctx, tm)
    tok = lambda n: pl.BlockSpec((1, tm, n), lambda b, j: (b, j, 0))
    once = lambda arr: pl.BlockSpec(arr.shape, lambda b, j: (0,) * arr.ndim, pipeline_mode=pl.Buffered(1))
    gm_next, gm_next_specs = _stream_specs(gm, None, tm, functools.partial(_next_tile, nb=bsz, nt=nt))
    gm_first, gm_first_specs = _stream_specs(gm, None, tm, lambda b, j: (0, 0), pipeline_mode=pl.Buffered(1))
    return pl.pallas_call(
        functools.partial(_post_kernel, t_lat=t_lat, tm=tm, ctx_row=ctx_row, n_stream=len(stream),
                          ctx_separate=ctx is not None),
        grid=(bsz, nt),
        in_specs=stream_specs + gm_next_specs + gm_first_specs + [tok(GLA_V), tok(SWA_Q), _const_spec(mod.shape),
                                 _const_spec(n1_post.shape), _const_spec(n2_pre.shape), _const_spec(n2_post.shape),
                                 once(wo), once(wgu), once(wd)] + [_const_spec(p.shape) for p in gmlp_params],
        out_specs=tok(D_MODEL),
        out_shape=jax.ShapeDtypeStruct((bsz, t_out, D_MODEL), F32),
        scratch_shapes=[pltpu.VMEM((2, tm, GMLP_WIDTH), BF16)],
        compiler_params=_cparams("arbitrary", "arbitrary"),
        name="post",
    )(*stream, *gm_next, *gm_first, a, c, mod, n1_post, n2_pre, n2_post, wo, wgu, wd, *gmlp_params)


def _rope_tables(t_lat, t_ctx):
    rows = t_lat // GRID_W
    row = jnp.repeat(jnp.arange(rows), GRID_W).astype(F32)
    col = jnp.tile(jnp.arange(GRID_W), rows).astype(F32)
    inv_freq = jnp.power(ROPE_THETA, -jnp.arange(0, ROPE_AXIS_DIM, 2, dtype=F32) / ROPE_AXIS_DIM)
    ang_row = row[:, None] * inv_freq[None, :]
    ang_col = col[:, None] * inv_freq[None, :]
    ang = jnp.concatenate([ang_row, ang_row, ang_col, ang_col], axis=-1)
    sign = jnp.tile(jnp.concatenate([-jnp.ones((16,), F32), jnp.ones((16,), F32)]), 2)
    cos = jnp.tile(jnp.concatenate([jnp.cos(ang), jnp.ones((t_ctx, HEAD_DIM), F32)], axis=0), (1, 2))
    sin = jnp.tile(jnp.concatenate([jnp.sin(ang) * sign, jnp.zeros((t_ctx, HEAD_DIM), F32)], axis=0), (1, 2))
    return cos, sin, cos.T, sin.T


def _inproj_weights(w, wa2, ba):
    r = GLA_GATE_RANK
    o_code = 2 * GLA_QK + 2 * GLA_V
    o_gm = o_code + 2 * r
    o_sq = o_gm + 2 * GMLP_WIDTH
    o_sk = o_sq + SWA_Q
    o_sv = o_sk + SWA_KV
    code = jnp.concatenate([w[:, o_code:o_gm], jnp.zeros((w.shape[0], COL_CODE - 2 * r), w.dtype)], axis=1)
    w_tok = jnp.concatenate([w[:, 0:o_code], w[:, o_gm:o_sq], w[:, o_sk:o_sv], code], axis=1)
    w_t = jnp.concatenate([w[:, o_sq:o_sk], w[:, o_sv:], w[:, GLA_QK:2 * GLA_QK]], axis=1).T
    wa = jnp.zeros((COL_CODE, 2 * GLA_QK), F32)
    wa = wa.at[0:r, 0:GLA_QK].set(wa2[0]).at[r:2 * r, GLA_QK:].set(wa2[1])
    return w_tok.astype(BF16), w_t.astype(BF16), wa.astype(BF16), ba.reshape(1, 2 * GLA_QK)


def _gla_consts():
    a3, a3t_neg, mask_w, bd_c = _gla_fast_tables()
    amat, mask = _gla_tables()
    lane_head = np.arange(GLA_QK)[None, :] // GLA_DK
    row_head = np.arange(GLA_HEADS * GLA_CHUNK)[:, None] // GLA_CHUNK
    hq = (lane_head == row_head).astype(np.float32)
    vlane_head = np.arange(GLA_V)[None, :] // GLA_DV
    bdv = (vlane_head == row_head).astype(np.float32)
    bds = (vlane_head == (np.arange(GLA_QK)[:, None] // GLA_DK)).astype(np.float32)
    ind = (vlane_head == vlane_head.T).astype(np.float32) / GLA_DV
    return (jnp.asarray(a3, BF16), jnp.asarray(a3t_neg, BF16), jnp.asarray(mask_w, BF16),
            jnp.asarray(bd_c, F32), jnp.asarray(amat, BF16), jnp.asarray(mask, F32), jnp.asarray(hq, BF16),
            jnp.asarray(bdv, F32), jnp.asarray(bds, F32), jnp.asarray(ind, BF16))


def kernel(x, c, ctx, c_ctx, mod_w, mod_b, n1_pre, n1_post, n2_pre, n2_post, w_in, w_out, gla_wa2, gla_ba,
           gla_norm, gmlp_ln_g, gmlp_ln_b, gmlp_ws, gmlp_bs, gmlp_out_g, swa_sink, swa_out_g, ffn_w_gu,
           ffn_w_down):
    bsz, t_lat, d = x.shape
    t_ctx = ctx.shape[1]
    t_all = t_lat + t_ctx
    depth = mod_w.shape[0]
    assert d == D_MODEL and bsz < MOD_ROWS
    assert t_lat % 1024 == 0 and t_ctx % 256 == 0 and t_all % 768 == 0
    ctx_row = bsz
    tm_all = 768
    assert t_ctx == tm_all - _row_split(t_lat, tm_all)

    cc = jnp.zeros((MOD_ROWS, d), F32).at[0:bsz].set(c).at[ctx_row].set(c_ctx)
    mods = _modulation(cc, mod_w, mod_b)
    rope = _rope_tables(t_lat, t_ctx)
    gla_consts = _gla_consts()
    eye = jnp.eye(SWA_WINDOW, dtype=BF16)
    gmask = jnp.asarray((np.arange(GMLP_WIDTH)[None, :] // GMLP_GDIM ==
                         np.arange(GMLP_GROUPS * GMLP_CHUNK)[:, None] // GMLP_CHUNK).astype(np.float32))
    row = lambda v: v.reshape(1, -1)

    assert t_ctx == STREAM_PIECE
    xs, xs_ctx = x, ctx
    for l in range(depth):
        last = l == depth - 1
        t_out = t_lat if last else t_all
        gla, lg3, gm, sk, sqt, svt, kt, lgt3 = _inproj(
            xs, xs_ctx, mods[l], row(n1_pre[l]), _inproj_weights(w_in[l], gla_wa2[l], gla_ba[l]), rope,
            t_lat=t_lat, t_all=t_all, tm=tm_all, ctx_row=ctx_row)
        a_out = _gla(gla, lg3, kt, lgt3, gla_consts, row(jnp.tile(gla_norm[l], GLA_HEADS)),
                     t_lat=t_lat, t_out=t_out)
        ws_s = gmlp_ws[l].reshape(GMLP_GROUPS * GMLP_CHUNK, GMLP_CHUNK).astype(BF16)
        bs_t = jnp.repeat(gmlp_bs[l].T, GMLP_GDIM, axis=1)
        gmlp_params = (row(gmlp_ln_g[l]), row(gmlp_ln_b[l]), ws_s, bs_t, gmask, row(gmlp_out_g[l]))
        sink_t = row(jnp.repeat(swa_sink[l], SWA_WINDOW))
        out_g_t = jnp.broadcast_to(swa_out_g[l][:, None], (SWA_Q, SWA_WINDOW))
        sb = next(n for n in SWA_STEP_BLOCKS if (t_out // SWA_WINDOW) % n == 0)
        c_out = _swa(sqt, sk, svt, sink_t, out_g_t, eye, t_lat=t_lat, t_out=t_out, sb=sb)
        xs = _post(a_out, gm, c_out, xs, xs_ctx, mods[l], row(n1_post[l]), row(n2_pre[l]), row(n2_post[l]),
                   w_out[l].astype(BF16), ffn_w_gu[l].astype(BF16), ffn_w_down[l].astype(BF16), gmlp_params,
                   t_lat=t_lat, t_out=t_out, tm=1024 if last else tm_all, ctx_row=ctx_row)
        xs_ctx = None
    return xs
```

```python
import functools

import numpy as np
import jax
import jax.numpy as jnp
from jax import lax
from jax.experimental import pallas as pl
from jax.experimental.pallas import tpu as pltpu

F32 = jnp.float32
BF16 = jnp.bfloat16

D_MODEL = 1024
GRID_W = 64
HEAD_DIM = 64
GLA_HEADS = 4
GLA_DK = 32
GLA_DV = 64
GLA_QK = GLA_HEADS * GLA_DK
GLA_V = GLA_HEADS * GLA_DV
GLA_GATE_RANK = 16
GLA_GATE_TAU = 16.0
GLA_CHUNK = 64
GMLP_GROUPS = 4
GMLP_GDIM = 64
GMLP_WIDTH = GMLP_GROUPS * GMLP_GDIM
GMLP_CHUNK = 128
SWA_HEADS = 8
SWA_KV_HEADS = 2
SWA_REP = SWA_HEADS // SWA_KV_HEADS
SWA_Q = SWA_HEADS * HEAD_DIM
SWA_KV = SWA_KV_HEADS * HEAD_DIM
SWA_WINDOW = 128
ROPE_AXIS_DIM = HEAD_DIM // 2
ROPE_THETA = 10000.0
MIX_WIDTH = GLA_V + GMLP_WIDTH + SWA_Q
D_FF = -(-8 * D_MODEL // (3 * 256)) * 256
N_MOD = 6
MOD_ROWS = 24

COL_GLA = 2 * GLA_QK + 2 * GLA_V
COL_GM = 2 * GMLP_WIDTH
COL_SK = SWA_KV
COL_CODE = 128
OFF_GM = COL_GLA
OFF_SK = OFF_GM + COL_GM
OFF_CODE = OFF_SK + COL_SK
ROW_SQ = 0
ROW_SV = ROW_SQ + SWA_Q
ROW_GK = ROW_SV + SWA_KV

LOG2E = 1.4426950408889634
SWA_Q_SCALE = HEAD_DIM ** -0.5 * LOG2E
SWA_KEY_TILE = 32
SWA_PART_HEADS = 2
SWA_STEP_BLOCKS = (8, 6, 4, 3)
SWA_DEN_ROWS = 16

VMEM_LIMIT = 56 * 1024 * 1024
GLA_LEVELS = (1, 2, 4, 8, 16, 32)


def _cparams(*sem):
    return pltpu.CompilerParams(dimension_semantics=sem, vmem_limit_bytes=VMEM_LIMIT)


def _const_spec(shape):
    nd = len(shape)
    return pl.BlockSpec(shape, lambda *_: (0,) * nd)


def _rms(x, g, eps=1e-6):
    return x * lax.rsqrt(jnp.mean(x * x, axis=-1, keepdims=True) + eps) * g


def _silu(x):
    return x * jax.nn.sigmoid(x)


def _row_split(t_lat, tm):
    return t_lat % tm if t_lat % tm else tm


def _mod_vectors(mod_ref, b, j, idx, *, t_lat, tm, ctx_row):
    lo = idx * D_MODEL
    lat = mod_ref[pl.ds(b, 1), lo:lo + D_MODEL]
    ctx = mod_ref[ctx_row:ctx_row + 1, lo:lo + D_MODEL]
    return lat, jnp.where(j * tm + _row_split(t_lat, tm) >= t_lat, ctx, lat)


STREAM_PIECE = 256


def _stream_specs(x, ctx, tm, tile=lambda b, j: (b, j), **spec_args):
    rows, d = x.shape[1:]
    per_tile = tm // STREAM_PIECE
    last = rows // STREAM_PIECE - 1

    def piece(k):
        def index(b, j):
            tb, tj = tile(b, j)
            return tb, jnp.minimum(tj * per_tile + k, last), 0
        return pl.BlockSpec((1, STREAM_PIECE, d), index, **spec_args)

    operands, specs = [x] * per_tile, [piece(k) for k in range(per_tile)]
    if ctx is not None:
        assert ctx.shape[1] == STREAM_PIECE
        operands.append(ctx)
        specs.append(pl.BlockSpec((1, STREAM_PIECE, d), lambda b, j: (tile(b, j)[0], 0, 0)))
    return operands, specs


def _stream_tile(refs, j, *, tm, t_lat, ctx_separate):
    pieces = [r[0] for r in refs[:tm // STREAM_PIECE]]
    if ctx_separate:
        pieces[-1] = jnp.where((j + 1) * tm - STREAM_PIECE >= t_lat, refs[-1][0], pieces[-1])
    return jnp.concatenate(pieces, axis=0)


def _by_rows(fn, split, *arrays):
    tm = arrays[0].shape[0]
    if split == tm:
        return fn(0, *arrays)
    return jnp.concatenate([fn(0, *[a[0:split] for a in arrays]), fn(1, *[a[split:tm] for a in arrays])], axis=0)


def _mod_kernel(c_ref, w_ref, b_ref, o_ref):
    s = _silu(c_ref[...]).astype(BF16)
    o_ref[0] = jnp.dot(s, w_ref[0].astype(BF16), preferred_element_type=F32) + b_ref[0]


def _modulation(cc, mod_w, mod_b):
    depth, d, n = mod_w.shape
    tn = n // 4
    return pl.pallas_call(
        _mod_kernel,
        grid=(depth, n // tn),
        in_specs=[pl.BlockSpec((MOD_ROWS, d), lambda l, j: (0, 0)),
                  pl.BlockSpec((1, d, tn), lambda l, j: (l, 0, j)),
                  pl.BlockSpec((1, 1, tn), lambda l, j: (l, 0, j))],
        out_specs=pl.BlockSpec((1, MOD_ROWS, tn), lambda l, j: (l, 0, j)),
        out_shape=jax.ShapeDtypeStruct((depth, MOD_ROWS, n), F32),
        compiler_params=_cparams("parallel", "parallel"),
        name="modulation",
    )(cc, mod_w, mod_b.reshape(depth, 1, n))


def _rope(z, cos, sin):
    lane = lax.broadcasted_iota(jnp.int32, z.shape, 1)
    first = (lane % 32) < 16
    rot = jnp.where(first, pltpu.roll(z, 128 - 16, 1), pltpu.roll(z, 16, 1))
    return z * cos + rot * sin


GATE_PIECES = 2


def _split(x):
    p1 = x.astype(BF16)
    return p1, (x - p1.astype(F32)).astype(BF16)


def _log_gate(z):
    return (jnp.minimum(z, 0.0) - jnp.log(1.0 + jnp.exp(-jnp.abs(z)))) * (1.0 / GLA_GATE_TAU)


def _rope_t(z, cos_t, sin_t):
    parts = []
    for r0 in range(0, z.shape[0], 32):
        parts += [z[r0 + 16:r0 + 32], z[r0:r0 + 16]]
    rot = jnp.concatenate(parts, axis=0)
    reps = z.shape[0] // cos_t.shape[0]
    return z * jnp.concatenate([cos_t] * reps, axis=0) + rot * jnp.concatenate([sin_t] * reps, axis=0)


def _inproj_kernel(*refs, t_lat, tm, ctx_row, n_stream, ctx_separate):
    (mod_ref, g_ref, w_ref, wt_ref, wa_ref, ba_ref, cos_ref, sin_ref, cost_ref, sint_ref,
     gla_ref, lg_ref, gm_ref, sk_ref, sqt_ref, svt_ref, kt_ref, lgt_ref, h_buf, ht_buf) = refs[2 * n_stream:]
    b = pl.program_id(0)
    j = pl.program_id(1)
    nt = pl.num_programs(1)
    step = b * nt + j
    slot = lax.rem(step, 2)

    def prologue(stream_refs, tb, tj, dst):
        x = _stream_tile(stream_refs, tj, tm=tm, t_lat=t_lat, ctx_separate=ctx_separate)
        mod = functools.partial(_mod_vectors, mod_ref, tb, tj, t_lat=t_lat, tm=tm, ctx_row=ctx_row)
        shift, scale, gain = mod(0), mod(1), g_ref[...]
        gains = [gain * (1.0 + s) for s in scale]
        hf = _by_rows(lambda part, x: _rms(x, gains[part]) + shift[part], _row_split(t_lat, tm), x)
        h_buf[dst] = hf.astype(BF16)
        ht_buf[dst] = hf.T.astype(BF16)

    @pl.when(step == 0)
    def _first():
        prologue(refs[n_stream:2 * n_stream], 0, 0, 0)

    def project(cur):
        def proj(off, n):
            return jnp.dot(h_buf[cur], w_ref[:, off:off + n], preferred_element_type=F32)

        def proj_t(off, n):
            return jnp.dot(wt_ref[off:off + n, :], ht_buf[cur], preferred_element_type=F32)

        code = proj(OFF_CODE, COL_CODE).astype(BF16)
        z = proj(0, COL_GLA)
        gla_ref[0, :, 0:GLA_QK] = (z[:, 0:GLA_QK] * (GLA_DK ** -0.5)).astype(BF16)
        gla_ref[0, :, GLA_QK:COL_GLA] = z[:, GLA_QK:COL_GLA].astype(BF16)

        n2 = 2 * GLA_QK
        lg = _log_gate(jnp.dot(code, wa_ref[...], preferred_element_type=F32) + ba_ref[...])
        for i, p in enumerate(_split(lg)):
            lg_ref[0, :, i * n2:(i + 1) * n2] = p
        for i, p in enumerate(_split(lg.T)):
            lgt_ref[0, i * n2:(i + 1) * n2, :] = p

        gm_ref[0] = proj(OFF_GM, COL_GM).astype(BF16)
        nb, nj = _next_tile(b, j, pl.num_programs(0), nt)
        prologue(refs[:n_stream], nb, nj, 1 - cur)
        kt_ref[0] = proj_t(ROW_GK, GLA_QK).astype(BF16)

        sk_ref[0] = _rope(proj(OFF_SK, COL_SK), cos_ref[...], sin_ref[...]).astype(BF16)
        sqt_ref[0] = (_rope_t(proj_t(ROW_SQ, SWA_Q), cost_ref[...], sint_ref[...]) * SWA_Q_SCALE).astype(BF16)
        svt_ref[0] = proj_t(ROW_SV, SWA_KV).astype(BF16)

    for cur in range(2):
        pl.when(slot == cur)(functools.partial(project, cur))


def _next_tile(b, j, nb, nt):
    wrap = j + 1 == nt
    last = jnp.logical_and(wrap, b + 1 == nb)
    return jnp.where(wrap & jnp.logical_not(last), b + 1, b), jnp.where(last, j, jnp.where(wrap, 0, j + 1))


def _inproj(x, ctx, mod, n_pre, weights, rope, *, t_lat, t_all, tm, ctx_row):
    bsz, _, d = x.shape
    nt = t_all // tm
    nxt, nxt_specs = _stream_specs(x, ctx, tm, functools.partial(_next_tile, nb=bsz, nt=nt))
    first, first_specs = _stream_specs(x, ctx, tm, lambda b, j: (0, 0))
    stream, stream_specs = nxt + first, nxt_specs + first_specs
    tok = lambda n: pl.BlockSpec((1, tm, n), lambda b, j: (b, j, 0))
    tok_t = lambda n: pl.BlockSpec((1, n, tm), lambda b, j: (b, 0, j))
    n2 = 2 * GLA_QK
    cols = (COL_GLA, GATE_PIECES * n2, COL_GM, COL_SK)
    rows = (SWA_Q, SWA_KV, GLA_QK, GATE_PIECES * n2)
    out_shape = [jax.ShapeDtypeStruct((bsz, t_all, n), BF16) for n in cols]
    out_shape += [jax.ShapeDtypeStruct((bsz, n, t_all), BF16) for n in rows]
    consts = (mod, n_pre) + tuple(weights)
    cos, sin, cos_t, sin_t = rope
    return pl.pallas_call(
        functools.partial(_inproj_kernel, t_lat=t_lat, tm=tm, ctx_row=ctx_row, n_stream=len(nxt),
                          ctx_separate=ctx is not None),
        grid=(bsz, nt),
        in_specs=stream_specs + [_const_spec(a.shape) for a in consts] +
                 [pl.BlockSpec((tm, 128), lambda b, j: (j, 0)), pl.BlockSpec((tm, 128), lambda b, j: (j, 0)),
                  pl.BlockSpec((128, tm), lambda b, j: (0, j)), pl.BlockSpec((128, tm), lambda b, j: (0, j))],
        out_specs=[tok(n) for n in cols] + [tok_t(n) for n in rows],
        out_shape=out_shape,
        scratch_shapes=[pltpu.VMEM((2, tm, d), BF16), pltpu.VMEM((2, d, tm), BF16)],
        compiler_params=_cparams("arbitrary", "arbitrary"),
        name="inproj",
    )(*stream, *consts, cos, sin, cos_t, sin_t)


GLA_FAST_CHUNK = 128
GLA_FAST_UNROLL = 2
GLA_SAFE_LOG_DECAY = -40.0


def _gla_fast_tables():
    c = GLA_FAST_CHUNK
    t = np.arange(c)[:, None]
    r = np.arange(c)[None, :]
    incl = [(r <= t), (r >= t)]
    a3 = np.stack([np.tile(m, (1, GATE_PIECES)) for m in incl]).astype(np.float32)
    a3t_neg = -np.stack([np.tile(m.T, (GATE_PIECES, 1)) for m in incl]).astype(np.float32)
    mask_w = np.stack([np.tile(m, (1, GLA_HEADS)) for m in incl]).astype(np.float32)
    half = (np.arange(128)[None, :] // GLA_DV) == ((np.arange(GLA_QK)[:, None] // GLA_DK) % 2)
    return a3, a3t_neg, mask_w, half.astype(np.float32)


def _gla_fast_chunks(jobs, a3, a3t_neg, mask_w, bd_c, states):
    c = GLA_FAST_CHUNK
    dk, hd = GLA_DK, GLA_QK
    cums = [(jnp.dot(a3[d], g3, preferred_element_type=F32),
             jnp.dot(gt3, a3t_neg[d], preferred_element_type=F32))
            for d, _, _, _, g3, gt3 in jobs]

    zr = lambda n: jnp.zeros((n, c), BF16)
    qes, dcols, khs, atts = [], [], [], []
    for (d, q, kt, _, _, _), (cum, ncum_t) in zip(jobs, cums):
        qe = (q.astype(F32) * jnp.exp(cum)).astype(BF16)
        ke_t = kt.astype(F32) * jnp.exp(ncum_t)
        edge = 0 if d else c - 1
        dcol = jnp.exp(-ncum_t[:, edge:edge + 1])
        khs.append((ke_t * dcol).astype(BF16))
        ke_t = ke_t.astype(BF16)
        cols = []
        for hh in range(GLA_HEADS):
            parts = ([zr(hh * dk)] if hh else []) + [ke_t[hh * dk:(hh + 1) * dk]]
            parts += [zr(hd - (hh + 1) * dk)] if hh < GLA_HEADS - 1 else []
            cols.append(jnp.concatenate(parts, axis=0))
        kbd = jnp.concatenate(cols, axis=1)
        atts.append(jnp.dot(qe, kbd, preferred_element_type=F32))
        qes.append(qe)
        dcols.append(dcol)

    kvs = [jnp.concatenate([jnp.dot(kh[0:hd // 2], v[:, 0:128], preferred_element_type=F32),
                            jnp.dot(kh[hd // 2:hd], v[:, 128:256], preferred_element_type=F32)], axis=0)
           for (_, _, _, v, _, _), kh in zip(jobs, khs)]

    lane = lax.broadcasted_iota(jnp.int32, (1, 128), 1)
    m_lo = (lane < GLA_DV).astype(BF16)
    m_hi = (lane >= GLA_DV).astype(BF16)
    z128 = jnp.zeros((c, 128), BF16)
    z64 = jnp.zeros((hd // 2, 128), BF16)
    states = list(states)
    outs = []
    for (d, _, _, v, _, _), qe, dcol, att, kv in zip(jobs, qes, dcols, atts, kvs):
        v_lo, v_hi = v[:, 0:128], v[:, 128:256]
        vbd = jnp.concatenate([jnp.concatenate([v_lo * m_lo, z128], axis=1),
                               jnp.concatenate([v_lo * m_hi, z128], axis=1),
                               jnp.concatenate([z128, v_hi * m_lo], axis=1),
                               jnp.concatenate([z128, v_hi * m_hi], axis=1)], axis=0)
        sb = states[d].astype(BF16)
        s_full = jnp.concatenate([jnp.concatenate([sb[0:hd // 2], z64], axis=1),
                                  jnp.concatenate([z64, sb[hd // 2:hd]], axis=1)], axis=0)
        lhs = jnp.concatenate([att.astype(BF16) * mask_w[d], qe], axis=1)
        outs.append(jnp.dot(lhs, jnp.concatenate([vbd, s_full], axis=0),
                            preferred_element_type=F32))
        states[d] = states[d] * dcol + kv * bd_c
    return outs, states


def _gla_tables():
    c = GLA_CHUNK
    t = np.arange(c)[:, None]
    r = np.arange(c)[None, :]
    sizes = [2 * b for b in GLA_LEVELS]
    same = lambda b: (t // b) == (r // b)
    a_f = [same(b) & (r <= t) for b in sizes] + [same(b) & (r > t) for b in sizes]
    a_b = [same(b) & (r >= t) for b in sizes] + [same(b) & (r < t) for b in sizes]
    m_f = [((t // b) % 2 == 1) & ((r // b) == (t // b) - 1) for b in GLA_LEVELS] + [t == r]
    m_b = [((t // b) % 2 == 0) & ((r // b) == (t // b) + 1) for b in GLA_LEVELS] + [t == r]
    amat = np.stack([np.tile(np.concatenate(a, 0), (1, GATE_PIECES)) for a in (a_f, a_b)]).astype(np.float32)
    mask = np.stack([np.stack([np.tile(m, (GLA_HEADS, 1)) for m in ms]) for ms in (m_f, m_b)])
    return amat, mask.astype(np.float32)


def _gla_chunk(q, k, v, g3, amat, masks, hmask_q, bd_v, bd_s, s_prev, *, backward):
    c = GLA_CHUNK
    nl = len(GLA_LEVELS)
    g = sum(g3[i * c:(i + 1) * c].astype(F32) for i in range(GATE_PIECES))
    ps = jnp.dot(amat, g3, preferred_element_type=F32)
    blk = lambda i: ps[i * c:(i + 1) * c]
    q_exp = [g] + [blk(i) for i in range(nl - 1)]
    k_exp = [None] + [blk(nl + i) for i in range(nl - 1)]
    q_full = blk(nl - 1)
    k_full = blk(2 * nl - 1)

    att = None
    for lvl in range(nl + 1):
        if lvl < nl:
            qe = q * jnp.exp(q_exp[lvl])
            ke = k if k_exp[lvl] is None else k * jnp.exp(k_exp[lvl])
        else:
            qe, ke = q, k
        qbd = jnp.concatenate([qe.astype(BF16)] * GLA_HEADS, axis=0) * hmask_q
        p = lax.dot_general(qbd, ke.astype(BF16), (((1,), (1,)), ((), ())), preferred_element_type=F32)
        p = p * masks[lvl]
        att = p if att is None else att + p

    r = jnp.dot(att.astype(BF16), v, preferred_element_type=F32) * bd_v
    o = r[0:c]
    for hh in range(1, GLA_HEADS):
        o = o + r[hh * c:(hh + 1) * c]
    o = o + jnp.dot((q * jnp.exp(q_full)).astype(BF16), s_prev.astype(BF16), preferred_element_type=F32)

    tot = q_full[0:1] if backward else q_full[c - 1:c]
    dcol = jnp.transpose(jnp.broadcast_to(jnp.exp(tot), (GLA_QK, GLA_QK)))
    dcol = jnp.concatenate([dcol, dcol], axis=1)
    kv = lax.dot_general((k * jnp.exp(k_full)).astype(BF16), v, (((0,), (0,)), ((), ())),
                         preferred_element_type=F32)
    return o, s_prev * dcol + kv * bd_s


def _gla_kernel(gla_ref, lg_ref, kt_ref, lgt_ref, a3_ref, a3t_ref, mw_ref, bdc_ref, amat_ref, mask_ref,
                hq_ref, bdv_ref, bds_ref, ind_ref, norm_ref, out_ref, of_ref, ob_ref, *, t_lat, t_all, t_out):
    n2 = 2 * GLA_QK
    o_v = 2 * GLA_QK

    def pieces(rows, d):
        return jnp.concatenate([lg_ref[0, rows, i * n2 + d * GLA_QK:i * n2 + (d + 1) * GLA_QK]
                                for i in range(GATE_PIECES)], axis=0)

    cf_ = GLA_FAST_CHUNK
    worst = None
    for ci in range(t_all // cf_):
        tot = jnp.sum(lg_ref[0, ci * cf_:(ci + 1) * cf_, 0:n2].astype(F32), axis=0, keepdims=True)
        worst = tot if worst is None else jnp.minimum(worst, tot)
    safe = jnp.min(worst) >= GLA_SAFE_LOG_DECAY

    @pl.when(safe)
    def _fast():
        c = GLA_FAST_CHUNK
        n_lat = t_lat // c
        n_all = t_all // c
        per_step = GLA_FAST_UNROLL
        assert n_all % per_step == 0

        def job(ci, d):
            rows = pl.ds(pl.multiple_of(ci * c, c), c)
            gt3 = jnp.concatenate([lgt_ref[0, i * n2 + d * GLA_QK:i * n2 + (d + 1) * GLA_QK, rows]
                                   for i in range(GATE_PIECES)], axis=1)
            return rows, (d, gla_ref[0, rows, 0:GLA_QK], kt_ref[0, :, rows], gla_ref[0, rows, o_v:o_v + GLA_V],
                          pieces(rows, d), gt3)

        def step(i, carry):
            chunks = [(lax.rem(i * per_step + u + n_lat, n_all), 0) for u in range(per_step)]
            chunks += [(n_all - 1 - (i * per_step + u), 1) for u in range(per_step)]
            rows, jobs = zip(*[job(ci, d) for ci, d in chunks])
            outs, states = _gla_fast_chunks(jobs, a3_ref, a3t_ref, mw_ref, bdc_ref[...], carry)
            for (_, d), r, o in zip(chunks, rows, outs):
                (ob_ref if d else of_ref)[r, :] = o
            return tuple(states)

        zero = jnp.zeros((GLA_QK, 128), F32)
        lax.fori_loop(0, n_all // per_step, step, (zero, zero))

    @pl.when(jnp.logical_not(safe))
    def _robust():
        c = GLA_CHUNK
        n_lat = t_lat // c
        n_all = t_all // c
        hq = hq_ref[...]
        bdv = bdv_ref[...]
        bds = bds_ref[...]

        def one(ci, d, s):
            start = pl.multiple_of(ci * c, c)
            rows = pl.ds(start, c)
            q = gla_ref[0, rows, 0:GLA_QK].astype(F32)
            k = gla_ref[0, rows, GLA_QK:2 * GLA_QK].astype(F32)
            v = gla_ref[0, rows, o_v:o_v + GLA_V]
            o, s = _gla_chunk(q, k, v, pieces(rows, d), amat_ref[d],
                              [mask_ref[d, l] for l in range(len(GLA_LEVELS) + 1)], hq, bdv, bds, s,
                              backward=bool(d))
            (ob_ref if d else of_ref)[rows, :] = o
            return s

        def step(i, carry):
            s_f, s_b = carry
            return one(lax.rem(i + n_lat, n_all), 0, s_f), one(n_all - 1 - i, 1, s_b)

        zero = jnp.zeros((GLA_QK, GLA_V), F32)
        lax.fori_loop(0, n_all, step, (zero, zero))

    tr = 256
    ind = ind_ref[...]
    for r0 in range(0, t_out, tr):
        o = of_ref[r0:r0 + tr, :] + ob_ref[r0:r0 + tr, :]
        sq = o * o
        hi = sq.astype(BF16)
        lo = (sq - hi.astype(F32)).astype(BF16)
        ms = jnp.dot(hi, ind, preferred_element_type=F32) + jnp.dot(lo, ind, preferred_element_type=F32)
        gate = gla_ref[0, r0:r0 + tr, o_v + GLA_V:COL_GLA].astype(F32)
        y = o * lax.rsqrt(ms + 1e-6) * norm_ref[...] * _silu(gate)
        out_ref[0, r0:r0 + tr, :] = y.astype(BF16)


def _gla(gla, lg3, kt, lgt3, consts, norm_t, *, t_lat, t_out):
    bsz, t_all, _ = gla.shape
    n2 = 2 * GLA_QK
    return pl.pallas_call(
        functools.partial(_gla_kernel, t_lat=t_lat, t_all=t_all, t_out=t_out),
        grid=(bsz,),
        in_specs=[pl.BlockSpec((1, t_all, COL_GLA), lambda b: (b, 0, 0)),
                  pl.BlockSpec((1, t_all, GATE_PIECES * n2), lambda b: (b, 0, 0)),
                  pl.BlockSpec((1, GLA_QK, t_all), lambda b: (b, 0, 0)),
                  pl.BlockSpec((1, GATE_PIECES * n2, t_all), lambda b: (b, 0, 0))] +
                 [_const_spec(a.shape) for a in consts] + [_const_spec(norm_t.shape)],
        out_specs=pl.BlockSpec((1, t_out, GLA_V), lambda b: (b, 0, 0)),
        out_shape=jax.ShapeDtypeStruct((bsz, t_out, GLA_V), BF16),
        scratch_shapes=[pltpu.VMEM((t_all, GLA_V), F32), pltpu.VMEM((t_all, GLA_V), F32)],
        compiler_params=_cparams("parallel"),
        name="gla",
    )(gla, lg3, kt, lgt3, *consts, norm_t)


def _gmlp_kernel(z_ref, lng_ref, lnb_ref, ws_ref, bs_ref, gmask_ref, og_ref, out_ref, *, tm):
    z = z_ref[0].astype(F32)
    zf = 0.5 * z * (1.0 + lax.erf(z * (2.0 ** -0.5)))
    u = zf[:, 0:GMLP_WIDTH]
    v = zf[:, GMLP_WIDTH:]
    mu = jnp.mean(v, axis=-1, keepdims=True)
    vc = v - mu
    v = vc * lax.rsqrt(jnp.mean(vc * vc, axis=-1, keepdims=True) + 1e-5) * lng_ref[...] + lnb_ref[...]
    vb = v.astype(BF16)
    ws = ws_ref[...]
    p = GMLP_CHUNK
    for ci in range(tm // p):
        r = jnp.dot(ws, vb[ci * p:(ci + 1) * p], preferred_element_type=F32) * gmask_ref[...]
        mixed = bs_ref[...]
        for g in range(GMLP_GROUPS):
            mixed = mixed + r[g * p:(g + 1) * p]
        y = u[ci * p:(ci + 1) * p] * mixed
        out_ref[0, ci * p:(ci + 1) * p, :] = _rms(y, og_ref[...]).astype(BF16)


def _gmlp(gm, ln_g, ln_b, ws_s, bs_t, gmask, out_g, *, t_out, tm):
    bsz = gm.shape[0]
    return pl.pallas_call(
        functools.partial(_gmlp_kernel, tm=tm),
        grid=(bsz, t_out // tm),
        in_specs=[pl.BlockSpec((1, tm, COL_GM), lambda b, j: (b, j, 0)),
                  _const_spec(ln_g.shape), _const_spec(ln_b.shape), _const_spec(ws_s.shape),
                  _const_spec(bs_t.shape), _const_spec(gmask.shape), _const_spec(out_g.shape)],
        out_specs=pl.BlockSpec((1, tm, GMLP_WIDTH), lambda b, j: (b, j, 0)),
        out_shape=jax.ShapeDtypeStruct((bsz, t_out, GMLP_WIDTH), BF16),
        compiler_params=_cparams("parallel", "parallel"),
        name="gmlp",
    )(gm, ln_g, ln_b, ws_s, bs_t, gmask, out_g)


def _swa_attend(qt_ref, k_ref, vt_ref, sink_ref, og_ref, eye_ref, out_ref, s_ref, p_ref, u, key_rows, biases):
    w = SWA_WINDOW
    d = HEAD_DIM
    kt = SWA_KEY_TILE
    ph = SWA_PART_HEADS
    pw = ph * w
    parts = range(SWA_HEADS // ph)
    group = lambda part: part * ph // SWA_REP
    s_ref = s_ref.at[u]
    p_ref = p_ref.at[u]
    cols = slice(u * w, (u + 1) * w)
    keys = jnp.concatenate([k_ref[0, pl.ds(start, size), :] for start, size in key_rows], axis=0)
    nk = keys.shape[0]

    zero = jnp.zeros((d, pw), BF16)
    sinks, tops = [], []
    for part in parts:
        q = jnp.concatenate([qt_ref[0, h * d:(h + 1) * d, cols] for h in range(part * ph, (part + 1) * ph)], axis=1)
        qbd = jnp.concatenate([q if g == group(part) else zero for g in range(SWA_KV_HEADS)], axis=0)
        best, off = None, 0
        for (_, size), bias in zip(key_rows, biases):
            blk = jnp.dot(keys[off:off + size], qbd, preferred_element_type=F32)
            if bias is not None:
                blk = blk + jnp.concatenate([bias] * ph, axis=1)
            s_ref[part, off:off + size, :] = blk
            m8 = jnp.max(blk.reshape(size // 8, 8, pw), axis=0)
            best = m8 if best is None else jnp.maximum(best, m8)
            off += size
        sink = sink_ref[:, part * pw:(part + 1) * pw] * LOG2E
        sinks.append(sink)
        tops.append(jnp.maximum(jnp.max(best, axis=0, keepdims=True), sink))
    yield

    for part in parts:
        for r0 in range(0, nk, kt):
            p_ref[part, r0:r0 + kt, :] = jnp.exp2(s_ref[part, r0:r0 + kt, :] - tops[part]).astype(BF16)
    yield

    ones = jnp.ones((SWA_DEN_ROWS, nk), BF16)
    vts = [jnp.concatenate([vt_ref[0, g * d:(g + 1) * d, pl.ds(start, size)] for start, size in key_rows] , axis=1)
           for g in range(SWA_KV_HEADS)]
    heads = []
    for part in parts:
        o_ext = jnp.dot(jnp.concatenate([vts[group(part)], ones], axis=0), p_ref[part, 0:nk, :],
                        preferred_element_type=F32)
        scaled = o_ext[0:d] / (o_ext[d:d + 1] + jnp.exp2(sinks[part] - tops[part]))
        heads += [scaled[:, r * w:(r + 1) * w] for r in range(ph)]
    o_t = jnp.concatenate(heads, axis=0)
    yield

    y_t = o_t * lax.rsqrt(jnp.mean(o_t * o_t, axis=0, keepdims=True) + 1e-6) * og_ref[...]
    out = lax.dot_general(eye_ref[...], y_t.astype(BF16), (((1,), (1,)), ((), ())), preferred_element_type=F32)
    out_ref[0, cols, :] = out.astype(BF16)


def _run_stages(blocks):
    for t in range(len(blocks) + 2):
        for u in (t, t - 1, t - 2, t - 1):
            if 0 <= u < len(blocks):
                next(blocks[u], None)


def _swa_kernel(qt_ref, k_ref, vt_ref, sink_ref, og_ref, eye_ref, out_ref, s_ref, p_ref, *, t_lat, t_all, t_out, sb):
    w = SWA_WINDOW
    step = pl.program_id(1)
    n_lat = t_lat // w
    ctx_rows = (t_lat, t_all - t_lat)
    attend = functools.partial(_swa_attend, qt_ref, k_ref, vt_ref, sink_ref, og_ref, eye_ref, out_ref, s_ref, p_ref)

    def run(latent):
        blk = lambda i: (pl.multiple_of(i * w, w), w)
        sk = lax.broadcasted_iota(jnp.int32, (w, w), 0)
        tq = lax.broadcasted_iota(jnp.int32, (w, w), 1)
        neg = jnp.full((w, w), -jnp.inf, F32)
        zero = jnp.zeros((w, w), F32)
        blocks = []
        for u, is_latent in enumerate(latent):
            if not is_latent:
                blocks.append(attend(u, [ctx_rows], [None]))
                continue
            n = step * sb + u
            b_prev = jnp.where((sk >= tq) & (n >= 1), zero, neg)
            b_next = jnp.where((sk <= tq) & (n < n_lat - 1), zero, neg)
            blocks.append(attend(u, [ctx_rows, blk(jnp.maximum(n - 1, 0)), blk(n), blk(jnp.minimum(n + 1, n_lat - 1))],
                                 [None, b_prev, None, b_next]))
        _run_stages(blocks)

    n_steps = t_out // (sb * w)
    makeup = [tuple(s * sb + u < n_lat for u in range(sb)) for s in range(n_steps)]
    for kind in sorted(set(makeup), reverse=True):
        steps = [s for s in range(n_steps) if makeup[s] == kind]
        assert steps == list(range(steps[0], steps[-1] + 1))
        pl.when((step >= steps[0]) & (step <= steps[-1]))(functools.partial(run, kind))


def _swa(sqt, sk, svt, sink_t, out_g, eye, *, t_lat, t_out, sb):
    bsz, t_all, _ = sk.shape
    w = SWA_WINDOW
    n_keys = t_all - t_lat + 3 * w
    n_parts = SWA_HEADS // SWA_PART_HEADS
    assert t_out % (sb * w) == 0
    return pl.pallas_call(
        functools.partial(_swa_kernel, t_lat=t_lat, t_all=t_all, t_out=t_out, sb=sb),
        grid=(bsz, t_out // (sb * w)),
        in_specs=[pl.BlockSpec((1, SWA_Q, sb * w), lambda b, n: (b, 0, n)),
                  pl.BlockSpec((1, t_all, SWA_KV), lambda b, n: (b, 0, 0)),
                  pl.BlockSpec((1, SWA_KV, t_all), lambda b, n: (b, 0, 0)),
                  _const_spec(sink_t.shape), _const_spec(out_g.shape), _const_spec(eye.shape)],
        out_specs=pl.BlockSpec((1, sb * w, SWA_Q), lambda b, n: (b, n, 0)),
        out_shape=jax.ShapeDtypeStruct((bsz, t_out, SWA_Q), BF16),
        scratch_shapes=[pltpu.VMEM((sb, n_parts, n_keys, SWA_PART_HEADS * w), F32),
                        pltpu.VMEM((sb, n_parts, n_keys, SWA_PART_HEADS * w), BF16)],
        compiler_params=_cparams("parallel", "parallel"),
        name="swa",
    )(sqt, sk, svt, sink_t, out_g, eye)


FFN_CHUNKS = ((0, 1536), (1536, 1280))


def _post_kernel(*refs, t_lat, tm, ctx_row, n_stream, ctx_separate):
    a_ref, b_ref, c_ref, mod_ref, n1_ref, n2a_ref, n2b_ref, wo_ref, wgu_ref, wd_ref, out_ref = refs[n_stream:]
    b = pl.program_id(0)
    j = pl.program_id(1)
    x = _stream_tile(refs[:n_stream], j, tm=tm, t_lat=t_lat, ctx_separate=ctx_separate)
    mod = functools.partial(_mod_vectors, mod_ref, b, j, t_lat=t_lat, tm=tm, ctx_row=ctx_row)
    split = _row_split(t_lat, tm)
    gate1, shift2, scale2, gate2 = mod(2), mod(3), mod(4), mod(5)
    g1 = [n1_ref[...] * g for g in gate1]
    g2a = [n2a_ref[...] * (1.0 + s) for s in scale2]
    g2b = [n2b_ref[...] * g for g in gate2]

    cat = jnp.concatenate([a_ref[0], b_ref[0], c_ref[0]], axis=-1)
    y = jnp.dot(cat, wo_ref[...], preferred_element_type=F32)
    x1 = _by_rows(lambda p, x, y: x + _rms(y, g1[p]), split, x, y)
    h = _by_rows(lambda p, x: (_rms(x, g2a[p]) + shift2[p]).astype(BF16), split, x1)
    f = None
    for off, n in FFN_CHUNKS:
        gt = jnp.dot(h, wgu_ref[:, off:off + n], preferred_element_type=F32)
        up = jnp.dot(h, wgu_ref[:, D_FF + off:D_FF + off + n], preferred_element_type=F32)
        act = (_silu(gt) * up).astype(BF16)
        part = jnp.dot(act, wd_ref[off:off + n, :], preferred_element_type=F32)
        f = part if f is None else f + part
    out_ref[0] = _by_rows(lambda p, x, f: x + _rms(f, g2b[p]), split, x1, f)


def _post(a, bo, c, x, ctx, mod, n1_post, n2_pre, n2_post, wo, wgu, wd, *, t_lat, t_out, tm, ctx_row):
    bsz = x.shape[0]
    stream, stream_specs = _stream_specs(x, ctx, tm)
    tok = lambda n: pl.BlockSpec((1, tm, n), lambda b, j: (b, j, 0))
    once = lambda arr: pl.BlockSpec(arr.shape, lambda b, j: (0,) * arr.ndim, pipeline_mode=pl.Buffered(1))
    return pl.pallas_call(
        functools.partial(_post_kernel, t_lat=t_lat, tm=tm, ctx_row=ctx_row, n_stream=len(stream),
                          ctx_separate=ctx is not None),
        grid=(bsz, t_out // tm),
        in_specs=stream_specs + [tok(GLA_V), tok(GMLP_WIDTH), tok(SWA_Q), _const_spec(mod.shape),
                                 _const_spec(n1_post.shape), _const_spec(n2_pre.shape), _const_spec(n2_post.shape),
                                 once(wo), once(wgu), once(wd)],
        out_specs=tok(D_MODEL),
        out_shape=jax.ShapeDtypeStruct((bsz, t_out, D_MODEL), F32),
        compiler_params=_cparams("parallel", "parallel"),
        name="post",
    )(*stream, a, bo, c, mod, n1_post, n2_pre, n2_post, wo, wgu, wd)


def _rope_tables(t_lat, t_ctx):
    rows = t_lat // GRID_W
    row = jnp.repeat(jnp.arange(rows), GRID_W).astype(F32)
    col = jnp.tile(jnp.arange(GRID_W), rows).astype(F32)
    inv_freq = jnp.power(ROPE_THETA, -jnp.arange(0, ROPE_AXIS_DIM, 2, dtype=F32) / ROPE_AXIS_DIM)
    ang_row = row[:, None] * inv_freq[None, :]
    ang_col = col[:, None] * inv_freq[None, :]
    ang = jnp.concatenate([ang_row, ang_row, ang_col, ang_col], axis=-1)
    sign = jnp.tile(jnp.concatenate([-jnp.ones((16,), F32), jnp.ones((16,), F32)]), 2)
    cos = jnp.tile(jnp.concatenate([jnp.cos(ang), jnp.ones((t_ctx, HEAD_DIM), F32)], axis=0), (1, 2))
    sin = jnp.tile(jnp.concatenate([jnp.sin(ang) * sign, jnp.zeros((t_ctx, HEAD_DIM), F32)], axis=0), (1, 2))
    return cos, sin, cos.T, sin.T


def _inproj_weights(w, wa2, ba):
    r = GLA_GATE_RANK
    o_code = 2 * GLA_QK + 2 * GLA_V
    o_gm = o_code + 2 * r
    o_sq = o_gm + 2 * GMLP_WIDTH
    o_sk = o_sq + SWA_Q
    o_sv = o_sk + SWA_KV
    code = jnp.concatenate([w[:, o_code:o_gm], jnp.zeros((w.shape[0], COL_CODE - 2 * r), w.dtype)], axis=1)
    w_tok = jnp.concatenate([w[:, 0:o_code], w[:, o_gm:o_sq], w[:, o_sk:o_sv], code], axis=1)
    w_t = jnp.concatenate([w[:, o_sq:o_sk], w[:, o_sv:], w[:, GLA_QK:2 * GLA_QK]], axis=1).T
    wa = jnp.zeros((COL_CODE, 2 * GLA_QK), F32)
    wa = wa.at[0:r, 0:GLA_QK].set(wa2[0]).at[r:2 * r, GLA_QK:].set(wa2[1])
    return w_tok.astype(BF16), w_t.astype(BF16), wa.astype(BF16), ba.reshape(1, 2 * GLA_QK)


def _gla_consts():
    a3, a3t_neg, mask_w, bd_c = _gla_fast_tables()
    amat, mask = _gla_tables()
    lane_head = np.arange(GLA_QK)[None, :] // GLA_DK
    row_head = np.arange(GLA_HEADS * GLA_CHUNK)[:, None] // GLA_CHUNK
    hq = (lane_head == row_head).astype(np.float32)
    vlane_head = np.arange(GLA_V)[None, :] // GLA_DV
    bdv = (vlane_head == row_head).astype(np.float32)
    bds = (vlane_head == (np.arange(GLA_QK)[:, None] // GLA_DK)).astype(np.float32)
    ind = (vlane_head == vlane_head.T).astype(np.float32) / GLA_DV
    return (jnp.asarray(a3, BF16), jnp.asarray(a3t_neg, BF16), jnp.asarray(mask_w, BF16),
            jnp.asarray(bd_c, F32), jnp.asarray(amat, BF16), jnp.asarray(mask, F32), jnp.asarray(hq, BF16),
            jnp.asarray(bdv, F32), jnp.asarray(bds, F32), jnp.asarray(ind, BF16))


def kernel(x, c, ctx, c_ctx, mod_w, mod_b, n1_pre, n1_post, n2_pre, n2_post, w_in, w_out, gla_wa2, gla_ba,
           gla_norm, gmlp_ln_g, gmlp_ln_b, gmlp_ws, gmlp_bs, gmlp_out_g, swa_sink, swa_out_g, ffn_w_gu,
           ffn_w_down):
    bsz, t_lat, d = x.shape
    t_ctx = ctx.shape[1]
    t_all = t_lat + t_ctx
    depth = mod_w.shape[0]
    assert d == D_MODEL and bsz < MOD_ROWS
    assert t_lat % 1024 == 0 and t_ctx % 256 == 0 and t_all % 768 == 0
    ctx_row = bsz
    tm_all = 768
    assert t_ctx == tm_all - _row_split(t_lat, tm_all)

    cc = jnp.zeros((MOD_ROWS, d), F32).at[0:bsz].set(c).at[ctx_row].set(c_ctx)
    mods = _modulation(cc, mod_w, mod_b)
    rope = _rope_tables(t_lat, t_ctx)
    gla_consts = _gla_consts()
    eye = jnp.eye(SWA_WINDOW, dtype=BF16)
    gmask = jnp.asarray((np.arange(GMLP_WIDTH)[None, :] // GMLP_GDIM ==
                         np.arange(GMLP_GROUPS * GMLP_CHUNK)[:, None] // GMLP_CHUNK).astype(np.float32))
    row = lambda v: v.reshape(1, -1)

    assert t_ctx == STREAM_PIECE
    xs, xs_ctx = x, ctx
    for l in range(depth):
        last = l == depth - 1
        t_out = t_lat if last else t_all
        gla, lg3, gm, sk, sqt, svt, kt, lgt3 = _inproj(
            xs, xs_ctx, mods[l], row(n1_pre[l]), _inproj_weights(w_in[l], gla_wa2[l], gla_ba[l]), rope,
            t_lat=t_lat, t_all=t_all, tm=tm_all, ctx_row=ctx_row)
        a_out = _gla(gla, lg3, kt, lgt3, gla_consts, row(jnp.tile(gla_norm[l], GLA_HEADS)),
                     t_lat=t_lat, t_out=t_out)
        ws_s = gmlp_ws[l].reshape(GMLP_GROUPS * GMLP_CHUNK, GMLP_CHUNK).astype(BF16)
        bs_t = jnp.repeat(gmlp_bs[l].T, GMLP_GDIM, axis=1)
        b_out = _gmlp(gm, row(gmlp_ln_g[l]), row(gmlp_ln_b[l]), ws_s, bs_t, gmask, row(gmlp_out_g[l]),
                      t_out=t_out, tm=1024 if last else tm_all)
        sink_t = row(jnp.repeat(swa_sink[l], SWA_WINDOW))
        out_g_t = jnp.broadcast_to(swa_out_g[l][:, None], (SWA_Q, SWA_WINDOW))
        sb = next(n for n in SWA_STEP_BLOCKS if (t_out // SWA_WINDOW) % n == 0)
        c_out = _swa(sqt, sk, svt, sink_t, out_g_t, eye, t_lat=t_lat, t_out=t_out, sb=sb)
        xs = _post(a_out, b_out, c_out, xs, xs_ctx, mods[l], row(n1_post[l]), row(n2_pre[l]), row(n2_post[l]),
                   w_out[l].astype(BF16), ffn_w_gu[l].astype(BF16), ffn_w_down[l].astype(BF16),
                   t_lat=t_lat, t_out=t_out, tm=1024 if last else tm_all, ctx_row=ctx_row)
        xs_ctx = None
    return xs
```

```python
import functools

import numpy as np
import jax
import jax.numpy as jnp
from jax import lax
from jax.experimental import pallas as pl
from jax.experimental.pallas import tpu as pltpu

F32 = jnp.float32
BF16 = jnp.bfloat16

D_MODEL = 1024
GRID_W = 64
HEAD_DIM = 64
GLA_HEADS = 4
GLA_DK = 32
GLA_DV = 64
GLA_QK = GLA_HEADS * GLA_DK
GLA_V = GLA_HEADS * GLA_DV
GLA_GATE_RANK = 16
GLA_GATE_TAU = 16.0
GLA_CHUNK = 64
GMLP_GROUPS = 4
GMLP_GDIM = 64
GMLP_WIDTH = GMLP_GROUPS * GMLP_GDIM
GMLP_CHUNK = 128
SWA_HEADS = 8
SWA_KV_HEADS = 2
SWA_REP = SWA_HEADS // SWA_KV_HEADS
SWA_Q = SWA_HEADS * HEAD_DIM
SWA_KV = SWA_KV_HEADS * HEAD_DIM
SWA_WINDOW = 128
ROPE_AXIS_DIM = HEAD_DIM // 2
ROPE_THETA = 10000.0
MIX_WIDTH = GLA_V + GMLP_WIDTH + SWA_Q
D_FF = -(-8 * D_MODEL // (3 * 256)) * 256
N_MOD = 6
MOD_ROWS = 24

COL_GLA = 2 * GLA_QK + 2 * GLA_V
COL_GM = 2 * GMLP_WIDTH
COL_SK = SWA_KV
COL_CODE = 128
OFF_GM = COL_GLA
OFF_SK = OFF_GM + COL_GM
OFF_CODE = OFF_SK + COL_SK
ROW_SQ = 0
ROW_SV = ROW_SQ + SWA_Q
ROW_GK = ROW_SV + SWA_KV

LOG2E = 1.4426950408889634
SWA_Q_SCALE = HEAD_DIM ** -0.5 * LOG2E
SWA_KEY_TILE = 32
SWA_PART_HEADS = 2
SWA_STEP_BLOCKS = (8, 6, 4, 3)
SWA_DEN_ROWS = 16

VMEM_LIMIT = 56 * 1024 * 1024
GLA_LEVELS = (1, 2, 4, 8, 16, 32)


def _cparams(*sem):
    return pltpu.CompilerParams(dimension_semantics=sem, vmem_limit_bytes=VMEM_LIMIT)


def _const_spec(shape):
    nd = len(shape)
    return pl.BlockSpec(shape, lambda *_: (0,) * nd)


def _rms(x, g, eps=1e-6):
    return x * lax.rsqrt(jnp.mean(x * x, axis=-1, keepdims=True) + eps) * g


def _silu(x):
    return x * jax.nn.sigmoid(x)


def _row_split(t_lat, tm):
    return t_lat % tm if t_lat % tm else tm


def _mod_vectors(mod_ref, b, j, idx, *, t_lat, tm, ctx_row):
    lo = idx * D_MODEL
    lat = mod_ref[pl.ds(b, 1), lo:lo + D_MODEL]
    ctx = mod_ref[ctx_row:ctx_row + 1, lo:lo + D_MODEL]
    return lat, jnp.where(j * tm + _row_split(t_lat, tm) >= t_lat, ctx, lat)


STREAM_PIECE = 256


def _stream_specs(x, ctx, tm, tile=lambda b, j: (b, j), **spec_args):
    rows, d = x.shape[1:]
    per_tile = tm // STREAM_PIECE
    last = rows // STREAM_PIECE - 1

    def piece(k):
        def index(b, j):
            tb, tj = tile(b, j)
            return tb, jnp.minimum(tj * per_tile + k, last), 0
        return pl.BlockSpec((1, STREAM_PIECE, d), index, **spec_args)

    operands, specs = [x] * per_tile, [piece(k) for k in range(per_tile)]
    if ctx is not None:
        assert ctx.shape[1] == STREAM_PIECE
        operands.append(ctx)
        specs.append(pl.BlockSpec((1, STREAM_PIECE, d), lambda b, j: (tile(b, j)[0], 0, 0)))
    return operands, specs


def _stream_tile(refs, j, *, tm, t_lat, ctx_separate):
    pieces = [r[0] for r in refs[:tm // STREAM_PIECE]]
    if ctx_separate:
        pieces[-1] = jnp.where((j + 1) * tm - STREAM_PIECE >= t_lat, refs[-1][0], pieces[-1])
    return jnp.concatenate(pieces, axis=0)


def _by_rows(fn, split, *arrays):
    tm = arrays[0].shape[0]
    if split == tm:
        return fn(0, *arrays)
    return jnp.concatenate([fn(0, *[a[0:split] for a in arrays]), fn(1, *[a[split:tm] for a in arrays])], axis=0)


def _mod_kernel(c_ref, w_ref, b_ref, o_ref):
    s = _silu(c_ref[...]).astype(BF16)
    o_ref[0] = jnp.dot(s, w_ref[0].astype(BF16), preferred_element_type=F32) + b_ref[0]


def _modulation(cc, mod_w, mod_b):
    depth, d, n = mod_w.shape
    tn = n // 4
    return pl.pallas_call(
        _mod_kernel,
        grid=(depth, n // tn),
        in_specs=[pl.BlockSpec((MOD_ROWS, d), lambda l, j: (0, 0)),
                  pl.BlockSpec((1, d, tn), lambda l, j: (l, 0, j)),
                  pl.BlockSpec((1, 1, tn), lambda l, j: (l, 0, j))],
        out_specs=pl.BlockSpec((1, MOD_ROWS, tn), lambda l, j: (l, 0, j)),
        out_shape=jax.ShapeDtypeStruct((depth, MOD_ROWS, n), F32),
        compiler_params=_cparams("parallel", "parallel"),
        name="modulation",
    )(cc, mod_w, mod_b.reshape(depth, 1, n))


def _rope(z, cos, sin):
    lane = lax.broadcasted_iota(jnp.int32, z.shape, 1)
    first = (lane % 32) < 16
    rot = jnp.where(first, pltpu.roll(z, 128 - 16, 1), pltpu.roll(z, 16, 1))
    return z * cos + rot * sin


GATE_PIECES = 2


def _split(x):
    p1 = x.astype(BF16)
    return p1, (x - p1.astype(F32)).astype(BF16)


def _log_gate(z):
    return (jnp.minimum(z, 0.0) - jnp.log(1.0 + jnp.exp(-jnp.abs(z)))) * (1.0 / GLA_GATE_TAU)


def _rope_t(z, cos_t, sin_t):
    parts = []
    for r0 in range(0, z.shape[0], 32):
        parts += [z[r0 + 16:r0 + 32], z[r0:r0 + 16]]
    rot = jnp.concatenate(parts, axis=0)
    reps = z.shape[0] // cos_t.shape[0]
    return z * jnp.concatenate([cos_t] * reps, axis=0) + rot * jnp.concatenate([sin_t] * reps, axis=0)


def _inproj_kernel(*refs, t_lat, tm, ctx_row, n_stream, ctx_separate):
    (mod_ref, g_ref, w_ref, wt_ref, wa_ref, ba_ref, cos_ref, sin_ref, cost_ref, sint_ref,
     gla_ref, lg_ref, gm_ref, sk_ref, sqt_ref, svt_ref, kt_ref, lgt_ref, h_buf, ht_buf) = refs[2 * n_stream:]
    b = pl.program_id(0)
    j = pl.program_id(1)
    nt = pl.num_programs(1)
    step = b * nt + j
    slot = lax.rem(step, 2)

    def prologue(stream_refs, tb, tj, dst):
        x = _stream_tile(stream_refs, tj, tm=tm, t_lat=t_lat, ctx_separate=ctx_separate)
        mod = functools.partial(_mod_vectors, mod_ref, tb, tj, t_lat=t_lat, tm=tm, ctx_row=ctx_row)
        shift, scale, gain = mod(0), mod(1), g_ref[...]
        gains = [gain * (1.0 + s) for s in scale]
        hf = _by_rows(lambda part, x: _rms(x, gains[part]) + shift[part], _row_split(t_lat, tm), x)
        h_buf[dst] = hf.astype(BF16)
        ht_buf[dst] = hf.T.astype(BF16)

    @pl.when(step == 0)
    def _first():
        prologue(refs[n_stream:2 * n_stream], 0, 0, 0)

    def project(cur):
        def proj(off, n):
            return jnp.dot(h_buf[cur], w_ref[:, off:off + n], preferred_element_type=F32)

        def proj_t(off, n):
            return jnp.dot(wt_ref[off:off + n, :], ht_buf[cur], preferred_element_type=F32)

        code = proj(OFF_CODE, COL_CODE).astype(BF16)
        z = proj(0, COL_GLA)
        gla_ref[0, :, 0:GLA_QK] = (z[:, 0:GLA_QK] * (GLA_DK ** -0.5)).astype(BF16)
        gla_ref[0, :, GLA_QK:COL_GLA] = z[:, GLA_QK:COL_GLA].astype(BF16)

        n2 = 2 * GLA_QK
        lg = _log_gate(jnp.dot(code, wa_ref[...], preferred_element_type=F32) + ba_ref[...])
        for i, p in enumerate(_split(lg)):
            lg_ref[0, :, i * n2:(i + 1) * n2] = p
        for i, p in enumerate(_split(lg.T)):
            lgt_ref[0, i * n2:(i + 1) * n2, :] = p

        gm_ref[0] = proj(OFF_GM, COL_GM).astype(BF16)
        nb, nj = _next_tile(b, j, pl.num_programs(0), nt)
        prologue(refs[:n_stream], nb, nj, 1 - cur)
        kt_ref[0] = proj_t(ROW_GK, GLA_QK).astype(BF16)

        sk_ref[0] = _rope(proj(OFF_SK, COL_SK), cos_ref[...], sin_ref[...]).astype(BF16)
        sqt_ref[0] = (_rope_t(proj_t(ROW_SQ, SWA_Q), cost_ref[...], sint_ref[...]) * SWA_Q_SCALE).astype(BF16)
        svt_ref[0] = proj_t(ROW_SV, SWA_KV).astype(BF16)

    for cur in range(2):
        pl.when(slot == cur)(functools.partial(project, cur))


def _next_tile(b, j, nb, nt):
    wrap = j + 1 == nt
    last = jnp.logical_and(wrap, b + 1 == nb)
    return jnp.where(wrap & jnp.logical_not(last), b + 1, b), jnp.where(last, j, jnp.where(wrap, 0, j + 1))


def _inproj(x, ctx, mod, n_pre, weights, rope, *, t_lat, t_all, tm, ctx_row):
    bsz, _, d = x.shape
    nt = t_all // tm
    nxt, nxt_specs = _stream_specs(x, ctx, tm, functools.partial(_next_tile, nb=bsz, nt=nt))
    first, first_specs = _stream_specs(x, ctx, tm, lambda b, j: (0, 0))
    stream, stream_specs = nxt + first, nxt_specs + first_specs
    tok = lambda n: pl.BlockSpec((1, tm, n), lambda b, j: (b, j, 0))
    tok_t = lambda n: pl.BlockSpec((1, n, tm), lambda b, j: (b, 0, j))
    n2 = 2 * GLA_QK
    cols = (COL_GLA, GATE_PIECES * n2, COL_GM, COL_SK)
    rows = (SWA_Q, SWA_KV, GLA_QK, GATE_PIECES * n2)
    out_shape = [jax.ShapeDtypeStruct((bsz, t_all, n), BF16) for n in cols]
    out_shape += [jax.ShapeDtypeStruct((bsz, n, t_all), BF16) for n in rows]
    consts = (mod, n_pre) + tuple(weights)
    cos, sin, cos_t, sin_t = rope
    return pl.pallas_call(
        functools.partial(_inproj_kernel, t_lat=t_lat, tm=tm, ctx_row=ctx_row, n_stream=len(nxt),
                          ctx_separate=ctx is not None),
        grid=(bsz, nt),
        in_specs=stream_specs + [_const_spec(a.shape) for a in consts] +
                 [pl.BlockSpec((tm, 128), lambda b, j: (j, 0)), pl.BlockSpec((tm, 128), lambda b, j: (j, 0)),
                  pl.BlockSpec((128, tm), lambda b, j: (0, j)), pl.BlockSpec((128, tm), lambda b, j: (0, j))],
        out_specs=[tok(n) for n in cols] + [tok_t(n) for n in rows],
        out_shape=out_shape,
        scratch_shapes=[pltpu.VMEM((2, tm, d), BF16), pltpu.VMEM((2, d, tm), BF16)],
        compiler_params=_cparams("arbitrary", "arbitrary"),
        name="inproj",
    )(*stream, *consts, cos, sin, cos_t, sin_t)


GLA_FAST_CHUNK = 128
GLA_FAST_UNROLL = 3
GLA_SAFE_LOG_DECAY = -40.0


def _gla_fast_tables():
    c = GLA_FAST_CHUNK
    t = np.arange(c)[:, None]
    r = np.arange(c)[None, :]
    incl = [(r <= t), (r >= t)]
    a3 = np.stack([np.tile(m, (1, GATE_PIECES)) for m in incl]).astype(np.float32)
    a3t_neg = -np.stack([np.tile(m.T, (GATE_PIECES, 1)) for m in incl]).astype(np.float32)
    mask_w = np.stack([np.tile(m, (1, GLA_HEADS)) for m in incl]).astype(np.float32)
    half = (np.arange(128)[None, :] // GLA_DV) == ((np.arange(GLA_QK)[:, None] // GLA_DK) % 2)
    return a3, a3t_neg, mask_w, half.astype(np.float32)


def _gla_fast_chunks(jobs, a3, a3t_neg, mask_w, bd_c, states):
    c = GLA_FAST_CHUNK
    dk, hd = GLA_DK, GLA_QK
    cums = [(jnp.dot(a3[d], g3, preferred_element_type=F32),
             jnp.dot(gt3, a3t_neg[d], preferred_element_type=F32))
            for d, _, _, _, g3, gt3 in jobs]

    zr = lambda n: jnp.zeros((n, c), BF16)
    qes, dcols, khs, atts = [], [], [], []
    for (d, q, kt, _, _, _), (cum, ncum_t) in zip(jobs, cums):
        qe = (q.astype(F32) * jnp.exp(cum)).astype(BF16)
        ke_t = kt.astype(F32) * jnp.exp(ncum_t)
        edge = 0 if d else c - 1
        dcol = jnp.exp(-ncum_t[:, edge:edge + 1])
        khs.append((ke_t * dcol).astype(BF16))
        ke_t = ke_t.astype(BF16)
        cols = []
        for hh in range(GLA_HEADS):
            parts = ([zr(hh * dk)] if hh else []) + [ke_t[hh * dk:(hh + 1) * dk]]
            parts += [zr(hd - (hh + 1) * dk)] if hh < GLA_HEADS - 1 else []
            cols.append(jnp.concatenate(parts, axis=0))
        kbd = jnp.concatenate(cols, axis=1)
        atts.append(jnp.dot(qe, kbd, preferred_element_type=F32))
        qes.append(qe)
        dcols.append(dcol)

    kvs = [jnp.concatenate([jnp.dot(kh[0:hd // 2], v[:, 0:128], preferred_element_type=F32),
                            jnp.dot(kh[hd // 2:hd], v[:, 128:256], preferred_element_type=F32)], axis=0)
           for (_, _, _, v, _, _), kh in zip(jobs, khs)]

    lane = lax.broadcasted_iota(jnp.int32, (1, 128), 1)
    m_lo = (lane < GLA_DV).astype(BF16)
    m_hi = (lane >= GLA_DV).astype(BF16)
    z128 = jnp.zeros((c, 128), BF16)
    z64 = jnp.zeros((hd // 2, 128), BF16)
    states = list(states)
    outs = []
    for (d, _, _, v, _, _), qe, dcol, att, kv in zip(jobs, qes, dcols, atts, kvs):
        v_lo, v_hi = v[:, 0:128], v[:, 128:256]
        vbd = jnp.concatenate([jnp.concatenate([v_lo * m_lo, z128], axis=1),
                               jnp.concatenate([v_lo * m_hi, z128], axis=1),
                               jnp.concatenate([z128, v_hi * m_lo], axis=1),
                               jnp.concatenate([z128, v_hi * m_hi], axis=1)], axis=0)
        sb = states[d].astype(BF16)
        s_full = jnp.concatenate([jnp.concatenate([sb[0:hd // 2], z64], axis=1),
                                  jnp.concatenate([z64, sb[hd // 2:hd]], axis=1)], axis=0)
        lhs = jnp.concatenate([att.astype(BF16) * mask_w[d], qe], axis=1)
        outs.append(jnp.dot(lhs, jnp.concatenate([vbd, s_full], axis=0),
                            preferred_element_type=F32))
        states[d] = states[d] * dcol + kv * bd_c
    return outs, states


def _gla_tables():
    c = GLA_CHUNK
    t = np.arange(c)[:, None]
    r = np.arange(c)[None, :]
    sizes = [2 * b for b in GLA_LEVELS]
    same = lambda b: (t // b) == (r // b)
    a_f = [same(b) & (r <= t) for b in sizes] + [same(b) & (r > t) for b in sizes]
    a_b = [same(b) & (r >= t) for b in sizes] + [same(b) & (r < t) for b in sizes]
    m_f = [((t // b) % 2 == 1) & ((r // b) == (t // b) - 1) for b in GLA_LEVELS] + [t == r]
    m_b = [((t // b) % 2 == 0) & ((r // b) == (t // b) + 1) for b in GLA_LEVELS] + [t == r]
    amat = np.stack([np.tile(np.concatenate(a, 0), (1, GATE_PIECES)) for a in (a_f, a_b)]).astype(np.float32)
    mask = np.stack([np.stack([np.tile(m, (GLA_HEADS, 1)) for m in ms]) for ms in (m_f, m_b)])
    return amat, mask.astype(np.float32)


def _gla_chunk(q, k, v, g3, amat, masks, hmask_q, bd_v, bd_s, s_prev, *, backward):
    c = GLA_CHUNK
    nl = len(GLA_LEVELS)
    g = sum(g3[i * c:(i + 1) * c].astype(F32) for i in range(GATE_PIECES))
    ps = jnp.dot(amat, g3, preferred_element_type=F32)
    blk = lambda i: ps[i * c:(i + 1) * c]
    q_exp = [g] + [blk(i) for i in range(nl - 1)]
    k_exp = [None] + [blk(nl + i) for i in range(nl - 1)]
    q_full = blk(nl - 1)
    k_full = blk(2 * nl - 1)

    att = None
    for lvl in range(nl + 1):
        if lvl < nl:
            qe = q * jnp.exp(q_exp[lvl])
            ke = k if k_exp[lvl] is None else k * jnp.exp(k_exp[lvl])
        else:
            qe, ke = q, k
        qbd = jnp.concatenate([qe.astype(BF16)] * GLA_HEADS, axis=0) * hmask_q
        p = lax.dot_general(qbd, ke.astype(BF16), (((1,), (1,)), ((), ())), preferred_element_type=F32)
        p = p * masks[lvl]
        att = p if att is None else att + p

    r = jnp.dot(att.astype(BF16), v, preferred_element_type=F32) * bd_v
    o = r[0:c]
    for hh in range(1, GLA_HEADS):
        o = o + r[hh * c:(hh + 1) * c]
    o = o + jnp.dot((q * jnp.exp(q_full)).astype(BF16), s_prev.astype(BF16), preferred_element_type=F32)

    tot = q_full[0:1] if backward else q_full[c - 1:c]
    dcol = jnp.transpose(jnp.broadcast_to(jnp.exp(tot), (GLA_QK, GLA_QK)))
    dcol = jnp.concatenate([dcol, dcol], axis=1)
    kv = lax.dot_general((k * jnp.exp(k_full)).astype(BF16), v, (((0,), (0,)), ((), ())),
                         preferred_element_type=F32)
    return o, s_prev * dcol + kv * bd_s


def _gla_kernel(gla_ref, lg_ref, kt_ref, lgt_ref, a3_ref, a3t_ref, mw_ref, bdc_ref, amat_ref, mask_ref,
                hq_ref, bdv_ref, bds_ref, ind_ref, norm_ref, out_ref, of_ref, ob_ref, *, t_lat, t_all, t_out):
    n2 = 2 * GLA_QK
    o_v = 2 * GLA_QK

    def pieces(rows, d):
        return jnp.concatenate([lg_ref[0, rows, i * n2 + d * GLA_QK:i * n2 + (d + 1) * GLA_QK]
                                for i in range(GATE_PIECES)], axis=0)

    cf_ = GLA_FAST_CHUNK
    worst = None
    for ci in range(t_all // cf_):
        tot = jnp.sum(lg_ref[0, ci * cf_:(ci + 1) * cf_, 0:n2].astype(F32), axis=0, keepdims=True)
        worst = tot if worst is None else jnp.minimum(worst, tot)
    safe = jnp.min(worst) >= GLA_SAFE_LOG_DECAY

    @pl.when(safe)
    def _fast():
        c = GLA_FAST_CHUNK
        n_lat = t_lat // c
        n_all = t_all // c
        per_step = GLA_FAST_UNROLL
        assert n_all % per_step == 0

        def job(ci, d):
            rows = pl.ds(pl.multiple_of(ci * c, c), c)
            gt3 = jnp.concatenate([lgt_ref[0, i * n2 + d * GLA_QK:i * n2 + (d + 1) * GLA_QK, rows]
                                   for i in range(GATE_PIECES)], axis=1)
            return rows, (d, gla_ref[0, rows, 0:GLA_QK], kt_ref[0, :, rows], gla_ref[0, rows, o_v:o_v + GLA_V],
                          pieces(rows, d), gt3)

        def step(i, carry):
            chunks = [(lax.rem(i * per_step + u + n_lat, n_all), 0) for u in range(per_step)]
            chunks += [(n_all - 1 - (i * per_step + u), 1) for u in range(per_step)]
            rows, jobs = zip(*[job(ci, d) for ci, d in chunks])
            outs, states = _gla_fast_chunks(jobs, a3_ref, a3t_ref, mw_ref, bdc_ref[...], carry)
            for (_, d), r, o in zip(chunks, rows, outs):
                (ob_ref if d else of_ref)[r, :] = o
            return tuple(states)

        zero = jnp.zeros((GLA_QK, 128), F32)
        lax.fori_loop(0, n_all // per_step, step, (zero, zero))

    @pl.when(jnp.logical_not(safe))
    def _robust():
        c = GLA_CHUNK
        n_lat = t_lat // c
        n_all = t_all // c
        hq = hq_ref[...]
        bdv = bdv_ref[...]
        bds = bds_ref[...]

        def one(ci, d, s):
            start = pl.multiple_of(ci * c, c)
            rows = pl.ds(start, c)
            q = gla_ref[0, rows, 0:GLA_QK].astype(F32)
            k = gla_ref[0, rows, GLA_QK:2 * GLA_QK].astype(F32)
            v = gla_ref[0, rows, o_v:o_v + GLA_V]
            o, s = _gla_chunk(q, k, v, pieces(rows, d), amat_ref[d],
                              [mask_ref[d, l] for l in range(len(GLA_LEVELS) + 1)], hq, bdv, bds, s,
                              backward=bool(d))
            (ob_ref if d else of_ref)[rows, :] = o
            return s

        def step(i, carry):
            s_f, s_b = carry
            return one(lax.rem(i + n_lat, n_all), 0, s_f), one(n_all - 1 - i, 1, s_b)

        zero = jnp.zeros((GLA_QK, GLA_V), F32)
        lax.fori_loop(0, n_all, step, (zero, zero))

    tr = 256
    ind = ind_ref[...]
    for r0 in range(0, t_out, tr):
        o = of_ref[r0:r0 + tr, :] + ob_ref[r0:r0 + tr, :]
        sq = o * o
        hi = sq.astype(BF16)
        lo = (sq - hi.astype(F32)).astype(BF16)
        ms = jnp.dot(jnp.concatenate([hi, lo], axis=1), ind, preferred_element_type=F32)
        gate = gla_ref[0, r0:r0 + tr, o_v + GLA_V:COL_GLA].astype(F32)
        y = o * lax.rsqrt(ms + 1e-6) * norm_ref[...] * _silu(gate)
        out_ref[0, r0:r0 + tr, :] = y.astype(BF16)


def _gla(gla, lg3, kt, lgt3, consts, norm_t, *, t_lat, t_out):
    bsz, t_all, _ = gla.shape
    n2 = 2 * GLA_QK
    return pl.pallas_call(
        functools.partial(_gla_kernel, t_lat=t_lat, t_all=t_all, t_out=t_out),
        grid=(bsz,),
        in_specs=[pl.BlockSpec((1, t_all, COL_GLA), lambda b: (b, 0, 0)),
                  pl.BlockSpec((1, t_all, GATE_PIECES * n2), lambda b: (b, 0, 0)),
                  pl.BlockSpec((1, GLA_QK, t_all), lambda b: (b, 0, 0)),
                  pl.BlockSpec((1, GATE_PIECES * n2, t_all), lambda b: (b, 0, 0))] +
                 [_const_spec(a.shape) for a in consts] + [_const_spec(norm_t.shape)],
        out_specs=pl.BlockSpec((1, t_out, GLA_V), lambda b: (b, 0, 0)),
        out_shape=jax.ShapeDtypeStruct((bsz, t_out, GLA_V), BF16),
        scratch_shapes=[pltpu.VMEM((t_all, GLA_V), F32), pltpu.VMEM((t_all, GLA_V), F32)],
        compiler_params=_cparams("parallel"),
        name="gla",
    )(gla, lg3, kt, lgt3, *consts, norm_t)


def _gmlp_kernel(z_ref, lng_ref, lnb_ref, ws_ref, bs_ref, gmask_ref, og_ref, out_ref, *, tm):
    z = z_ref[0].astype(F32)
    zf = 0.5 * z * (1.0 + lax.erf(z * (2.0 ** -0.5)))
    u = zf[:, 0:GMLP_WIDTH]
    v = zf[:, GMLP_WIDTH:]
    mu = jnp.mean(v, axis=-1, keepdims=True)
    vc = v - mu
    v = vc * lax.rsqrt(jnp.mean(vc * vc, axis=-1, keepdims=True) + 1e-5) * lng_ref[...] + lnb_ref[...]
    vb = v.astype(BF16)
    ws = ws_ref[...]
    p = GMLP_CHUNK
    for ci in range(tm // p):
        r = jnp.dot(ws, vb[ci * p:(ci + 1) * p], preferred_element_type=F32) * gmask_ref[...]
        mixed = bs_ref[...]
        for g in range(GMLP_GROUPS):
            mixed = mixed + r[g * p:(g + 1) * p]
        y = u[ci * p:(ci + 1) * p] * mixed
        out_ref[0, ci * p:(ci + 1) * p, :] = _rms(y, og_ref[...]).astype(BF16)


def _gmlp(gm, ln_g, ln_b, ws_s, bs_t, gmask, out_g, *, t_out, tm):
    bsz = gm.shape[0]
    return pl.pallas_call(
        functools.partial(_gmlp_kernel, tm=tm),
        grid=(bsz, t_out // tm),
        in_specs=[pl.BlockSpec((1, tm, COL_GM), lambda b, j: (b, j, 0)),
                  _const_spec(ln_g.shape), _const_spec(ln_b.shape), _const_spec(ws_s.shape),
                  _const_spec(bs_t.shape), _const_spec(gmask.shape), _const_spec(out_g.shape)],
        out_specs=pl.BlockSpec((1, tm, GMLP_WIDTH), lambda b, j: (b, j, 0)),
        out_shape=jax.ShapeDtypeStruct((bsz, t_out, GMLP_WIDTH), BF16),
        compiler_params=_cparams("parallel", "parallel"),
        name="gmlp",
    )(gm, ln_g, ln_b, ws_s, bs_t, gmask, out_g)


def _swa_attend(qt_ref, k_ref, vt_ref, sink_ref, og_ref, eye_ref, out_ref, s_ref, p_ref, u, key_rows, biases):
    w = SWA_WINDOW
    d = HEAD_DIM
    kt = SWA_KEY_TILE
    ph = SWA_PART_HEADS
    pw = ph * w
    parts = range(SWA_HEADS // ph)
    group = lambda part: part * ph // SWA_REP
    s_ref = s_ref.at[u]
    p_ref = p_ref.at[u]
    cols = slice(u * w, (u + 1) * w)
    keys = jnp.concatenate([k_ref[0, pl.ds(start, size), :] for start, size in key_rows], axis=0)
    nk = keys.shape[0]

    zero = jnp.zeros((d, pw), BF16)
    sinks, tops = [], []
    for part in parts:
        q = jnp.concatenate([qt_ref[0, h * d:(h + 1) * d, cols] for h in range(part * ph, (part + 1) * ph)], axis=1)
        qbd = jnp.concatenate([q if g == group(part) else zero for g in range(SWA_KV_HEADS)], axis=0)
        best, off = None, 0
        for (_, size), bias in zip(key_rows, biases):
            blk = jnp.dot(keys[off:off + size], qbd, preferred_element_type=F32)
            if bias is not None:
                blk = blk + jnp.concatenate([bias] * ph, axis=1)
            s_ref[part, off:off + size, :] = blk
            m8 = jnp.max(blk.reshape(size // 8, 8, pw), axis=0)
            best = m8 if best is None else jnp.maximum(best, m8)
            off += size
        sink = sink_ref[:, part * pw:(part + 1) * pw] * LOG2E
        sinks.append(sink)
        tops.append(jnp.maximum(jnp.max(best, axis=0, keepdims=True), sink))
    yield

    for part in parts:
        for r0 in range(0, nk, kt):
            p_ref[part, r0:r0 + kt, :] = jnp.exp2(s_ref[part, r0:r0 + kt, :] - tops[part]).astype(BF16)
    yield

    ones = jnp.ones((SWA_DEN_ROWS, nk), BF16)
    vts = [jnp.concatenate([vt_ref[0, g * d:(g + 1) * d, pl.ds(start, size)] for start, size in key_rows] , axis=1)
           for g in range(SWA_KV_HEADS)]
    heads = []
    for part in parts:
        o_ext = jnp.dot(jnp.concatenate([vts[group(part)], ones], axis=0), p_ref[part, 0:nk, :],
                        preferred_element_type=F32)
        scaled = o_ext[0:d] / (o_ext[d:d + 1] + jnp.exp2(sinks[part] - tops[part]))
        heads += [scaled[:, r * w:(r + 1) * w] for r in range(ph)]
    o_t = jnp.concatenate(heads, axis=0)
    yield

    y_t = o_t * lax.rsqrt(jnp.mean(o_t * o_t, axis=0, keepdims=True) + 1e-6) * og_ref[...]
    out = lax.dot_general(eye_ref[...], y_t.astype(BF16), (((1,), (1,)), ((), ())), preferred_element_type=F32)
    out_ref[0, cols, :] = out.astype(BF16)


def _run_stages(blocks):
    for t in range(len(blocks) + 2):
        for u in (t, t - 1, t - 2, t - 1):
            if 0 <= u < len(blocks):
                next(blocks[u], None)


def _swa_kernel(qt_ref, k_ref, vt_ref, sink_ref, og_ref, eye_ref, out_ref, s_ref, p_ref, *, t_lat, t_all, t_out, sb):
    w = SWA_WINDOW
    step = pl.program_id(1)
    n_lat = t_lat // w
    ctx_rows = (t_lat, t_all - t_lat)
    attend = functools.partial(_swa_attend, qt_ref, k_ref, vt_ref, sink_ref, og_ref, eye_ref, out_ref, s_ref, p_ref)

    def run(latent):
        blk = lambda i: (pl.multiple_of(i * w, w), w)
        sk = lax.broadcasted_iota(jnp.int32, (w, w), 0)
        tq = lax.broadcasted_iota(jnp.int32, (w, w), 1)
        neg = jnp.full((w, w), -jnp.inf, F32)
        zero = jnp.zeros((w, w), F32)
        blocks = []
        for u, is_latent in enumerate(latent):
            if not is_latent:
                blocks.append(attend(u, [ctx_rows], [None]))
                continue
            n = step * sb + u
            b_prev = jnp.where((sk >= tq) & (n >= 1), zero, neg)
            b_next = jnp.where((sk <= tq) & (n < n_lat - 1), zero, neg)
            blocks.append(attend(u, [ctx_rows, blk(jnp.maximum(n - 1, 0)), blk(n), blk(jnp.minimum(n + 1, n_lat - 1))],
                                 [None, b_prev, None, b_next]))
        _run_stages(blocks)

    n_steps = t_out // (sb * w)
    makeup = [tuple(s * sb + u < n_lat for u in range(sb)) for s in range(n_steps)]
    for kind in sorted(set(makeup), reverse=True):
        steps = [s for s in range(n_steps) if makeup[s] == kind]
        assert steps == list(range(steps[0], steps[-1] + 1))
        pl.when((step >= steps[0]) & (step <= steps[-1]))(functools.partial(run, kind))


def _swa(sqt, sk, svt, sink_t, out_g, eye, *, t_lat, t_out, sb):
    bsz, t_all, _ = sk.shape
    w = SWA_WINDOW
    n_keys = t_all - t_lat + 3 * w
    n_parts = SWA_HEADS // SWA_PART_HEADS
    assert t_out % (sb * w) == 0
    return pl.pallas_call(
        functools.partial(_swa_kernel, t_lat=t_lat, t_all=t_all, t_out=t_out, sb=sb),
        grid=(bsz, t_out // (sb * w)),
        in_specs=[pl.BlockSpec((1, SWA_Q, sb * w), lambda b, n: (b, 0, n)),
                  pl.BlockSpec((1, t_all, SWA_KV), lambda b, n: (b, 0, 0)),
                  pl.BlockSpec((1, SWA_KV, t_all), lambda b, n: (b, 0, 0)),
                  _const_spec(sink_t.shape), _const_spec(out_g.shape), _const_spec(eye.shape)],
        out_specs=pl.BlockSpec((1, sb * w, SWA_Q), lambda b, n: (b, n, 0)),
        out_shape=jax.ShapeDtypeStruct((bsz, t_out, SWA_Q), BF16),
        scratch_shapes=[pltpu.VMEM((sb, n_parts, n_keys, SWA_PART_HEADS * w), F32),
                        pltpu.VMEM((sb, n_parts, n_keys, SWA_PART_HEADS * w), BF16)],
        compiler_params=_cparams("parallel", "parallel"),
        name="swa",
    )(sqt, sk, svt, sink_t, out_g, eye)


FFN_CHUNKS = ((0, 1536), (1536, 1280))


def _post_kernel(*refs, t_lat, tm, ctx_row, n_stream, ctx_separate):
    a_ref, b_ref, c_ref, mod_ref, n1_ref, n2a_ref, n2b_ref, wo_ref, wgu_ref, wd_ref, out_ref = refs[n_stream:]
    b = pl.program_id(0)
    j = pl.program_id(1)
    x = _stream_tile(refs[:n_stream], j, tm=tm, t_lat=t_lat, ctx_separate=ctx_separate)
    mod = functools.partial(_mod_vectors, mod_ref, b, j, t_lat=t_lat, tm=tm, ctx_row=ctx_row)
    split = _row_split(t_lat, tm)
    gate1, shift2, scale2, gate2 = mod(2), mod(3), mod(4), mod(5)
    g1 = [n1_ref[...] * g for g in gate1]
    g2a = [n2a_ref[...] * (1.0 + s) for s in scale2]
    g2b = [n2b_ref[...] * g for g in gate2]

    cat = jnp.concatenate([a_ref[0], b_ref[0], c_ref[0]], axis=-1)
    y = jnp.dot(cat, wo_ref[...], preferred_element_type=F32)
    x1 = _by_rows(lambda p, x, y: x + _rms(y, g1[p]), split, x, y)
    h = _by_rows(lambda p, x: (_rms(x, g2a[p]) + shift2[p]).astype(BF16), split, x1)
    f = None
    for off, n in FFN_CHUNKS:
        gt = jnp.dot(h, wgu_ref[:, off:off + n], preferred_element_type=F32)
        up = jnp.dot(h, wgu_ref[:, D_FF + off:D_FF + off + n], preferred_element_type=F32)
        act = (_silu(gt) * up).astype(BF16)
        part = jnp.dot(act, wd_ref[off:off + n, :], preferred_element_type=F32)
        f = part if f is None else f + part
    out_ref[0] = _by_rows(lambda p, x, f: x + _rms(f, g2b[p]), split, x1, f)


def _post(a, bo, c, x, ctx, mod, n1_post, n2_pre, n2_post, wo, wgu, wd, *, t_lat, t_out, tm, ctx_row):
    bsz = x.shape[0]
    stream, stream_specs = _stream_specs(x, ctx, tm)
    tok = lambda n: pl.BlockSpec((1, tm, n), lambda b, j: (b, j, 0))
    once = lambda arr: pl.BlockSpec(arr.shape, lambda b, j: (0,) * arr.ndim, pipeline_mode=pl.Buffered(1))
    return pl.pallas_call(
        functools.partial(_post_kernel, t_lat=t_lat, tm=tm, ctx_row=ctx_row, n_stream=len(stream),
                          ctx_separate=ctx is not None),
        grid=(bsz, t_out // tm),
        in_specs=stream_specs + [tok(GLA_V), tok(GMLP_WIDTH), tok(SWA_Q), _const_spec(mod.shape),
                                 _const_spec(n1_post.shape), _const_spec(n2_pre.shape), _const_spec(n2_post.shape),
                                 once(wo), once(wgu), once(wd)],
        out_specs=tok(D_MODEL),
        out_shape=jax.ShapeDtypeStruct((bsz, t_out, D_MODEL), F32),
        compiler_params=_cparams("parallel", "parallel"),
        name="post",
    )(*stream, a, bo, c, mod, n1_post, n2_pre, n2_post, wo, wgu, wd)


def _rope_tables(t_lat, t_ctx):
    rows = t_lat // GRID_W
    row = jnp.repeat(jnp.arange(rows), GRID_W).astype(F32)
    col = jnp.tile(jnp.arange(GRID_W), rows).astype(F32)
    inv_freq = jnp.power(ROPE_THETA, -jnp.arange(0, ROPE_AXIS_DIM, 2, dtype=F32) / ROPE_AXIS_DIM)
    ang_row = row[:, None] * inv_freq[None, :]
    ang_col = col[:, None] * inv_freq[None, :]
    ang = jnp.concatenate([ang_row, ang_row, ang_col, ang_col], axis=-1)
    sign = jnp.tile(jnp.concatenate([-jnp.ones((16,), F32), jnp.ones((16,), F32)]), 2)
    cos = jnp.tile(jnp.concatenate([jnp.cos(ang), jnp.ones((t_ctx, HEAD_DIM), F32)], axis=0), (1, 2))
    sin = jnp.tile(jnp.concatenate([jnp.sin(ang) * sign, jnp.zeros((t_ctx, HEAD_DIM), F32)], axis=0), (1, 2))
    return cos, sin, cos.T, sin.T


def _inproj_weights(w, wa2, ba):
    r = GLA_GATE_RANK
    o_code = 2 * GLA_QK + 2 * GLA_V
    o_gm = o_code + 2 * r
    o_sq = o_gm + 2 * GMLP_WIDTH
    o_sk = o_sq + SWA_Q
    o_sv = o_sk + SWA_KV
    code = jnp.concatenate([w[:, o_code:o_gm], jnp.zeros((w.shape[0], COL_CODE - 2 * r), w.dtype)], axis=1)
    w_tok = jnp.concatenate([w[:, 0:o_code], w[:, o_gm:o_sq], w[:, o_sk:o_sv], code], axis=1)
    w_t = jnp.concatenate([w[:, o_sq:o_sk], w[:, o_sv:], w[:, GLA_QK:2 * GLA_QK]], axis=1).T
    wa = jnp.zeros((COL_CODE, 2 * GLA_QK), F32)
    wa = wa.at[0:r, 0:GLA_QK].set(wa2[0]).at[r:2 * r, GLA_QK:].set(wa2[1])
    return w_tok.astype(BF16), w_t.astype(BF16), wa.astype(BF16), ba.reshape(1, 2 * GLA_QK)


def _gla_consts():
    a3, a3t_neg, mask_w, bd_c = _gla_fast_tables()
    amat, mask = _gla_tables()
    lane_head = np.arange(GLA_QK)[None, :] // GLA_DK
    row_head = np.arange(GLA_HEADS * GLA_CHUNK)[:, None] // GLA_CHUNK
    hq = (lane_head == row_head).astype(np.float32)
    vlane_head = np.arange(GLA_V)[None, :] // GLA_DV
    bdv = (vlane_head == row_head).astype(np.float32)
    bds = (vlane_head == (np.arange(GLA_QK)[:, None] // GLA_DK)).astype(np.float32)
    ind = np.tile((vlane_head == vlane_head.T).astype(np.float32) / GLA_DV, (2, 1))
    return (jnp.asarray(a3, BF16), jnp.asarray(a3t_neg, BF16), jnp.asarray(mask_w, BF16),
            jnp.asarray(bd_c, F32), jnp.asarray(amat, BF16), jnp.asarray(mask, F32), jnp.asarray(hq, BF16),
            jnp.asarray(bdv, F32), jnp.asarray(bds, F32), jnp.asarray(ind, BF16))


def kernel(x, c, ctx, c_ctx, mod_w, mod_b, n1_pre, n1_post, n2_pre, n2_post, w_in, w_out, gla_wa2, gla_ba,
           gla_norm, gmlp_ln_g, gmlp_ln_b, gmlp_ws, gmlp_bs, gmlp_out_g, swa_sink, swa_out_g, ffn_w_gu,
           ffn_w_down):
    bsz, t_lat, d = x.shape
    t_ctx = ctx.shape[1]
    t_all = t_lat + t_ctx
    depth = mod_w.shape[0]
    assert d == D_MODEL and bsz < MOD_ROWS
    assert t_lat % 1024 == 0 and t_ctx % 256 == 0 and t_all % 768 == 0
    ctx_row = bsz
    tm_all = 768
    assert t_ctx == tm_all - _row_split(t_lat, tm_all)

    cc = jnp.zeros((MOD_ROWS, d), F32).at[0:bsz].set(c).at[ctx_row].set(c_ctx)
    mods = _modulation(cc, mod_w, mod_b)
    rope = _rope_tables(t_lat, t_ctx)
    gla_consts = _gla_consts()
    eye = jnp.eye(SWA_WINDOW, dtype=BF16)
    gmask = jnp.asarray((np.arange(GMLP_WIDTH)[None, :] // GMLP_GDIM ==
                         np.arange(GMLP_GROUPS * GMLP_CHUNK)[:, None] // GMLP_CHUNK).astype(np.float32))
    row = lambda v: v.reshape(1, -1)

    assert t_ctx == STREAM_PIECE
    xs, xs_ctx = x, ctx
    for l in range(depth):
        last = l == depth - 1
        t_out = t_lat if last else t_all
        gla, lg3, gm, sk, sqt, svt, kt, lgt3 = _inproj(
            xs, xs_ctx, mods[l], row(n1_pre[l]), _inproj_weights(w_in[l], gla_wa2[l], gla_ba[l]), rope,
            t_lat=t_lat, t_all=t_all, tm=tm_all, ctx_row=ctx_row)
        a_out = _gla(gla, lg3, kt, lgt3, gla_consts, row(jnp.tile(gla_norm[l], GLA_HEADS)),
                     t_lat=t_lat, t_out=t_out)
        ws_s = gmlp_ws[l].reshape(GMLP_GROUPS * GMLP_CHUNK, GMLP_CHUNK).astype(BF16)
        bs_t = jnp.repeat(gmlp_bs[l].T, GMLP_GDIM, axis=1)
        b_out = _gmlp(gm, row(gmlp_ln_g[l]), row(gmlp_ln_b[l]), ws_s, bs_t, gmask, row(gmlp_out_g[l]),
                      t_out=t_out, tm=1024 if last else tm_all)
        sink_t = row(jnp.repeat(swa_sink[l], SWA_WINDOW))
        out_g_t = jnp.broadcast_to(swa_out_g[l][:, None], (SWA_Q, SWA_WINDOW))
        sb = next(n for n in SWA_STEP_BLOCKS if (t_out // SWA_WINDOW) % n == 0)
        c_out = _swa(sqt, sk, svt, sink_t, out_g_t, eye, t_lat=t_lat, t_out=t_out, sb=sb)
        xs = _post(a_out, b_out, c_out, xs, xs_ctx, mods[l], row(n1_post[l]), row(n2_pre[l]), row(n2_post[l]),
                   w_out[l].astype(BF16), ffn_w_gu[l].astype(BF16), ffn_w_down[l].astype(BF16),
                   t_lat=t_lat, t_out=t_out, tm=1024 if last else tm_all, ctx_row=ctx_row)
        xs_ctx = None
    return xs
```

```python
import functools

import numpy as np
import jax
import jax.numpy as jnp
from jax import lax
from jax.experimental import pallas as pl
from jax.experimental.pallas import tpu as pltpu

F32 = jnp.float32
BF16 = jnp.bfloat16

D_MODEL = 1024
GRID_W = 64
HEAD_DIM = 64
GLA_HEADS = 4
GLA_DK = 32
GLA_DV = 64
GLA_QK = GLA_HEADS * GLA_DK
GLA_V = GLA_HEADS * GLA_DV
GLA_GATE_RANK = 16
GLA_GATE_TAU = 16.0
GLA_CHUNK = 64
GMLP_GROUPS = 4
GMLP_GDIM = 64
GMLP_WIDTH = GMLP_GROUPS * GMLP_GDIM
GMLP_CHUNK = 128
SWA_HEADS = 8
SWA_KV_HEADS = 2
SWA_REP = SWA_HEADS // SWA_KV_HEADS
SWA_Q = SWA_HEADS * HEAD_DIM
SWA_KV = SWA_KV_HEADS * HEAD_DIM
SWA_WINDOW = 128
ROPE_AXIS_DIM = HEAD_DIM // 2
ROPE_THETA = 10000.0
MIX_WIDTH = GLA_V + GMLP_WIDTH + SWA_Q
D_FF = -(-8 * D_MODEL // (3 * 256)) * 256
N_MOD = 6
MOD_ROWS = 24

COL_GLA = 2 * GLA_QK + 2 * GLA_V
COL_GM = 2 * GMLP_WIDTH
COL_SK = SWA_KV
COL_CODE = 128
OFF_GM = COL_GLA
OFF_SK = OFF_GM + COL_GM
OFF_CODE = OFF_SK + COL_SK
ROW_SQ = 0
ROW_SV = ROW_SQ + SWA_Q
ROW_GK = ROW_SV + SWA_KV

LOG2E = 1.4426950408889634
SWA_Q_SCALE = HEAD_DIM ** -0.5 * LOG2E
SWA_KEY_TILE = 64
SWA_PART_HEADS = 2
SWA_STEP_BLOCKS = (8, 6, 4, 3)
SWA_DEN_ROWS = 16

VMEM_LIMIT = 56 * 1024 * 1024
GLA_LEVELS = (1, 2, 4, 8, 16, 32)


def _cparams(*sem):
    return pltpu.CompilerParams(dimension_semantics=sem, vmem_limit_bytes=VMEM_LIMIT)


def _const_spec(shape):
    nd = len(shape)
    return pl.BlockSpec(shape, lambda *_: (0,) * nd)


def _rms(x, g, eps=1e-6):
    return x * lax.rsqrt(jnp.mean(x * x, axis=-1, keepdims=True) + eps) * g


def _silu(x):
    return x * jax.nn.sigmoid(x)


def _row_split(t_lat, tm):
    return t_lat % tm if t_lat % tm else tm


def _mod_vectors(mod_ref, b, j, idx, *, t_lat, tm, ctx_row):
    lo = idx * D_MODEL
    lat = mod_ref[pl.ds(b, 1), lo:lo + D_MODEL]
    ctx = mod_ref[ctx_row:ctx_row + 1, lo:lo + D_MODEL]
    return lat, jnp.where(j * tm + _row_split(t_lat, tm) >= t_lat, ctx, lat)


STREAM_PIECE = 256


def _stream_specs(x, ctx, tm, tile=lambda b, j: (b, j), **spec_args):
    rows, d = x.shape[1:]
    per_tile = tm // STREAM_PIECE
    last = rows // STREAM_PIECE - 1

    def piece(k):
        def index(b, j):
            tb, tj = tile(b, j)
            return tb, jnp.minimum(tj * per_tile + k, last), 0
        return pl.BlockSpec((1, STREAM_PIECE, d), index, **spec_args)

    operands, specs = [x] * per_tile, [piece(k) for k in range(per_tile)]
    if ctx is not None:
        assert ctx.shape[1] == STREAM_PIECE
        operands.append(ctx)
        specs.append(pl.BlockSpec((1, STREAM_PIECE, d), lambda b, j: (tile(b, j)[0], 0, 0)))
    return operands, specs


def _stream_tile(refs, j, *, tm, t_lat, ctx_separate):
    pieces = [r[0] for r in refs[:tm // STREAM_PIECE]]
    if ctx_separate:
        pieces[-1] = jnp.where((j + 1) * tm - STREAM_PIECE >= t_lat, refs[-1][0], pieces[-1])
    return jnp.concatenate(pieces, axis=0)


def _by_rows(fn, split, *arrays):
    tm = arrays[0].shape[0]
    if split == tm:
        return fn(0, *arrays)
    return jnp.concatenate([fn(0, *[a[0:split] for a in arrays]), fn(1, *[a[split:tm] for a in arrays])], axis=0)


def _mod_kernel(c_ref, w_ref, b_ref, o_ref):
    s = _silu(c_ref[...]).astype(BF16)
    o_ref[0] = jnp.dot(s, w_ref[0].astype(BF16), preferred_element_type=F32) + b_ref[0]


def _modulation(cc, mod_w, mod_b):
    depth, d, n = mod_w.shape
    tn = n // 4
    return pl.pallas_call(
        _mod_kernel,
        grid=(depth, n // tn),
        in_specs=[pl.BlockSpec((MOD_ROWS, d), lambda l, j: (0, 0)),
                  pl.BlockSpec((1, d, tn), lambda l, j: (l, 0, j)),
                  pl.BlockSpec((1, 1, tn), lambda l, j: (l, 0, j))],
        out_specs=pl.BlockSpec((1, MOD_ROWS, tn), lambda l, j: (l, 0, j)),
        out_shape=jax.ShapeDtypeStruct((depth, MOD_ROWS, n), F32),
        compiler_params=_cparams("parallel", "parallel"),
        name="modulation",
    )(cc, mod_w, mod_b.reshape(depth, 1, n))


def _rope(z, cos, sin):
    lane = lax.broadcasted_iota(jnp.int32, z.shape, 1)
    first = (lane % 32) < 16
    rot = jnp.where(first, pltpu.roll(z, 128 - 16, 1), pltpu.roll(z, 16, 1))
    return z * cos + rot * sin


GATE_PIECES = 2
GATE_TOTAL_ROWS = 8


def _split(x):
    p1 = x.astype(BF16)
    return p1, (x - p1.astype(F32)).astype(BF16)


def _log_gate(z):
    return (jnp.minimum(z, 0.0) - jnp.log(1.0 + jnp.exp(-jnp.abs(z)))) * (1.0 / GLA_GATE_TAU)


def _rope_t(z, cos_t, sin_t):
    parts = []
    for r0 in range(0, z.shape[0], 32):
        parts += [z[r0 + 16:r0 + 32], z[r0:r0 + 16]]
    rot = jnp.concatenate(parts, axis=0)
    reps = z.shape[0] // cos_t.shape[0]
    return z * jnp.concatenate([cos_t] * reps, axis=0) + rot * jnp.concatenate([sin_t] * reps, axis=0)


def _inproj_kernel(*refs, t_lat, tm, ctx_row, n_stream, ctx_separate):
    (mod_ref, g_ref, w_ref, wt_ref, wa_ref, ba_ref, cos_ref, sin_ref, cost_ref, sint_ref,
     gla_ref, lg_ref, gm_ref, sk_ref, sqt_ref, svt_ref, kt_ref, lgt_ref, tot_ref, h_buf, ht_buf) = refs[2 * n_stream:]
    b = pl.program_id(0)
    j = pl.program_id(1)
    nt = pl.num_programs(1)
    step = b * nt + j
    slot = lax.rem(step, 2)

    def prologue(stream_refs, tb, tj, dst):
        x = _stream_tile(stream_refs, tj, tm=tm, t_lat=t_lat, ctx_separate=ctx_separate)
        mod = functools.partial(_mod_vectors, mod_ref, tb, tj, t_lat=t_lat, tm=tm, ctx_row=ctx_row)
        shift, scale, gain = mod(0), mod(1), g_ref[...]
        gains = [gain * (1.0 + s) for s in scale]
        hf = _by_rows(lambda part, x: _rms(x, gains[part]) + shift[part], _row_split(t_lat, tm), x)
        h_buf[dst] = hf.astype(BF16)
        ht_buf[dst] = hf.T.astype(BF16)

    @pl.when(step == 0)
    def _first():
        prologue(refs[n_stream:2 * n_stream], 0, 0, 0)

    def project(cur):
        def proj(off, n):
            return jnp.dot(h_buf[cur], w_ref[:, off:off + n], preferred_element_type=F32)

        def proj_t(off, n):
            return jnp.dot(wt_ref[off:off + n, :], ht_buf[cur], preferred_element_type=F32)

        code = proj(OFF_CODE, COL_CODE).astype(BF16)
        z = proj(0, COL_GLA)
        gla_ref[0, :, 0:GLA_QK] = (z[:, 0:GLA_QK] * (GLA_DK ** -0.5)).astype(BF16)
        gla_ref[0, :, GLA_QK:COL_GLA] = z[:, GLA_QK:COL_GLA].astype(BF16)

        n2 = 2 * GLA_QK
        lg = _log_gate(jnp.dot(code, wa_ref[...], preferred_element_type=F32) + ba_ref[...])
        for i, p in enumerate(_split(lg)):
            lg_ref[0, :, i * n2:(i + 1) * n2] = p
        for i, p in enumerate(_split(lg.T)):
            lgt_ref[0, i * n2:(i + 1) * n2, :] = p
        c = GLA_FAST_CHUNK
        totals = [jnp.sum(lg[i * c:(i + 1) * c], axis=0, keepdims=True) for i in range(tm // c)]
        tot_ref[0, 0] = jnp.concatenate(totals + [jnp.zeros((GATE_TOTAL_ROWS - tm // c, n2), F32)], axis=0)

        gm_ref[0] = proj(OFF_GM, COL_GM).astype(BF16)
        nb, nj = _next_tile(b, j, pl.num_programs(0), nt)
        prologue(refs[:n_stream], nb, nj, 1 - cur)
        kt_ref[0] = proj_t(ROW_GK, GLA_QK).astype(BF16)

        sk_ref[0] = _rope(proj(OFF_SK, COL_SK), cos_ref[...], sin_ref[...]).astype(BF16)
        sqt_ref[0] = (_rope_t(proj_t(ROW_SQ, SWA_Q), cost_ref[...], sint_ref[...]) * SWA_Q_SCALE).astype(BF16)
        svt_ref[0] = proj_t(ROW_SV, SWA_KV).astype(BF16)

    for cur in range(2):
        pl.when(slot == cur)(functools.partial(project, cur))


def _next_tile(b, j, nb, nt):
    wrap = j + 1 == nt
    last = jnp.logical_and(wrap, b + 1 == nb)
    return jnp.where(wrap & jnp.logical_not(last), b + 1, b), jnp.where(last, j, jnp.where(wrap, 0, j + 1))


def _inproj(x, ctx, mod, n_pre, weights, rope, *, t_lat, t_all, tm, ctx_row):
    bsz, _, d = x.shape
    nt = t_all // tm
    nxt, nxt_specs = _stream_specs(x, ctx, tm, functools.partial(_next_tile, nb=bsz, nt=nt))
    first, first_specs = _stream_specs(x, ctx, tm, lambda b, j: (0, 0))
    stream, stream_specs = nxt + first, nxt_specs + first_specs
    tok = lambda n: pl.BlockSpec((1, tm, n), lambda b, j: (b, j, 0))
    tok_t = lambda n: pl.BlockSpec((1, n, tm), lambda b, j: (b, 0, j))
    n2 = 2 * GLA_QK
    cols = (COL_GLA, GATE_PIECES * n2, COL_GM, COL_SK)
    rows = (SWA_Q, SWA_KV, GLA_QK, GATE_PIECES * n2)
    out_shape = [jax.ShapeDtypeStruct((bsz, t_all, n), BF16) for n in cols]
    out_shape += [jax.ShapeDtypeStruct((bsz, n, t_all), BF16) for n in rows]
    assert tm // GLA_FAST_CHUNK <= GATE_TOTAL_ROWS
    out_shape.append(jax.ShapeDtypeStruct((bsz, nt, GATE_TOTAL_ROWS, n2), F32))
    totals_spec = pl.BlockSpec((1, 1, GATE_TOTAL_ROWS, n2), lambda b, j: (b, j, 0, 0))
    consts = (mod, n_pre) + tuple(weights)
    cos, sin, cos_t, sin_t = rope
    return pl.pallas_call(
        functools.partial(_inproj_kernel, t_lat=t_lat, tm=tm, ctx_row=ctx_row, n_stream=len(nxt),
                          ctx_separate=ctx is not None),
        grid=(bsz, nt),
        in_specs=stream_specs + [_const_spec(a.shape) for a in consts] +
                 [pl.BlockSpec((tm, 128), lambda b, j: (j, 0)), pl.BlockSpec((tm, 128), lambda b, j: (j, 0)),
                  pl.BlockSpec((128, tm), lambda b, j: (0, j)), pl.BlockSpec((128, tm), lambda b, j: (0, j))],
        out_specs=[tok(n) for n in cols] + [tok_t(n) for n in rows] + [totals_spec],
        out_shape=out_shape,
        scratch_shapes=[pltpu.VMEM((2, tm, d), BF16), pltpu.VMEM((2, d, tm), BF16)],
        compiler_params=_cparams("arbitrary", "arbitrary"),
        name="inproj",
    )(*stream, *consts, cos, sin, cos_t, sin_t)


GLA_FAST_CHUNK = 128
GLA_FAST_UNROLL = 6
GLA_SAFE_LOG_DECAY = -40.0


def _gla_fast_tables():
    c = GLA_FAST_CHUNK
    t = np.arange(c)[:, None]
    r = np.arange(c)[None, :]
    incl = [(r <= t), (r >= t)]
    a3 = np.stack([np.tile(m, (1, GATE_PIECES)) for m in incl]).astype(np.float32)
    a3t_neg = -np.stack([np.tile(m.T, (GATE_PIECES, 1)) for m in incl]).astype(np.float32)
    mask_w = np.stack([np.tile(m, (1, GLA_HEADS)) for m in incl]).astype(np.float32)
    half = (np.arange(128)[None, :] // GLA_DV) == ((np.arange(GLA_QK)[:, None] // GLA_DK) % 2)
    return a3, a3t_neg, mask_w, half.astype(np.float32)


def _gla_fast_chunks(jobs, a3, a3t_neg, mask_w, bd_c, states):
    c = GLA_FAST_CHUNK
    dk, hd = GLA_DK, GLA_QK
    cums = [(jnp.dot(a3[d], g3, preferred_element_type=F32),
             jnp.dot(gt3, a3t_neg[d], preferred_element_type=F32))
            for d, _, _, _, g3, gt3 in jobs]

    zr = lambda n: jnp.zeros((n, c), BF16)
    qes, dcols, khs, atts = [], [], [], []
    for (d, q, kt, _, _, _), (cum, ncum_t) in zip(jobs, cums):
        qe = (q.astype(F32) * jnp.exp(cum)).astype(BF16)
        ke_t = kt.astype(F32) * jnp.exp(ncum_t)
        edge = 0 if d else c - 1
        dcol = jnp.exp(-ncum_t[:, edge:edge + 1])
        khs.append((ke_t * dcol).astype(BF16))
        ke_t = ke_t.astype(BF16)
        cols = []
        for hh in range(GLA_HEADS):
            parts = ([zr(hh * dk)] if hh else []) + [ke_t[hh * dk:(hh + 1) * dk]]
            parts += [zr(hd - (hh + 1) * dk)] if hh < GLA_HEADS - 1 else []
            cols.append(jnp.concatenate(parts, axis=0))
        kbd = jnp.concatenate(cols, axis=1)
        atts.append(jnp.dot(qe, kbd, preferred_element_type=F32))
        qes.append(qe)
        dcols.append(dcol)

    kvs = [jnp.concatenate([jnp.dot(kh[0:hd // 2], v[:, 0:128], preferred_element_type=F32),
                            jnp.dot(kh[hd // 2:hd], v[:, 128:256], preferred_element_type=F32)], axis=0)
           for (_, _, _, v, _, _), kh in zip(jobs, khs)]

    lane = lax.broadcasted_iota(jnp.int32, (1, 128), 1)
    m_lo = (lane < GLA_DV).astype(BF16)
    m_hi = (lane >= GLA_DV).astype(BF16)
    z128 = jnp.zeros((c, 128), BF16)
    z64 = jnp.zeros((hd // 2, 128), BF16)
    states = list(states)
    outs = []
    for (d, _, _, v, _, _), qe, dcol, att, kv in zip(jobs, qes, dcols, atts, kvs):
        v_lo, v_hi = v[:, 0:128], v[:, 128:256]
        vbd = jnp.concatenate([jnp.concatenate([v_lo * m_lo, z128], axis=1),
                               jnp.concatenate([v_lo * m_hi, z128], axis=1),
                               jnp.concatenate([z128, v_hi * m_lo], axis=1),
                               jnp.concatenate([z128, v_hi * m_hi], axis=1)], axis=0)
        sb = states[d].astype(BF16)
        s_full = jnp.concatenate([jnp.concatenate([sb[0:hd // 2], z64], axis=1),
                                  jnp.concatenate([z64, sb[hd // 2:hd]], axis=1)], axis=0)
        lhs = jnp.concatenate([att.astype(BF16) * mask_w[d], qe], axis=1)
        outs.append(jnp.dot(lhs, jnp.concatenate([vbd, s_full], axis=0),
                            preferred_element_type=F32))
        states[d] = states[d] * dcol + kv * bd_c
    return outs, states


def _gla_tables():
    c = GLA_CHUNK
    t = np.arange(c)[:, None]
    r = np.arange(c)[None, :]
    sizes = [2 * b for b in GLA_LEVELS]
    same = lambda b: (t // b) == (r // b)
    a_f = [same(b) & (r <= t) for b in sizes] + [same(b) & (r > t) for b in sizes]
    a_b = [same(b) & (r >= t) for b in sizes] + [same(b) & (r < t) for b in sizes]
    m_f = [((t // b) % 2 == 1) & ((r // b) == (t // b) - 1) for b in GLA_LEVELS] + [t == r]
    m_b = [((t // b) % 2 == 0) & ((r // b) == (t // b) + 1) for b in GLA_LEVELS] + [t == r]
    amat = np.stack([np.tile(np.concatenate(a, 0), (1, GATE_PIECES)) for a in (a_f, a_b)]).astype(np.float32)
    mask = np.stack([np.stack([np.tile(m, (GLA_HEADS, 1)) for m in ms]) for ms in (m_f, m_b)])
    return amat, mask.astype(np.float32)


def _gla_chunk(q, k, v, g3, amat, masks, hmask_q, bd_v, bd_s, s_prev, *, backward):
    c = GLA_CHUNK
    nl = len(GLA_LEVELS)
    g = sum(g3[i * c:(i + 1) * c].astype(F32) for i in range(GATE_PIECES))
    ps = jnp.dot(amat, g3, preferred_element_type=F32)
    blk = lambda i: ps[i * c:(i + 1) * c]
    q_exp = [g] + [blk(i) for i in range(nl - 1)]
    k_exp = [None] + [blk(nl + i) for i in range(nl - 1)]
    q_full = blk(nl - 1)
    k_full = blk(2 * nl - 1)

    att = None
    for lvl in range(nl + 1):
        if lvl < nl:
            qe = q * jnp.exp(q_exp[lvl])
            ke = k if k_exp[lvl] is None else k * jnp.exp(k_exp[lvl])
        else:
            qe, ke = q, k
        qbd = jnp.concatenate([qe.astype(BF16)] * GLA_HEADS, axis=0) * hmask_q
        p = lax.dot_general(qbd, ke.astype(BF16), (((1,), (1,)), ((), ())), preferred_element_type=F32)
        p = p * masks[lvl]
        att = p if att is None else att + p

    r = jnp.dot(att.astype(BF16), v, preferred_element_type=F32) * bd_v
    o = r[0:c]
    for hh in range(1, GLA_HEADS):
        o = o + r[hh * c:(hh + 1) * c]
    o = o + jnp.dot((q * jnp.exp(q_full)).astype(BF16), s_prev.astype(BF16), preferred_element_type=F32)

    tot = q_full[0:1] if backward else q_full[c - 1:c]
    dcol = jnp.transpose(jnp.broadcast_to(jnp.exp(tot), (GLA_QK, GLA_QK)))
    dcol = jnp.concatenate([dcol, dcol], axis=1)
    kv = lax.dot_general((k * jnp.exp(k_full)).astype(BF16), v, (((0,), (0,)), ((), ())),
                         preferred_element_type=F32)
    return o, s_prev * dcol + kv * bd_s


def _gla_kernel(gla_ref, lg_ref, kt_ref, lgt_ref, tot_ref, a3_ref, a3t_ref, mw_ref, bdc_ref, amat_ref, mask_ref,
                hq_ref, bdv_ref, bds_ref, ind_ref, norm_ref, out_ref, of_ref, ob_ref, *, t_lat, t_all, t_out):
    n2 = 2 * GLA_QK
    o_v = 2 * GLA_QK

    def pieces(rows, d):
        return jnp.concatenate([lg_ref[0, rows, i * n2 + d * GLA_QK:i * n2 + (d + 1) * GLA_QK]
                                for i in range(GATE_PIECES)], axis=0)

    safe = jnp.min(tot_ref[0]) >= GLA_SAFE_LOG_DECAY

    @pl.when(safe)
    def _fast():
        c = GLA_FAST_CHUNK
        n_lat = t_lat // c
        n_all = t_all // c
        per_step = GLA_FAST_UNROLL
        assert n_all % per_step == 0

        def job(ci, d):
            rows = pl.ds(pl.multiple_of(ci * c, c), c)
            gt3 = jnp.concatenate([lgt_ref[0, i * n2 + d * GLA_QK:i * n2 + (d + 1) * GLA_QK, rows]
                                   for i in range(GATE_PIECES)], axis=1)
            return rows, (d, gla_ref[0, rows, 0:GLA_QK], kt_ref[0, :, rows], gla_ref[0, rows, o_v:o_v + GLA_V],
                          pieces(rows, d), gt3)

        def step(i, carry):
            chunks = [(lax.rem(i * per_step + u + n_lat, n_all), 0) for u in range(per_step)]
            chunks += [(n_all - 1 - (i * per_step + u), 1) for u in range(per_step)]
            rows, jobs = zip(*[job(ci, d) for ci, d in chunks])
            outs, states = _gla_fast_chunks(jobs, a3_ref, a3t_ref, mw_ref, bdc_ref[...], carry)
            for (_, d), r, o in zip(chunks, rows, outs):
                (ob_ref if d else of_ref)[r, :] = o
            return tuple(states)

        zero = jnp.zeros((GLA_QK, 128), F32)
        lax.fori_loop(0, n_all // per_step, step, (zero, zero))

    @pl.when(jnp.logical_not(safe))
    def _robust():
        c = GLA_CHUNK
        n_lat = t_lat // c
        n_all = t_all // c
        hq = hq_ref[...]
        bdv = bdv_ref[...]
        bds = bds_ref[...]

        def one(ci, d, s):
            start = pl.multiple_of(ci * c, c)
            rows = pl.ds(start, c)
            q = gla_ref[0, rows, 0:GLA_QK].astype(F32)
            k = gla_ref[0, rows, GLA_QK:2 * GLA_QK].astype(F32)
            v = gla_ref[0, rows, o_v:o_v + GLA_V]
            o, s = _gla_chunk(q, k, v, pieces(rows, d), amat_ref[d],
                              [mask_ref[d, l] for l in range(len(GLA_LEVELS) + 1)], hq, bdv, bds, s,
                              backward=bool(d))
            (ob_ref if d else of_ref)[rows, :] = o
            return s

        def step(i, carry):
            s_f, s_b = carry
            return one(lax.rem(i + n_lat, n_all), 0, s_f), one(n_all - 1 - i, 1, s_b)

        zero = jnp.zeros((GLA_QK, GLA_V), F32)
        lax.fori_loop(0, n_all, step, (zero, zero))

    tr = 256
    ind = ind_ref[...]
    for r0 in range(0, t_out, tr):
        o = of_ref[r0:r0 + tr, :] + ob_ref[r0:r0 + tr, :]
        sq = o * o
        hi = sq.astype(BF16)
        lo = (sq - hi.astype(F32)).astype(BF16)
        ms = jnp.dot(jnp.concatenate([hi, lo], axis=1), ind, preferred_element_type=F32)
        gate = gla_ref[0, r0:r0 + tr, o_v + GLA_V:COL_GLA].astype(F32)
        y = o * lax.rsqrt(ms + 1e-6) * norm_ref[...] * _silu(gate)
        out_ref[0, r0:r0 + tr, :] = y.astype(BF16)


def _gla(gla, lg3, kt, lgt3, totals, consts, norm_t, *, t_lat, t_out):
    bsz, t_all, _ = gla.shape
    n2 = 2 * GLA_QK
    return pl.pallas_call(
        functools.partial(_gla_kernel, t_lat=t_lat, t_all=t_all, t_out=t_out),
        grid=(bsz,),
        in_specs=[pl.BlockSpec((1, t_all, COL_GLA), lambda b: (b, 0, 0)),
                  pl.BlockSpec((1, t_all, GATE_PIECES * n2), lambda b: (b, 0, 0)),
                  pl.BlockSpec((1, GLA_QK, t_all), lambda b: (b, 0, 0)),
                  pl.BlockSpec((1, GATE_PIECES * n2, t_all), lambda b: (b, 0, 0)),
                  pl.BlockSpec((1,) + totals.shape[1:], lambda b: (b, 0, 0, 0))] +
                 [_const_spec(a.shape) for a in consts] + [_const_spec(norm_t.shape)],
        out_specs=pl.BlockSpec((1, t_out, GLA_V), lambda b: (b, 0, 0)),
        out_shape=jax.ShapeDtypeStruct((bsz, t_out, GLA_V), BF16),
        scratch_shapes=[pltpu.VMEM((t_all, GLA_V), F32), pltpu.VMEM((t_all, GLA_V), F32)],
        compiler_params=_cparams("parallel"),
        name="gla",
    )(gla, lg3, kt, lgt3, totals, *consts, norm_t)


def _gmlp_kernel(z_ref, lng_ref, lnb_ref, ws_ref, bs_ref, gmask_ref, og_ref, out_ref, *, tm):
    z = z_ref[0].astype(F32)
    zf = 0.5 * z * (1.0 + lax.erf(z * (2.0 ** -0.5)))
    u = zf[:, 0:GMLP_WIDTH]
    v = zf[:, GMLP_WIDTH:]
    mu = jnp.mean(v, axis=-1, keepdims=True)
    vc = v - mu
    v = vc * lax.rsqrt(jnp.mean(vc * vc, axis=-1, keepdims=True) + 1e-5) * lng_ref[...] + lnb_ref[...]
    vb = v.astype(BF16)
    ws = ws_ref[...]
    p = GMLP_CHUNK
    for ci in range(tm // p):
        r = jnp.dot(ws, vb[ci * p:(ci + 1) * p], preferred_element_type=F32) * gmask_ref[...]
        mixed = bs_ref[...]
        for g in range(GMLP_GROUPS):
            mixed = mixed + r[g * p:(g + 1) * p]
        y = u[ci * p:(ci + 1) * p] * mixed
        out_ref[0, ci * p:(ci + 1) * p, :] = _rms(y, og_ref[...]).astype(BF16)


def _gmlp(gm, ln_g, ln_b, ws_s, bs_t, gmask, out_g, *, t_out, tm):
    bsz = gm.shape[0]
    return pl.pallas_call(
        functools.partial(_gmlp_kernel, tm=tm),
        grid=(bsz, t_out // tm),
        in_specs=[pl.BlockSpec((1, tm, COL_GM), lambda b, j: (b, j, 0)),
                  _const_spec(ln_g.shape), _const_spec(ln_b.shape), _const_spec(ws_s.shape),
                  _const_spec(bs_t.shape), _const_spec(gmask.shape), _const_spec(out_g.shape)],
        out_specs=pl.BlockSpec((1, tm, GMLP_WIDTH), lambda b, j: (b, j, 0)),
        out_shape=jax.ShapeDtypeStruct((bsz, t_out, GMLP_WIDTH), BF16),
        compiler_params=_cparams("parallel", "parallel"),
        name="gmlp",
    )(gm, ln_g, ln_b, ws_s, bs_t, gmask, out_g)


def _swa_attend(qt_ref, k_ref, vt_ref, sink_ref, og_ref, eye_ref, out_ref, s_ref, p_ref, u, key_rows, biases):
    w = SWA_WINDOW
    d = HEAD_DIM
    kt = SWA_KEY_TILE
    ph = SWA_PART_HEADS
    pw = ph * w
    parts = range(SWA_HEADS // ph)
    group = lambda part: part * ph // SWA_REP
    s_ref = s_ref.at[u]
    p_ref = p_ref.at[u]
    cols = slice(u * w, (u + 1) * w)
    keys = jnp.concatenate([k_ref[0, pl.ds(start, size), :] for start, size in key_rows], axis=0)
    nk = keys.shape[0]

    zero = jnp.zeros((d, pw), BF16)
    sinks, tops = [], []
    for part in parts:
        q = jnp.concatenate([qt_ref[0, h * d:(h + 1) * d, cols] for h in range(part * ph, (part + 1) * ph)], axis=1)
        qbd = jnp.concatenate([q if g == group(part) else zero for g in range(SWA_KV_HEADS)], axis=0)
        best, off = None, 0
        for (_, size), bias in zip(key_rows, biases):
            blk = jnp.dot(keys[off:off + size], qbd, preferred_element_type=F32)
            if bias is not None:
                blk = blk + jnp.concatenate([bias] * ph, axis=1)
            s_ref[part, off:off + size, :] = blk
            m8 = jnp.max(blk.reshape(size // 8, 8, pw), axis=0)
            best = m8 if best is None else jnp.maximum(best, m8)
            off += size
        sink = sink_ref[:, part * pw:(part + 1) * pw] * LOG2E
        sinks.append(sink)
        tops.append(jnp.maximum(jnp.max(best, axis=0, keepdims=True), sink))
    yield

    for part in parts:
        for r0 in range(0, nk, kt):
            p_ref[part, r0:r0 + kt, :] = jnp.exp2(s_ref[part, r0:r0 + kt, :] - tops[part]).astype(BF16)
    yield

    ones = jnp.ones((SWA_DEN_ROWS, nk), BF16)
    vts = [jnp.concatenate([vt_ref[0, g * d:(g + 1) * d, pl.ds(start, size)] for start, size in key_rows] , axis=1)
           for g in range(SWA_KV_HEADS)]
    heads = []
    for part in parts:
        o_ext = jnp.dot(jnp.concatenate([vts[group(part)], ones], axis=0), p_ref[part, 0:nk, :],
                        preferred_element_type=F32)
        scaled = o_ext[0:d] / (o_ext[d:d + 1] + jnp.exp2(sinks[part] - tops[part]))
        heads += [scaled[:, r * w:(r + 1) * w] for r in range(ph)]
    o_t = jnp.concatenate(heads, axis=0)
    yield

    y_t = o_t * lax.rsqrt(jnp.mean(o_t * o_t, axis=0, keepdims=True) + 1e-6) * og_ref[...]
    out = lax.dot_general(eye_ref[...], y_t.astype(BF16), (((1,), (1,)), ((), ())), preferred_element_type=F32)
    out_ref[0, cols, :] = out.astype(BF16)


def _run_stages(blocks):
    for t in range(len(blocks) + 2):
        for u in (t, t - 1, t - 2, t - 1):
            if 0 <= u < len(blocks):
                next(blocks[u], None)


def _swa_kernel(qt_ref, k_ref, vt_ref, sink_ref, og_ref, eye_ref, out_ref, s_ref, p_ref, *, t_lat, t_all, t_out, sb):
    w = SWA_WINDOW
    step = pl.program_id(1)
    n_lat = t_lat // w
    ctx_rows = (t_lat, t_all - t_lat)
    attend = functools.partial(_swa_attend, qt_ref, k_ref, vt_ref, sink_ref, og_ref, eye_ref, out_ref, s_ref, p_ref)

    def run(latent):
        blk = lambda i: (pl.multiple_of(i * w, w), w)
        sk = lax.broadcasted_iota(jnp.int32, (w, w), 0)
        tq = lax.broadcasted_iota(jnp.int32, (w, w), 1)
        neg = jnp.full((w, w), -jnp.inf, F32)
        zero = jnp.zeros((w, w), F32)
        blocks = []
        for u, is_latent in enumerate(latent):
            if not is_latent:
                blocks.append(attend(u, [ctx_rows], [None]))
                continue
            n = step * sb + u
            b_prev = jnp.where((sk >= tq) & (n >= 1), zero, neg)
            b_next = jnp.where((sk <= tq) & (n < n_lat - 1), zero, neg)
            blocks.append(attend(u, [ctx_rows, blk(jnp.maximum(n - 1, 0)), blk(n), blk(jnp.minimum(n + 1, n_lat - 1))],
                                 [None, b_prev, None, b_next]))
        _run_stages(blocks)

    n_steps = t_out // (sb * w)
    makeup = [tuple(s * sb + u < n_lat for u in range(sb)) for s in range(n_steps)]
    for kind in sorted(set(makeup), reverse=True):
        steps = [s for s in range(n_steps) if makeup[s] == kind]
        assert steps == list(range(steps[0], steps[-1] + 1))
        pl.when((step >= steps[0]) & (step <= steps[-1]))(functools.partial(run, kind))


def _swa(sqt, sk, svt, sink_t, out_g, eye, *, t_lat, t_out, sb):
    bsz, t_all, _ = sk.shape
    w = SWA_WINDOW
    n_keys = t_all - t_lat + 3 * w
    n_parts = SWA_HEADS // SWA_PART_HEADS
    assert t_out % (sb * w) == 0
    return pl.pallas_call(
        functools.partial(_swa_kernel, t_lat=t_lat, t_all=t_all, t_out=t_out, sb=sb),
        grid=(bsz, t_out // (sb * w)),
        in_specs=[pl.BlockSpec((1, SWA_Q, sb * w), lambda b, n: (b, 0, n)),
                  pl.BlockSpec((1, t_all, SWA_KV), lambda b, n: (b, 0, 0)),
                  pl.BlockSpec((1, SWA_KV, t_all), lambda b, n: (b, 0, 0)),
                  _const_spec(sink_t.shape), _const_spec(out_g.shape), _const_spec(eye.shape)],
        out_specs=pl.BlockSpec((1, sb * w, SWA_Q), lambda b, n: (b, n, 0)),
        out_shape=jax.ShapeDtypeStruct((bsz, t_out, SWA_Q), BF16),
        scratch_shapes=[pltpu.VMEM((sb, n_parts, n_keys, SWA_PART_HEADS * w), F32),
                        pltpu.VMEM((sb, n_parts, n_keys, SWA_PART_HEADS * w), BF16)],
        compiler_params=_cparams("parallel", "parallel"),
        name="swa",
    )(sqt, sk, svt, sink_t, out_g, eye)


FFN_CHUNKS = ((0, 1536), (1536, 1280))


def _post_kernel(*refs, t_lat, tm, ctx_row, n_stream, ctx_separate):
    a_ref, b_ref, c_ref, mod_ref, n1_ref, n2a_ref, n2b_ref, wo_ref, wgu_ref, wd_ref, out_ref = refs[n_stream:]
    b = pl.program_id(0)
    j = pl.program_id(1)
    x = _stream_tile(refs[:n_stream], j, tm=tm, t_lat=t_lat, ctx_separate=ctx_separate)
    mod = functools.partial(_mod_vectors, mod_ref, b, j, t_lat=t_lat, tm=tm, ctx_row=ctx_row)
    split = _row_split(t_lat, tm)
    gate1, shift2, scale2, gate2 = mod(2), mod(3), mod(4), mod(5)
    g1 = [n1_ref[...] * g for g in gate1]
    g2a = [n2a_ref[...] * (1.0 + s) for s in scale2]
    g2b = [n2b_ref[...] * g for g in gate2]

    cat = jnp.concatenate([a_ref[0], b_ref[0], c_ref[0]], axis=-1)
    y = jnp.dot(cat, wo_ref[...], preferred_element_type=F32)
    x1 = _by_rows(lambda p, x, y: x + _rms(y, g1[p]), split, x, y)
    h = _by_rows(lambda p, x: (_rms(x, g2a[p]) + shift2[p]).astype(BF16), split, x1)
    f = None
    for off, n in FFN_CHUNKS:
        gt = jnp.dot(h, wgu_ref[:, off:off + n], preferred_element_type=F32)
        up = jnp.dot(h, wgu_ref[:, D_FF + off:D_FF + off + n], preferred_element_type=F32)
        act = (_silu(gt) * up).astype(BF16)
        part = jnp.dot(act, wd_ref[off:off + n, :], preferred_element_type=F32)
        f = part if f is None else f + part
    out_ref[0] = _by_rows(lambda p, x, f: x + _rms(f, g2b[p]), split, x1, f)


def _post(a, bo, c, x, ctx, mod, n1_post, n2_pre, n2_post, wo, wgu, wd, *, t_lat, t_out, tm, ctx_row):
    bsz = x.shape[0]
    stream, stream_specs = _stream_specs(x, ctx, tm)
    tok = lambda n: pl.BlockSpec((1, tm, n), lambda b, j: (b, j, 0))
    once = lambda arr: pl.BlockSpec(arr.shape, lambda b, j: (0,) * arr.ndim, pipeline_mode=pl.Buffered(1))
    return pl.pallas_call(
        functools.partial(_post_kernel, t_lat=t_lat, tm=tm, ctx_row=ctx_row, n_stream=len(stream),
                          ctx_separate=ctx is not None),
        grid=(bsz, t_out // tm),
        in_specs=stream_specs + [tok(GLA_V), tok(GMLP_WIDTH), tok(SWA_Q), _const_spec(mod.shape),
                                 _const_spec(n1_post.shape), _const_spec(n2_pre.shape), _const_spec(n2_post.shape),
                                 once(wo), once(wgu), once(wd)],
        out_specs=tok(D_MODEL),
        out_shape=jax.ShapeDtypeStruct((bsz, t_out, D_MODEL), F32),
        compiler_params=_cparams("parallel", "parallel"),
        name="post",
    )(*stream, a, bo, c, mod, n1_post, n2_pre, n2_post, wo, wgu, wd)


def _rope_tables(t_lat, t_ctx):
    rows = t_lat // GRID_W
    row = jnp.repeat(jnp.arange(rows), GRID_W).astype(F32)
    col = jnp.tile(jnp.arange(GRID_W), rows).astype(F32)
    inv_freq = jnp.power(ROPE_THETA, -jnp.arange(0, ROPE_AXIS_DIM, 2, dtype=F32) / ROPE_AXIS_DIM)
    ang_row = row[:, None] * inv_freq[None, :]
    ang_col = col[:, None] * inv_freq[None, :]
    ang = jnp.concatenate([ang_row, ang_row, ang_col, ang_col], axis=-1)
    sign = jnp.tile(jnp.concatenate([-jnp.ones((16,), F32), jnp.ones((16,), F32)]), 2)
    cos = jnp.tile(jnp.concatenate([jnp.cos(ang), jnp.ones((t_ctx, HEAD_DIM), F32)], axis=0), (1, 2))
    sin = jnp.tile(jnp.concatenate([jnp.sin(ang) * sign, jnp.zeros((t_ctx, HEAD_DIM), F32)], axis=0), (1, 2))
    return cos, sin, cos.T, sin.T


def _inproj_weights(w, wa2, ba):
    r = GLA_GATE_RANK
    o_code = 2 * GLA_QK + 2 * GLA_V
    o_gm = o_code + 2 * r
    o_sq = o_gm + 2 * GMLP_WIDTH
    o_sk = o_sq + SWA_Q
    o_sv = o_sk + SWA_KV
    code = jnp.concatenate([w[:, o_code:o_gm], jnp.zeros((w.shape[0], COL_CODE - 2 * r), w.dtype)], axis=1)
    w_tok = jnp.concatenate([w[:, 0:o_code], w[:, o_gm:o_sq], w[:, o_sk:o_sv], code], axis=1)
    w_t = jnp.concatenate([w[:, o_sq:o_sk], w[:, o_sv:], w[:, GLA_QK:2 * GLA_QK]], axis=1).T
    wa = jnp.zeros((COL_CODE, 2 * GLA_QK), F32)
    wa = wa.at[0:r, 0:GLA_QK].set(wa2[0]).at[r:2 * r, GLA_QK:].set(wa2[1])
    return w_tok.astype(BF16), w_t.astype(BF16), wa.astype(BF16), ba.reshape(1, 2 * GLA_QK)


def _gla_consts():
    a3, a3t_neg, mask_w, bd_c = _gla_fast_tables()
    amat, mask = _gla_tables()
    lane_head = np.arange(GLA_QK)[None, :] // GLA_DK
    row_head = np.arange(GLA_HEADS * GLA_CHUNK)[:, None] // GLA_CHUNK
    hq = (lane_head == row_head).astype(np.float32)
    vlane_head = np.arange(GLA_V)[None, :] // GLA_DV
    bdv = (vlane_head == row_head).astype(np.float32)
    bds = (vlane_head == (np.arange(GLA_QK)[:, None] // GLA_DK)).astype(np.float32)
    ind = np.tile((vlane_head == vlane_head.T).astype(np.float32) / GLA_DV, (2, 1))
    return (jnp.asarray(a3, BF16), jnp.asarray(a3t_neg, BF16), jnp.asarray(mask_w, BF16),
            jnp.asarray(bd_c, F32), jnp.asarray(amat, BF16), jnp.asarray(mask, F32), jnp.asarray(hq, BF16),
            jnp.asarray(bdv, F32), jnp.asarray(bds, F32), jnp.asarray(ind, BF16))


def kernel(x, c, ctx, c_ctx, mod_w, mod_b, n1_pre, n1_post, n2_pre, n2_post, w_in, w_out, gla_wa2, gla_ba,
           gla_norm, gmlp_ln_g, gmlp_ln_b, gmlp_ws, gmlp_bs, gmlp_out_g, swa_sink, swa_out_g, ffn_w_gu,
           ffn_w_down):
    bsz, t_lat, d = x.shape
    t_ctx = ctx.shape[1]
    t_all = t_lat + t_ctx
    depth = mod_w.shape[0]
    assert d == D_MODEL and bsz < MOD_ROWS
    assert t_lat % 1024 == 0 and t_ctx % 256 == 0 and t_all % 768 == 0
    ctx_row = bsz
    tm_all = 768
    assert t_ctx == tm_all - _row_split(t_lat, tm_all)

    cc = jnp.zeros((MOD_ROWS, d), F32).at[0:bsz].set(c).at[ctx_row].set(c_ctx)
    mods = _modulation(cc, mod_w, mod_b)
    rope = _rope_tables(t_lat, t_ctx)
    gla_consts = _gla_consts()
    eye = jnp.eye(SWA_WINDOW, dtype=BF16)
    gmask = jnp.asarray((np.arange(GMLP_WIDTH)[None, :] // GMLP_GDIM ==
                         np.arange(GMLP_GROUPS * GMLP_CHUNK)[:, None] // GMLP_CHUNK).astype(np.float32))
    row = lambda v: v.reshape(1, -1)

    assert t_ctx == STREAM_PIECE
    xs, xs_ctx = x, ctx
    for l in range(depth):
        last = l == depth - 1
        t_out = t_lat if last else t_all
        gla, lg3, gm, sk, sqt, svt, kt, lgt3, gate_totals = _inproj(
            xs, xs_ctx, mods[l], row(n1_pre[l]), _inproj_weights(w_in[l], gla_wa2[l], gla_ba[l]), rope,
            t_lat=t_lat, t_all=t_all, tm=tm_all, ctx_row=ctx_row)
        a_out = _gla(gla, lg3, kt, lgt3, gate_totals, gla_consts, row(jnp.tile(gla_norm[l], GLA_HEADS)),
                     t_lat=t_lat, t_out=t_out)
        ws_s = gmlp_ws[l].reshape(GMLP_GROUPS * GMLP_CHUNK, GMLP_CHUNK).astype(BF16)
        bs_t = jnp.repeat(gmlp_bs[l].T, GMLP_GDIM, axis=1)
        b_out = _gmlp(gm, row(gmlp_ln_g[l]), row(gmlp_ln_b[l]), ws_s, bs_t, gmask, row(gmlp_out_g[l]),
                      t_out=t_out, tm=1024 if last else tm_all)
        sink_t = row(jnp.repeat(swa_sink[l], SWA_WINDOW))
        out_g_t = jnp.broadcast_to(swa_out_g[l][:, None], (SWA_Q, SWA_WINDOW))
        sb = next(n for n in SWA_STEP_BLOCKS if (t_out // SWA_WINDOW) % n == 0)
        c_out = _swa(sqt, sk, svt, sink_t, out_g_t, eye, t_lat=t_lat, t_out=t_out, sb=sb)
        xs = _post(a_out, b_out, c_out, xs, xs_ctx, mods[l], row(n1_post[l]), row(n2_pre[l]), row(n2_post[l]),
                   w_out[l].astype(BF16), ffn_w_gu[l].astype(BF16), ffn_w_down[l].astype(BF16),
                   t_lat=t_lat, t_out=t_out, tm=1024 if last else tm_all, ctx_row=ctx_row)
        xs_ctx = None
    return xs
```

```python
import functools

import numpy as np
import jax
import jax.numpy as jnp
from jax import lax
from jax.experimental import pallas as pl
from jax.experimental.pallas import tpu as pltpu

F32 = jnp.float32
BF16 = jnp.bfloat16

D_MODEL = 1024
GRID_W = 64
HEAD_DIM = 64
GLA_HEADS = 4
GLA_DK = 32
GLA_DV = 64
GLA_QK = GLA_HEADS * GLA_DK
GLA_V = GLA_HEADS * GLA_DV
GLA_GATE_RANK = 16
GLA_GATE_TAU = 16.0
GLA_CHUNK = 64
GMLP_GROUPS = 4
GMLP_GDIM = 64
GMLP_WIDTH = GMLP_GROUPS * GMLP_GDIM
GMLP_CHUNK = 128
SWA_HEADS = 8
SWA_KV_HEADS = 2
SWA_REP = SWA_HEADS // SWA_KV_HEADS
SWA_Q = SWA_HEADS * HEAD_DIM
SWA_KV = SWA_KV_HEADS * HEAD_DIM
SWA_WINDOW = 128
ROPE_AXIS_DIM = HEAD_DIM // 2
ROPE_THETA = 10000.0
MIX_WIDTH = GLA_V + GMLP_WIDTH + SWA_Q
D_FF = -(-8 * D_MODEL // (3 * 256)) * 256
N_MOD = 6
MOD_ROWS = 24

COL_GLA = 2 * GLA_QK + 2 * GLA_V
COL_GM = 2 * GMLP_WIDTH
COL_SK = SWA_KV
COL_CODE = 128
OFF_GM = COL_GLA
OFF_SK = OFF_GM + COL_GM
OFF_CODE = OFF_SK + COL_SK
ROW_SQ = 0
ROW_SV = ROW_SQ + SWA_Q
ROW_GK = ROW_SV + SWA_KV

LOG2E = 1.4426950408889634
SWA_Q_SCALE = HEAD_DIM ** -0.5 * LOG2E
SWA_KEY_TILE = 64
SWA_PART_HEADS = 2
SWA_STEP_BLOCKS = (8, 6, 4, 3)
SWA_DEN_ROWS = 16

VMEM_LIMIT = 56 * 1024 * 1024
GLA_LEVELS = (1, 2, 4, 8, 16, 32)


def _cparams(*sem):
    return pltpu.CompilerParams(dimension_semantics=sem, vmem_limit_bytes=VMEM_LIMIT)


def _const_spec(shape):
    nd = len(shape)
    return pl.BlockSpec(shape, lambda *_: (0,) * nd)


def _rms(x, g, eps=1e-6):
    return x * lax.rsqrt(jnp.mean(x * x, axis=-1, keepdims=True) + eps) * g


def _silu(x):
    return x * jax.nn.sigmoid(x)


def _row_split(t_lat, tm):
    return t_lat % tm if t_lat % tm else tm


def _mod_vectors(mod_ref, b, j, idx, *, t_lat, tm, ctx_row):
    lo = idx * D_MODEL
    lat = mod_ref[pl.ds(b, 1), lo:lo + D_MODEL]
    ctx = mod_ref[ctx_row:ctx_row + 1, lo:lo + D_MODEL]
    return lat, jnp.where(j * tm + _row_split(t_lat, tm) >= t_lat, ctx, lat)


STREAM_PIECE = 256


def _stream_specs(x, ctx, tm, tile=lambda b, j: (b, j), **spec_args):
    rows, d = x.shape[1:]
    per_tile = tm // STREAM_PIECE
    last = rows // STREAM_PIECE - 1

    def piece(k):
        def index(b, j):
            tb, tj = tile(b, j)
            return tb, jnp.minimum(tj * per_tile + k, last), 0
        return pl.BlockSpec((1, STREAM_PIECE, d), index, **spec_args)

    operands, specs = [x] * per_tile, [piece(k) for k in range(per_tile)]
    if ctx is not None:
        assert ctx.shape[1] == STREAM_PIECE
        operands.append(ctx)
        specs.append(pl.BlockSpec((1, STREAM_PIECE, d), lambda b, j: (tile(b, j)[0], 0, 0)))
    return operands, specs


def _stream_tile(refs, j, *, tm, t_lat, ctx_separate):
    pieces = [r[0] for r in refs[:tm // STREAM_PIECE]]
    if ctx_separate:
        pieces[-1] = jnp.where((j + 1) * tm - STREAM_PIECE >= t_lat, refs[-1][0], pieces[-1])
    return jnp.concatenate(pieces, axis=0)


def _by_rows(fn, split, *arrays):
    tm = arrays[0].shape[0]
    if split == tm:
        return fn(0, *arrays)
    return jnp.concatenate([fn(0, *[a[0:split] for a in arrays]), fn(1, *[a[split:tm] for a in arrays])], axis=0)


def _mod_kernel(c_ref, w_ref, b_ref, o_ref):
    s = _silu(c_ref[...]).astype(BF16)
    o_ref[0] = jnp.dot(s, w_ref[0].astype(BF16), preferred_element_type=F32) + b_ref[0]


def _modulation(cc, mod_w, mod_b):
    depth, d, n = mod_w.shape
    tn = n // 4
    return pl.pallas_call(
        _mod_kernel,
        grid=(depth, n // tn),
        in_specs=[pl.BlockSpec((MOD_ROWS, d), lambda l, j: (0, 0)),
                  pl.BlockSpec((1, d, tn), lambda l, j: (l, 0, j)),
                  pl.BlockSpec((1, 1, tn), lambda l, j: (l, 0, j))],
        out_specs=pl.BlockSpec((1, MOD_ROWS, tn), lambda l, j: (l, 0, j)),
        out_shape=jax.ShapeDtypeStruct((depth, MOD_ROWS, n), F32),
        compiler_params=_cparams("parallel", "parallel"),
        name="modulation",
    )(cc, mod_w, mod_b.reshape(depth, 1, n))


def _rope(z, cos, sin):
    lane = lax.broadcasted_iota(jnp.int32, z.shape, 1)
    first = (lane % 32) < 16
    rot = jnp.where(first, pltpu.roll(z, 128 - 16, 1), pltpu.roll(z, 16, 1))
    return z * cos + rot * sin


GATE_PIECES = 2
GATE_TOTAL_ROWS = 8


def _split(x):
    p1 = x.astype(BF16)
    return p1, (x - p1.astype(F32)).astype(BF16)


def _log_gate(z):
    return (jnp.minimum(z, 0.0) - jnp.log(1.0 + jnp.exp(-jnp.abs(z)))) * (1.0 / GLA_GATE_TAU)


def _rope_t(z, cos_t, sin_t):
    parts = []
    for r0 in range(0, z.shape[0], 32):
        parts += [z[r0 + 16:r0 + 32], z[r0:r0 + 16]]
    rot = jnp.concatenate(parts, axis=0)
    reps = z.shape[0] // cos_t.shape[0]
    return z * jnp.concatenate([cos_t] * reps, axis=0) + rot * jnp.concatenate([sin_t] * reps, axis=0)


def _inproj_kernel(*refs, t_lat, tm, ctx_row, n_stream, ctx_separate):
    (mod_ref, g_ref, w_ref, wt_ref, wa_ref, ba_ref, cos_ref, sin_ref, cost_ref, sint_ref,
     gla_ref, lg_ref, gm_ref, sk_ref, sqt_ref, svt_ref, kt_ref, lgt_ref, tot_ref, h_buf, ht_buf) = refs[2 * n_stream:]
    b = pl.program_id(0)
    j = pl.program_id(1)
    nt = pl.num_programs(1)
    step = b * nt + j
    slot = lax.rem(step, 2)

    def prologue(stream_refs, tb, tj, dst):
        x = _stream_tile(stream_refs, tj, tm=tm, t_lat=t_lat, ctx_separate=ctx_separate)
        mod = functools.partial(_mod_vectors, mod_ref, tb, tj, t_lat=t_lat, tm=tm, ctx_row=ctx_row)
        shift, scale, gain = mod(0), mod(1), g_ref[...]
        gains = [gain * (1.0 + s) for s in scale]
        hf = _by_rows(lambda part, x: _rms(x, gains[part]) + shift[part], _row_split(t_lat, tm), x)
        h_buf[dst] = hf.astype(BF16)
        ht_buf[dst] = hf.T.astype(BF16)

    @pl.when(step == 0)
    def _first():
        prologue(refs[n_stream:2 * n_stream], 0, 0, 0)

    def project(cur):
        def proj(off, n):
            return jnp.dot(h_buf[cur], w_ref[:, off:off + n], preferred_element_type=F32)

        def proj_t(off, n):
            return jnp.dot(wt_ref[off:off + n, :], ht_buf[cur], preferred_element_type=F32)

        code = proj(OFF_CODE, COL_CODE).astype(BF16)
        z = proj(0, COL_GLA)
        gla_ref[0, :, 0:GLA_QK] = (z[:, 0:GLA_QK] * (GLA_DK ** -0.5)).astype(BF16)
        gla_ref[0, :, GLA_QK:COL_GLA] = z[:, GLA_QK:COL_GLA].astype(BF16)

        n2 = 2 * GLA_QK
        lg = _log_gate(jnp.dot(code, wa_ref[...], preferred_element_type=F32) + ba_ref[...])
        for i, p in enumerate(_split(lg)):
            lg_ref[0, :, i * n2:(i + 1) * n2] = p
        for i, p in enumerate(_split(lg.T)):
            lgt_ref[0, i * n2:(i + 1) * n2, :] = p
        c = GLA_FAST_CHUNK
        totals = [jnp.sum(lg[i * c:(i + 1) * c], axis=0, keepdims=True) for i in range(tm // c)]
        tot_ref[0, 0] = jnp.concatenate(totals + [jnp.zeros((GATE_TOTAL_ROWS - tm // c, n2), F32)], axis=0)

        gm_ref[0] = proj(OFF_GM, COL_GM).astype(BF16)
        nb, nj = _next_tile(b, j, pl.num_programs(0), nt)
        prologue(refs[:n_stream], nb, nj, 1 - cur)
        kt_ref[0] = proj_t(ROW_GK, GLA_QK).astype(BF16)

        sk_ref[0] = _rope(proj(OFF_SK, COL_SK), cos_ref[...], sin_ref[...]).astype(BF16)
        sqt_ref[0] = (_rope_t(proj_t(ROW_SQ, SWA_Q), cost_ref[...], sint_ref[...]) * SWA_Q_SCALE).astype(BF16)
        svt_ref[0] = proj_t(ROW_SV, SWA_KV).astype(BF16)

    for cur in range(2):
        pl.when(slot == cur)(functools.partial(project, cur))


def _next_tile(b, j, nb, nt):
    wrap = j + 1 == nt
    last = jnp.logical_and(wrap, b + 1 == nb)
    return jnp.where(wrap & jnp.logical_not(last), b + 1, b), jnp.where(last, j, jnp.where(wrap, 0, j + 1))


def _inproj(x, ctx, mod, n_pre, weights, rope, *, t_lat, t_all, tm, ctx_row):
    bsz, _, d = x.shape
    nt = t_all // tm
    nxt, nxt_specs = _stream_specs(x, ctx, tm, functools.partial(_next_tile, nb=bsz, nt=nt))
    first, first_specs = _stream_specs(x, ctx, tm, lambda b, j: (0, 0))
    stream, stream_specs = nxt + first, nxt_specs + first_specs
    tok = lambda n: pl.BlockSpec((1, tm, n), lambda b, j: (b, j, 0))
    tok_t = lambda n: pl.BlockSpec((1, n, tm), lambda b, j: (b, 0, j))
    n2 = 2 * GLA_QK
    cols = (COL_GLA, GATE_PIECES * n2, COL_GM, COL_SK)
    rows = (SWA_Q, SWA_KV, GLA_QK, GATE_PIECES * n2)
    out_shape = [jax.ShapeDtypeStruct((bsz, t_all, n), BF16) for n in cols]
    out_shape += [jax.ShapeDtypeStruct((bsz, n, t_all), BF16) for n in rows]
    assert tm // GLA_FAST_CHUNK <= GATE_TOTAL_ROWS
    out_shape.append(jax.ShapeDtypeStruct((bsz, nt, GATE_TOTAL_ROWS, n2), F32))
    totals_spec = pl.BlockSpec((1, 1, GATE_TOTAL_ROWS, n2), lambda b, j: (b, j, 0, 0))
    consts = (mod, n_pre) + tuple(weights)
    cos, sin, cos_t, sin_t = rope
    return pl.pallas_call(
        functools.partial(_inproj_kernel, t_lat=t_lat, tm=tm, ctx_row=ctx_row, n_stream=len(nxt),
                          ctx_separate=ctx is not None),
        grid=(bsz, nt),
        in_specs=stream_specs + [_const_spec(a.shape) for a in consts] +
                 [pl.BlockSpec((tm, 128), lambda b, j: (j, 0)), pl.BlockSpec((tm, 128), lambda b, j: (j, 0)),
                  pl.BlockSpec((128, tm), lambda b, j: (0, j)), pl.BlockSpec((128, tm), lambda b, j: (0, j))],
        out_specs=[tok(n) for n in cols] + [tok_t(n) for n in rows] + [totals_spec],
        out_shape=out_shape,
        scratch_shapes=[pltpu.VMEM((2, tm, d), BF16), pltpu.VMEM((2, d, tm), BF16)],
        compiler_params=_cparams("arbitrary", "arbitrary"),
        name="inproj",
    )(*stream, *consts, cos, sin, cos_t, sin_t)


GLA_FAST_CHUNK = 128
GLA_FAST_UNROLL = 6
GLA_SAFE_LOG_DECAY = -40.0


def _gla_fast_tables():
    c = GLA_FAST_CHUNK
    t = np.arange(c)[:, None]
    r = np.arange(c)[None, :]
    incl = [(r <= t), (r >= t)]
    a3 = np.stack([np.tile(m, (1, GATE_PIECES)) for m in incl]).astype(np.float32)
    a3t_neg = -np.stack([np.tile(m.T, (GATE_PIECES, 1)) for m in incl]).astype(np.float32)
    mask_w = np.stack([np.tile(m, (1, GLA_HEADS)) for m in incl]).astype(np.float32)
    half = (np.arange(128)[None, :] // GLA_DV) == ((np.arange(GLA_QK)[:, None] // GLA_DK) % 2)
    return a3, a3t_neg, mask_w, half.astype(np.float32)


def _gla_fast_chunks(jobs, a3, a3t_neg, mask_w, bd_c, states):
    c = GLA_FAST_CHUNK
    dk, hd = GLA_DK, GLA_QK
    cums = [(jnp.dot(a3[d], g3, preferred_element_type=F32),
             jnp.dot(gt3, a3t_neg[d], preferred_element_type=F32))
            for d, _, _, _, g3, gt3 in jobs]

    zr = lambda n: jnp.zeros((n, c), BF16)
    qes, dcols, khs, atts = [], [], [], []
    for (d, q, kt, _, _, _), (cum, ncum_t) in zip(jobs, cums):
        qe = (q.astype(F32) * jnp.exp(cum)).astype(BF16)
        ke_t = kt.astype(F32) * jnp.exp(ncum_t)
        edge = 0 if d else c - 1
        dcol = jnp.exp(-ncum_t[:, edge:edge + 1])
        khs.append((ke_t * dcol).astype(BF16))
        ke_t = ke_t.astype(BF16)
        cols = []
        for hh in range(GLA_HEADS):
            parts = ([zr(hh * dk)] if hh else []) + [ke_t[hh * dk:(hh + 1) * dk]]
            parts += [zr(hd - (hh + 1) * dk)] if hh < GLA_HEADS - 1 else []
            cols.append(jnp.concatenate(parts, axis=0))
        kbd = jnp.concatenate(cols, axis=1)
        atts.append(jnp.dot(qe, kbd, preferred_element_type=F32))
        qes.append(qe)
        dcols.append(dcol)

    kvs = [jnp.concatenate([jnp.dot(kh[0:hd // 2], v[:, 0:128], preferred_element_type=F32),
                            jnp.dot(kh[hd // 2:hd], v[:, 128:256], preferred_element_type=F32)], axis=0)
           for (_, _, _, v, _, _), kh in zip(jobs, khs)]

    lane = lax.broadcasted_iota(jnp.int32, (1, 128), 1)
    m_lo = (lane < GLA_DV).astype(BF16)
    m_hi = (lane >= GLA_DV).astype(BF16)
    z128 = jnp.zeros((c, 128), BF16)
    z64 = jnp.zeros((hd // 2, 128), BF16)
    states = list(states)
    outs = []
    for (d, _, _, v, _, _), qe, dcol, att, kv in zip(jobs, qes, dcols, atts, kvs):
        v_lo, v_hi = v[:, 0:128], v[:, 128:256]
        vbd = jnp.concatenate([jnp.concatenate([v_lo * m_lo, z128], axis=1),
                               jnp.concatenate([v_lo * m_hi, z128], axis=1),
                               jnp.concatenate([z128, v_hi * m_lo], axis=1),
                               jnp.concatenate([z128, v_hi * m_hi], axis=1)], axis=0)
        sb = states[d].astype(BF16)
        s_full = jnp.concatenate([jnp.concatenate([sb[0:hd // 2], z64], axis=1),
                                  jnp.concatenate([z64, sb[hd // 2:hd]], axis=1)], axis=0)
        lhs = jnp.concatenate([att.astype(BF16) * mask_w[d], qe], axis=1)
        outs.append(jnp.dot(lhs, jnp.concatenate([vbd, s_full], axis=0),
                            preferred_element_type=F32))
        states[d] = states[d] * dcol + kv * bd_c
    return outs, states


def _gla_tables():
    c = GLA_CHUNK
    t = np.arange(c)[:, None]
    r = np.arange(c)[None, :]
    sizes = [2 * b for b in GLA_LEVELS]
    same = lambda b: (t // b) == (r // b)
    a_f = [same(b) & (r <= t) for b in sizes] + [same(b) & (r > t) for b in sizes]
    a_b = [same(b) & (r >= t) for b in sizes] + [same(b) & (r < t) for b in sizes]
    m_f = [((t // b) % 2 == 1) & ((r // b) == (t // b) - 1) for b in GLA_LEVELS] + [t == r]
    m_b = [((t // b) % 2 == 0) & ((r // b) == (t // b) + 1) for b in GLA_LEVELS] + [t == r]
    amat = np.stack([np.tile(np.concatenate(a, 0), (1, GATE_PIECES)) for a in (a_f, a_b)]).astype(np.float32)
    mask = np.stack([np.stack([np.tile(m, (GLA_HEADS, 1)) for m in ms]) for ms in (m_f, m_b)])
    return amat, mask.astype(np.float32)


def _gla_chunk(q, k, v, g3, amat, masks, hmask_q, bd_v, bd_s, s_prev, *, backward):
    c = GLA_CHUNK
    nl = len(GLA_LEVELS)
    g = sum(g3[i * c:(i + 1) * c].astype(F32) for i in range(GATE_PIECES))
    ps = jnp.dot(amat, g3, preferred_element_type=F32)
    blk = lambda i: ps[i * c:(i + 1) * c]
    q_exp = [g] + [blk(i) for i in range(nl - 1)]
    k_exp = [None] + [blk(nl + i) for i in range(nl - 1)]
    q_full = blk(nl - 1)
    k_full = blk(2 * nl - 1)

    att = None
    for lvl in range(nl + 1):
        if lvl < nl:
            qe = q * jnp.exp(q_exp[lvl])
            ke = k if k_exp[lvl] is None else k * jnp.exp(k_exp[lvl])
        else:
            qe, ke = q, k
        qbd = jnp.concatenate([qe.astype(BF16)] * GLA_HEADS, axis=0) * hmask_q
        p = lax.dot_general(qbd, ke.astype(BF16), (((1,), (1,)), ((), ())), preferred_element_type=F32)
        p = p * masks[lvl]
        att = p if att is None else att + p

    r = jnp.dot(att.astype(BF16), v, preferred_element_type=F32) * bd_v
    o = r[0:c]
    for hh in range(1, GLA_HEADS):
        o = o + r[hh * c:(hh + 1) * c]
    o = o + jnp.dot((q * jnp.exp(q_full)).astype(BF16), s_prev.astype(BF16), preferred_element_type=F32)

    tot = q_full[0:1] if backward else q_full[c - 1:c]
    dcol = jnp.transpose(jnp.broadcast_to(jnp.exp(tot), (GLA_QK, GLA_QK)))
    dcol = jnp.concatenate([dcol, dcol], axis=1)
    kv = lax.dot_general((k * jnp.exp(k_full)).astype(BF16), v, (((0,), (0,)), ((), ())),
                         preferred_element_type=F32)
    return o, s_prev * dcol + kv * bd_s


def _gla_kernel(gla_ref, lg_ref, kt_ref, lgt_ref, tot_ref, a3_ref, a3t_ref, mw_ref, bdc_ref, amat_ref, mask_ref,
                hq_ref, bdv_ref, bds_ref, ind_ref, norm_ref, out_ref, of_ref, ob_ref, *, t_lat, t_all, t_out):
    n2 = 2 * GLA_QK
    o_v = 2 * GLA_QK

    def pieces(rows, d):
        return jnp.concatenate([lg_ref[0, rows, i * n2 + d * GLA_QK:i * n2 + (d + 1) * GLA_QK]
                                for i in range(GATE_PIECES)], axis=0)

    safe = jnp.min(tot_ref[0]) >= GLA_SAFE_LOG_DECAY

    @pl.when(safe)
    def _fast():
        c = GLA_FAST_CHUNK
        n_lat = t_lat // c
        n_all = t_all // c
        per_step = GLA_FAST_UNROLL
        assert n_all % per_step == 0

        def job(ci, d):
            rows = pl.ds(pl.multiple_of(ci * c, c), c)
            gt3 = jnp.concatenate([lgt_ref[0, i * n2 + d * GLA_QK:i * n2 + (d + 1) * GLA_QK, rows]
                                   for i in range(GATE_PIECES)], axis=1)
            return rows, (d, gla_ref[0, rows, 0:GLA_QK], kt_ref[0, :, rows], gla_ref[0, rows, o_v:o_v + GLA_V],
                          pieces(rows, d), gt3)

        def step(i, carry):
            chunks = [(lax.rem(i * per_step + u + n_lat, n_all), 0) for u in range(per_step)]
            chunks += [(n_all - 1 - (i * per_step + u), 1) for u in range(per_step)]
            rows, jobs = zip(*[job(ci, d) for ci, d in chunks])
            outs, states = _gla_fast_chunks(jobs, a3_ref, a3t_ref, mw_ref, bdc_ref[...], carry)
            for (_, d), r, o in zip(chunks, rows, outs):
                (ob_ref if d else of_ref)[r, :] = o
            return tuple(states)

        zero = jnp.zeros((GLA_QK, 128), F32)
        lax.fori_loop(0, n_all // per_step, step, (zero, zero))

    @pl.when(jnp.logical_not(safe))
    def _robust():
        c = GLA_CHUNK
        n_lat = t_lat // c
        n_all = t_all // c
        hq = hq_ref[...]
        bdv = bdv_ref[...]
        bds = bds_ref[...]

        def one(ci, d, s):
            start = pl.multiple_of(ci * c, c)
            rows = pl.ds(start, c)
            q = gla_ref[0, rows, 0:GLA_QK].astype(F32)
            k = gla_ref[0, rows, GLA_QK:2 * GLA_QK].astype(F32)
            v = gla_ref[0, rows, o_v:o_v + GLA_V]
            o, s = _gla_chunk(q, k, v, pieces(rows, d), amat_ref[d],
                              [mask_ref[d, l] for l in range(len(GLA_LEVELS) + 1)], hq, bdv, bds, s,
                              backward=bool(d))
            (ob_ref if d else of_ref)[rows, :] = o
            return s

        def step(i, carry):
            s_f, s_b = carry
            return one(lax.rem(i + n_lat, n_all), 0, s_f), one(n_all - 1 - i, 1, s_b)

        zero = jnp.zeros((GLA_QK, GLA_V), F32)
        lax.fori_loop(0, n_all, step, (zero, zero))

    tr = 256
    ind = ind_ref[...]
    for r0 in range(0, t_out, tr):
        o = of_ref[r0:r0 + tr, :] + ob_ref[r0:r0 + tr, :]
        sq = o * o
        hi = sq.astype(BF16)
        lo = (sq - hi.astype(F32)).astype(BF16)
        ms = jnp.dot(jnp.concatenate([hi, lo], axis=1), ind, preferred_element_type=F32)
        gate = gla_ref[0, r0:r0 + tr, o_v + GLA_V:COL_GLA].astype(F32)
        y = o * lax.rsqrt(ms + 1e-6) * norm_ref[...] * _silu(gate)
        out_ref[0, r0:r0 + tr, :] = y.astype(BF16)


def _gla(gla, lg3, kt, lgt3, totals, consts, norm_t, *, t_lat, t_out):
    bsz, t_all, _ = gla.shape
    n2 = 2 * GLA_QK
    return pl.pallas_call(
        functools.partial(_gla_kernel, t_lat=t_lat, t_all=t_all, t_out=t_out),
        grid=(bsz,),
        in_specs=[pl.BlockSpec((1, t_all, COL_GLA), lambda b: (b, 0, 0)),
                  pl.BlockSpec((1, t_all, GATE_PIECES * n2), lambda b: (b, 0, 0)),
                  pl.BlockSpec((1, GLA_QK, t_all), lambda b: (b, 0, 0)),
                  pl.BlockSpec((1, GATE_PIECES * n2, t_all), lambda b: (b, 0, 0)),
                  pl.BlockSpec((1,) + totals.shape[1:], lambda b: (b, 0, 0, 0))] +
                 [_const_spec(a.shape) for a in consts] + [_const_spec(norm_t.shape)],
        out_specs=pl.BlockSpec((1, t_out, GLA_V), lambda b: (b, 0, 0)),
        out_shape=jax.ShapeDtypeStruct((bsz, t_out, GLA_V), BF16),
        scratch_shapes=[pltpu.VMEM((t_all, GLA_V), F32), pltpu.VMEM((t_all, GLA_V), F32)],
        compiler_params=_cparams("parallel"),
        name="gla",
    )(gla, lg3, kt, lgt3, totals, *consts, norm_t)


def _gmlp_kernel(z_ref, lng_ref, lnb_ref, ws_ref, bs_ref, gmask_ref, og_ref, out_ref, *, tm):
    z = z_ref[0].astype(F32)
    zf = 0.5 * z * (1.0 + lax.erf(z * (2.0 ** -0.5)))
    u = zf[:, 0:GMLP_WIDTH]
    v = zf[:, GMLP_WIDTH:]
    mu = jnp.mean(v, axis=-1, keepdims=True)
    vc = v - mu
    v = vc * lax.rsqrt(jnp.mean(vc * vc, axis=-1, keepdims=True) + 1e-5) * lng_ref[...] + lnb_ref[...]
    vb = v.astype(BF16)
    ws = ws_ref[...]
    p = GMLP_CHUNK
    for ci in range(tm // p):
        r = jnp.dot(ws, vb[ci * p:(ci + 1) * p], preferred_element_type=F32) * gmask_ref[...]
        mixed = bs_ref[...]
        for g in range(GMLP_GROUPS):
            mixed = mixed + r[g * p:(g + 1) * p]
        y = u[ci * p:(ci + 1) * p] * mixed
        out_ref[0, ci * p:(ci + 1) * p, :] = _rms(y, og_ref[...]).astype(BF16)


def _gmlp(gm, ln_g, ln_b, ws_s, bs_t, gmask, out_g, *, t_out, tm):
    bsz = gm.shape[0]
    return pl.pallas_call(
        functools.partial(_gmlp_kernel, tm=tm),
        grid=(bsz, t_out // tm),
        in_specs=[pl.BlockSpec((1, tm, COL_GM), lambda b, j: (b, j, 0)),
                  _const_spec(ln_g.shape), _const_spec(ln_b.shape), _const_spec(ws_s.shape),
                  _const_spec(bs_t.shape), _const_spec(gmask.shape), _const_spec(out_g.shape)],
        out_specs=pl.BlockSpec((1, tm, GMLP_WIDTH), lambda b, j: (b, j, 0)),
        out_shape=jax.ShapeDtypeStruct((bsz, t_out, GMLP_WIDTH), BF16),
        compiler_params=_cparams("parallel", "parallel"),
        name="gmlp",
    )(gm, ln_g, ln_b, ws_s, bs_t, gmask, out_g)


def _swa_attend(qt_ref, k_ref, vt_ref, sink_ref, og_ref, eye_ref, out_ref, s_ref, p_ref, u, key_rows, biases):
    w = SWA_WINDOW
    d = HEAD_DIM
    kt = SWA_KEY_TILE
    ph = SWA_PART_HEADS
    pw = ph * w
    parts = range(SWA_HEADS // ph)
    group = lambda part: part * ph // SWA_REP
    s_ref = s_ref.at[u]
    p_ref = p_ref.at[u]
    cols = slice(u * w, (u + 1) * w)
    keys = jnp.concatenate([k_ref[0, pl.ds(start, size), :] for start, size in key_rows], axis=0)
    nk = keys.shape[0]

    zero = jnp.zeros((d, pw), BF16)
    sinks, tops = [], []
    for part in parts:
        q = jnp.concatenate([qt_ref[0, h * d:(h + 1) * d, cols] for h in range(part * ph, (part + 1) * ph)], axis=1)
        qbd = jnp.concatenate([q if g == group(part) else zero for g in range(SWA_KV_HEADS)], axis=0)
        best, off = None, 0
        for (_, size), bias in zip(key_rows, biases):
            blk = jnp.dot(keys[off:off + size], qbd, preferred_element_type=F32)
            if bias is not None:
                blk = blk + jnp.concatenate([bias] * ph, axis=1)
            s_ref[part, off:off + size, :] = blk
            m8 = jnp.max(blk.reshape(size // 8, 8, pw), axis=0)
            best = m8 if best is None else jnp.maximum(best, m8)
            off += size
        sink = sink_ref[:, part * pw:(part + 1) * pw] * LOG2E
        sinks.append(sink)
        tops.append(jnp.maximum(jnp.max(best, axis=0, keepdims=True), sink))
    yield

    for part in parts:
        for r0 in range(0, nk, kt):
            p_ref[part, r0:r0 + kt, :] = jnp.exp2(s_ref[part, r0:r0 + kt, :] - tops[part]).astype(BF16)
    yield

    ones = jnp.ones((SWA_DEN_ROWS, nk), BF16)
    vts = [jnp.concatenate([vt_ref[0, g * d:(g + 1) * d, pl.ds(start, size)] for start, size in key_rows] , axis=1)
           for g in range(SWA_KV_HEADS)]
    heads = []
    for part in parts:
        o_ext = jnp.dot(jnp.concatenate([vts[group(part)], ones], axis=0), p_ref[part, 0:nk, :],
                        preferred_element_type=F32)
        scaled = o_ext[0:d] / (o_ext[d:d + 1] + jnp.exp2(sinks[part] - tops[part]))
        heads += [scaled[:, r * w:(r + 1) * w] for r in range(ph)]
    o_t = jnp.concatenate(heads, axis=0)
    yield

    y_t = o_t * lax.rsqrt(jnp.mean(o_t * o_t, axis=0, keepdims=True) + 1e-6) * og_ref[...]
    out = lax.dot_general(eye_ref[...], y_t.astype(BF16), (((1,), (1,)), ((), ())), preferred_element_type=F32)
    out_ref[0, cols, :] = out.astype(BF16)


def _run_stages(blocks):
    for t in range(len(blocks) + 2):
        for u in (t, t - 1, t - 2, t - 1):
            if 0 <= u < len(blocks):
                next(blocks[u], None)


def _swa_kernel(qt_ref, k_ref, vt_ref, sink_ref, og_ref, eye_ref, out_ref, s_ref, p_ref, *, t_lat, t_all, t_out, sb):
    w = SWA_WINDOW
    step = pl.program_id(1)
    n_lat = t_lat // w
    ctx_rows = (t_lat, t_all - t_lat)
    attend = functools.partial(_swa_attend, qt_ref, k_ref, vt_ref, sink_ref, og_ref, eye_ref, out_ref, s_ref, p_ref)

    def run(latent):
        blk = lambda i: (pl.multiple_of(i * w, w), w)
        sk = lax.broadcasted_iota(jnp.int32, (w, w), 0)
        tq = lax.broadcasted_iota(jnp.int32, (w, w), 1)
        neg = jnp.full((w, w), -jnp.inf, F32)
        zero = jnp.zeros((w, w), F32)
        blocks = []
        for u, is_latent in enumerate(latent):
            if not is_latent:
                blocks.append(attend(u, [ctx_rows], [None]))
                continue
            n = step * sb + u
            b_prev = jnp.where((sk >= tq) & (n >= 1), zero, neg)
            b_next = jnp.where((sk <= tq) & (n < n_lat - 1), zero, neg)
            blocks.append(attend(u, [ctx_rows, blk(jnp.maximum(n - 1, 0)), blk(n), blk(jnp.minimum(n + 1, n_lat - 1))],
                                 [None, b_prev, None, b_next]))
        _run_stages(blocks)

    n_steps = t_out // (sb * w)
    makeup = [tuple(s * sb + u < n_lat for u in range(sb)) for s in range(n_steps)]
    for kind in sorted(set(makeup), reverse=True):
        steps = [s for s in range(n_steps) if makeup[s] == kind]
        assert steps == list(range(steps[0], steps[-1] + 1))
        pl.when((step >= steps[0]) & (step <= steps[-1]))(functools.partial(run, kind))


def _swa(sqt, sk, svt, sink_t, out_g, eye, *, t_lat, t_out, sb):
    bsz, t_all, _ = sk.shape
    w = SWA_WINDOW
    n_keys = t_all - t_lat + 3 * w
    n_parts = SWA_HEADS // SWA_PART_HEADS
    assert t_out % (sb * w) == 0
    return pl.pallas_call(
        functools.partial(_swa_kernel, t_lat=t_lat, t_all=t_all, t_out=t_out, sb=sb),
        grid=(bsz, t_out // (sb * w)),
        in_specs=[pl.BlockSpec((1, SWA_Q, sb * w), lambda b, n: (b, 0, n)),
                  pl.BlockSpec((1, t_all, SWA_KV), lambda b, n: (b, 0, 0)),
                  pl.BlockSpec((1, SWA_KV, t_all), lambda b, n: (b, 0, 0)),
                  _const_spec(sink_t.shape), _const_spec(out_g.shape), _const_spec(eye.shape)],
        out_specs=pl.BlockSpec((1, sb * w, SWA_Q), lambda b, n: (b, n, 0)),
        out_shape=jax.ShapeDtypeStruct((bsz, t_out, SWA_Q), BF16),
        scratch_shapes=[pltpu.VMEM((sb, n_parts, n_keys, SWA_PART_HEADS * w), F32),
                        pltpu.VMEM((sb, n_parts, n_keys, SWA_PART_HEADS * w), BF16)],
        compiler_params=_cparams("parallel", "parallel"),
        name="swa",
    )(sqt, sk, svt, sink_t, out_g, eye)


FFN_CHUNKS = ((0, 1536), (1536, 1280))


def _post_kernel(*refs, t_lat, tm, ctx_row, n_stream, ctx_separate):
    a_ref, b_ref, c_ref, mod_ref, n1_ref, n2a_ref, n2b_ref, wo_ref, wgu_ref, wd_ref, out_ref = refs[n_stream:]
    b = pl.program_id(0)
    j = pl.program_id(1)
    x = _stream_tile(refs[:n_stream], j, tm=tm, t_lat=t_lat, ctx_separate=ctx_separate)
    mod = functools.partial(_mod_vectors, mod_ref, b, j, t_lat=t_lat, tm=tm, ctx_row=ctx_row)
    split = _row_split(t_lat, tm)
    gate1, shift2, scale2, gate2 = mod(2), mod(3), mod(4), mod(5)
    g1 = [n1_ref[...] * g for g in gate1]
    g2a = [n2a_ref[...] * (1.0 + s) for s in scale2]
    g2b = [n2b_ref[...] * g for g in gate2]

    half = tm // 2
    halves = ((0, half), (half, tm))
    pieces = sorted({0, half, split, tm})
    pieces = [(lo, hi, 0 if hi <= split else 1) for lo, hi in zip(pieces[:-1], pieces[1:])]

    def rows_of(parts, lo, hi):
        k = 0 if hi <= half else 1
        return parts[k][lo - halves[k][0]:hi - halves[k][0]]

    cat = jnp.concatenate([a_ref[0], b_ref[0], c_ref[0]], axis=-1)
    y = [jnp.dot(cat[lo:hi], wo_ref[...], preferred_element_type=F32) for lo, hi in halves]
    x1 = [x[lo:hi] + _rms(rows_of(y, lo, hi), g1[p]) for lo, hi, p in pieces]
    h = jnp.concatenate([(_rms(xp, g2a[p]) + shift2[p]).astype(BF16) for xp, (_, _, p) in zip(x1, pieces)], axis=0)
    f = None
    for i, (off, n) in enumerate(FFN_CHUNKS):
        gt = jnp.dot(h, wgu_ref[:, off:off + n], preferred_element_type=F32)
        up = jnp.dot(h, wgu_ref[:, D_FF + off:D_FF + off + n], preferred_element_type=F32)
        act = (_silu(gt) * up).astype(BF16)
        if i < len(FFN_CHUNKS) - 1:
            part = jnp.dot(act, wd_ref[off:off + n, :], preferred_element_type=F32)
            f = part if f is None else f + part
        else:
            f = [f[lo:hi] + jnp.dot(act[lo:hi], wd_ref[off:off + n, :], preferred_element_type=F32)
                 for lo, hi in halves]
    for xp, (lo, hi, p) in zip(x1, pieces):
        out_ref[0, lo:hi, :] = xp + _rms(rows_of(f, lo, hi), g2b[p])


def _post(a, bo, c, x, ctx, mod, n1_post, n2_pre, n2_post, wo, wgu, wd, *, t_lat, t_out, tm, ctx_row):
    bsz = x.shape[0]
    stream, stream_specs = _stream_specs(x, ctx, tm)
    tok = lambda n: pl.BlockSpec((1, tm, n), lambda b, j: (b, j, 0))
    once = lambda arr: pl.BlockSpec(arr.shape, lambda b, j: (0,) * arr.ndim, pipeline_mode=pl.Buffered(1))
    return pl.pallas_call(
        functools.partial(_post_kernel, t_lat=t_lat, tm=tm, ctx_row=ctx_row, n_stream=len(stream),
                          ctx_separate=ctx is not None),
        grid=(bsz, t_out // tm),
        in_specs=stream_specs + [tok(GLA_V), tok(GMLP_WIDTH), tok(SWA_Q), _const_spec(mod.shape),
                                 _const_spec(n1_post.shape), _const_spec(n2_pre.shape), _const_spec(n2_post.shape),
                                 once(wo), once(wgu), once(wd)],
        out_specs=tok(D_MODEL),
        out_shape=jax.ShapeDtypeStruct((bsz, t_out, D_MODEL), F32),
        compiler_params=_cparams("parallel", "parallel"),
        name="post",
    )(*stream, a, bo, c, mod, n1_post, n2_pre, n2_post, wo, wgu, wd)


def _rope_tables(t_lat, t_ctx):
    rows = t_lat // GRID_W
    row = jnp.repeat(jnp.arange(rows), GRID_W).astype(F32)
    col = jnp.tile(jnp.arange(GRID_W), rows).astype(F32)
    inv_freq = jnp.power(ROPE_THETA, -jnp.arange(0, ROPE_AXIS_DIM, 2, dtype=F32) / ROPE_AXIS_DIM)
    ang_row = row[:, None] * inv_freq[None, :]
    ang_col = col[:, None] * inv_freq[None, :]
    ang = jnp.concatenate([ang_row, ang_row, ang_col, ang_col], axis=-1)
    sign = jnp.tile(jnp.concatenate([-jnp.ones((16,), F32), jnp.ones((16,), F32)]), 2)
    cos = jnp.tile(jnp.concatenate([jnp.cos(ang), jnp.ones((t_ctx, HEAD_DIM), F32)], axis=0), (1, 2))
    sin = jnp.tile(jnp.concatenate([jnp.sin(ang) * sign, jnp.zeros((t_ctx, HEAD_DIM), F32)], axis=0), (1, 2))
    return cos, sin, cos.T, sin.T


def _inproj_weights(w, wa2, ba):
    r = GLA_GATE_RANK
    o_code = 2 * GLA_QK + 2 * GLA_V
    o_gm = o_code + 2 * r
    o_sq = o_gm + 2 * GMLP_WIDTH
    o_sk = o_sq + SWA_Q
    o_sv = o_sk + SWA_KV
    code = jnp.concatenate([w[:, o_code:o_gm], jnp.zeros((w.shape[0], COL_CODE - 2 * r), w.dtype)], axis=1)
    w_tok = jnp.concatenate([w[:, 0:o_code], w[:, o_gm:o_sq], w[:, o_sk:o_sv], code], axis=1)
    w_t = jnp.concatenate([w[:, o_sq:o_sk], w[:, o_sv:], w[:, GLA_QK:2 * GLA_QK]], axis=1).T
    wa = jnp.zeros((COL_CODE, 2 * GLA_QK), F32)
    wa = wa.at[0:r, 0:GLA_QK].set(wa2[0]).at[r:2 * r, GLA_QK:].set(wa2[1])
    return w_tok.astype(BF16), w_t.astype(BF16), wa.astype(BF16), ba.reshape(1, 2 * GLA_QK)


def _gla_consts():
    a3, a3t_neg, mask_w, bd_c = _gla_fast_tables()
    amat, mask = _gla_tables()
    lane_head = np.arange(GLA_QK)[None, :] // GLA_DK
    row_head = np.arange(GLA_HEADS * GLA_CHUNK)[:, None] // GLA_CHUNK
    hq = (lane_head == row_head).astype(np.float32)
    vlane_head = np.arange(GLA_V)[None, :] // GLA_DV
    bdv = (vlane_head == row_head).astype(np.float32)
    bds = (vlane_head == (np.arange(GLA_QK)[:, None] // GLA_DK)).astype(np.float32)
    ind = np.tile((vlane_head == vlane_head.T).astype(np.float32) / GLA_DV, (2, 1))
    return (jnp.asarray(a3, BF16), jnp.asarray(a3t_neg, BF16), jnp.asarray(mask_w, BF16),
            jnp.asarray(bd_c, F32), jnp.asarray(amat, BF16), jnp.asarray(mask, F32), jnp.asarray(hq, BF16),
            jnp.asarray(bdv, F32), jnp.asarray(bds, F32), jnp.asarray(ind, BF16))


def kernel(x, c, ctx, c_ctx, mod_w, mod_b, n1_pre, n1_post, n2_pre, n2_post, w_in, w_out, gla_wa2, gla_ba,
           gla_norm, gmlp_ln_g, gmlp_ln_b, gmlp_ws, gmlp_bs, gmlp_out_g, swa_sink, swa_out_g, ffn_w_gu,
           ffn_w_down):
    bsz, t_lat, d = x.shape
    t_ctx = ctx.shape[1]
    t_all = t_lat + t_ctx
    depth = mod_w.shape[0]
    assert d == D_MODEL and bsz < MOD_ROWS
    assert t_lat % 1024 == 0 and t_ctx % 256 == 0 and t_all % 768 == 0
    ctx_row = bsz
    tm_all = 768
    assert t_ctx == tm_all - _row_split(t_lat, tm_all)

    cc = jnp.zeros((MOD_ROWS, d), F32).at[0:bsz].set(c).at[ctx_row].set(c_ctx)
    mods = _modulation(cc, mod_w, mod_b)
    rope = _rope_tables(t_lat, t_ctx)
    gla_consts = _gla_consts()
    eye = jnp.eye(SWA_WINDOW, dtype=BF16)
    gmask = jnp.asarray((np.arange(GMLP_WIDTH)[None, :] // GMLP_GDIM ==
                         np.arange(GMLP_GROUPS * GMLP_CHUNK)[:, None] // GMLP_CHUNK).astype(np.float32))
    row = lambda v: v.reshape(1, -1)

    assert t_ctx == STREAM_PIECE
    xs, xs_ctx = x, ctx
    for l in range(depth):
        last = l == depth - 1
        t_out = t_lat if last else t_all
        gla, lg3, gm, sk, sqt, svt, kt, lgt3, gate_totals = _inproj(
            xs, xs_ctx, mods[l], row(n1_pre[l]), _inproj_weights(w_in[l], gla_wa2[l], gla_ba[l]), rope,
            t_lat=t_lat, t_all=t_all, tm=tm_all, ctx_row=ctx_row)
        a_out = _gla(gla, lg3, kt, lgt3, gate_totals, gla_consts, row(jnp.tile(gla_norm[l], GLA_HEADS)),
                     t_lat=t_lat, t_out=t_out)
        ws_s = gmlp_ws[l].reshape(GMLP_GROUPS * GMLP_CHUNK, GMLP_CHUNK).astype(BF16)
        bs_t = jnp.repeat(gmlp_bs[l].T, GMLP_GDIM, axis=1)
        b_out = _gmlp(gm, row(gmlp_ln_g[l]), row(gmlp_ln_b[l]), ws_s, bs_t, gmask, row(gmlp_out_g[l]),
                      t_out=t_out, tm=1024 if last else tm_all)
        sink_t = row(jnp.repeat(swa_sink[l], SWA_WINDOW))
        out_g_t = jnp.broadcast_to(swa_out_g[l][:, None], (SWA_Q, SWA_WINDOW))
        sb = next(n for n in SWA_STEP_BLOCKS if (t_out // SWA_WINDOW) % n == 0)
        c_out = _swa(sqt, sk, svt, sink_t, out_g_t, eye, t_lat=t_lat, t_out=t_out, sb=sb)
        xs = _post(a_out, b_out, c_out, xs, xs_ctx, mods[l], row(n1_post[l]), row(n2_pre[l]), row(n2_post[l]),
                   w_out[l].astype(BF16), ffn_w_gu[l].astype(BF16), ffn_w_down[l].astype(BF16),
                   t_lat=t_lat, t_out=t_out, tm=1024 if last else tm_all, ctx_row=ctx_row)
        xs_ctx = None
    return xs
```

```python
import functools

import numpy as np
import jax
import jax.numpy as jnp
from jax import lax
from jax.experimental import pallas as pl
from jax.experimental.pallas import tpu as pltpu

F32 = jnp.float32
BF16 = jnp.bfloat16

D_MODEL = 1024
GRID_W = 64
HEAD_DIM = 64
GLA_HEADS = 4
GLA_DK = 32
GLA_DV = 64
GLA_QK = GLA_HEADS * GLA_DK
GLA_V = GLA_HEADS * GLA_DV
GLA_GATE_RANK = 16
GLA_GATE_TAU = 16.0
GLA_CHUNK = 64
GMLP_GROUPS = 4
GMLP_GDIM = 64
GMLP_WIDTH = GMLP_GROUPS * GMLP_GDIM
GMLP_CHUNK = 128
SWA_HEADS = 8
SWA_KV_HEADS = 2
SWA_REP = SWA_HEADS // SWA_KV_HEADS
SWA_Q = SWA_HEADS * HEAD_DIM
SWA_KV = SWA_KV_HEADS * HEAD_DIM
SWA_WINDOW = 128
ROPE_AXIS_DIM = HEAD_DIM // 2
ROPE_THETA = 10000.0
MIX_WIDTH = GLA_V + GMLP_WIDTH + SWA_Q
D_FF = -(-8 * D_MODEL // (3 * 256)) * 256
N_MOD = 6
MOD_ROWS = 24

COL_GLA = 2 * GLA_QK + 2 * GLA_V
COL_GM = 2 * GMLP_WIDTH
COL_SK = SWA_KV
COL_CODE = 128
OFF_GM = COL_GLA
OFF_SK = OFF_GM + COL_GM
OFF_CODE = OFF_SK + COL_SK
ROW_SQ = 0
ROW_SV = ROW_SQ + SWA_Q
ROW_GK = ROW_SV + SWA_KV

LOG2E = 1.4426950408889634
SWA_Q_SCALE = HEAD_DIM ** -0.5 * LOG2E
SWA_KEY_TILE = 64
SWA_PART_HEADS = 2
SWA_STEP_BLOCKS = (8, 6, 4, 3)
SWA_DEN_ROWS = 16

VMEM_LIMIT = 56 * 1024 * 1024
GLA_LEVELS = (1, 2, 4, 8, 16, 32)


def _cparams(*sem):
    return pltpu.CompilerParams(dimension_semantics=sem, vmem_limit_bytes=VMEM_LIMIT)


def _const_spec(shape):
    nd = len(shape)
    return pl.BlockSpec(shape, lambda *_: (0,) * nd)


def _rms(x, g, eps=1e-6):
    return x * lax.rsqrt(jnp.mean(x * x, axis=-1, keepdims=True) + eps) * g


def _silu(x):
    return x * jax.nn.sigmoid(x)


def _row_split(t_lat, tm):
    return t_lat % tm if t_lat % tm else tm


def _mod_vectors(mod_ref, b, j, idx, *, t_lat, tm, ctx_row):
    lo = idx * D_MODEL
    lat = mod_ref[pl.ds(b, 1), lo:lo + D_MODEL]
    ctx = mod_ref[ctx_row:ctx_row + 1, lo:lo + D_MODEL]
    return lat, jnp.where(j * tm + _row_split(t_lat, tm) >= t_lat, ctx, lat)


STREAM_PIECE = 256


def _stream_specs(x, ctx, tm, tile=lambda b, j: (b, j), **spec_args):
    rows, d = x.shape[1:]
    per_tile = tm // STREAM_PIECE
    last = rows // STREAM_PIECE - 1

    def piece(k):
        def index(b, j):
            tb, tj = tile(b, j)
            return tb, jnp.minimum(tj * per_tile + k, last), 0
        return pl.BlockSpec((1, STREAM_PIECE, d), index, **spec_args)

    operands, specs = [x] * per_tile, [piece(k) for k in range(per_tile)]
    if ctx is not None:
        assert ctx.shape[1] == STREAM_PIECE
        operands.append(ctx)
        specs.append(pl.BlockSpec((1, STREAM_PIECE, d), lambda b, j: (tile(b, j)[0], 0, 0)))
    return operands, specs


def _stream_tile(refs, j, *, tm, t_lat, ctx_separate):
    pieces = [r[0] for r in refs[:tm // STREAM_PIECE]]
    if ctx_separate:
        pieces[-1] = jnp.where((j + 1) * tm - STREAM_PIECE >= t_lat, refs[-1][0], pieces[-1])
    return jnp.concatenate(pieces, axis=0)


def _by_rows(fn, split, *arrays):
    tm = arrays[0].shape[0]
    if split == tm:
        return fn(0, *arrays)
    return jnp.concatenate([fn(0, *[a[0:split] for a in arrays]), fn(1, *[a[split:tm] for a in arrays])], axis=0)


def _mod_kernel(c_ref, w_ref, b_ref, o_ref):
    s = _silu(c_ref[...]).astype(BF16)
    o_ref[0] = jnp.dot(s, w_ref[0].astype(BF16), preferred_element_type=F32) + b_ref[0]


def _modulation(cc, mod_w, mod_b):
    depth, d, n = mod_w.shape
    tn = n // 4
    return pl.pallas_call(
        _mod_kernel,
        grid=(depth, n // tn),
        in_specs=[pl.BlockSpec((MOD_ROWS, d), lambda l, j: (0, 0)),
                  pl.BlockSpec((1, d, tn), lambda l, j: (l, 0, j)),
                  pl.BlockSpec((1, 1, tn), lambda l, j: (l, 0, j))],
        out_specs=pl.BlockSpec((1, MOD_ROWS, tn), lambda l, j: (l, 0, j)),
        out_shape=jax.ShapeDtypeStruct((depth, MOD_ROWS, n), F32),
        compiler_params=_cparams("parallel", "parallel"),
        name="modulation",
    )(cc, mod_w, mod_b.reshape(depth, 1, n))


def _rope(z, cos, sin):
    lane = lax.broadcasted_iota(jnp.int32, z.shape, 1)
    first = (lane % 32) < 16
    rot = jnp.where(first, pltpu.roll(z, 128 - 16, 1), pltpu.roll(z, 16, 1))
    return z * cos + rot * sin


GATE_PIECES = 2
GATE_TOTAL_ROWS = 8


def _split(x):
    p1 = x.astype(BF16)
    return p1, (x - p1.astype(F32)).astype(BF16)


def _log_gate(z):
    return (jnp.minimum(z, 0.0) - jnp.log(1.0 + jnp.exp(-jnp.abs(z)))) * (1.0 / GLA_GATE_TAU)


def _rope_t(z, cos_t, sin_t):
    parts = []
    for r0 in range(0, z.shape[0], 32):
        parts += [z[r0 + 16:r0 + 32], z[r0:r0 + 16]]
    rot = jnp.concatenate(parts, axis=0)
    reps = z.shape[0] // cos_t.shape[0]
    return z * jnp.concatenate([cos_t] * reps, axis=0) + rot * jnp.concatenate([sin_t] * reps, axis=0)


def _inproj_kernel(*refs, t_lat, tm, ctx_row, n_stream, ctx_separate):
    (mod_ref, g_ref, w_ref, wt_ref, wa_ref, ba_ref, cos_ref, sin_ref, cost_ref, sint_ref,
     gla_ref, lg_ref, gm_ref, sk_ref, sqt_ref, svt_ref, kt_ref, lgt_ref, tot_ref, h_buf, ht_buf) = refs[2 * n_stream:]
    b = pl.program_id(0)
    j = pl.program_id(1)
    nt = pl.num_programs(1)
    step = b * nt + j
    slot = lax.rem(step, 2)

    def prologue(stream_refs, tb, tj, dst):
        x = _stream_tile(stream_refs, tj, tm=tm, t_lat=t_lat, ctx_separate=ctx_separate)
        mod = functools.partial(_mod_vectors, mod_ref, tb, tj, t_lat=t_lat, tm=tm, ctx_row=ctx_row)
        shift, scale, gain = mod(0), mod(1), g_ref[...]
        gains = [gain * (1.0 + s) for s in scale]
        hf = _by_rows(lambda part, x: _rms(x, gains[part]) + shift[part], _row_split(t_lat, tm), x)
        h_buf[dst] = hf.astype(BF16)
        ht_buf[dst] = hf.T.astype(BF16)

    @pl.when(step == 0)
    def _first():
        prologue(refs[n_stream:2 * n_stream], 0, 0, 0)

    def project(cur):
        def proj(off, n):
            return jnp.dot(h_buf[cur], w_ref[:, off:off + n], preferred_element_type=F32)

        def proj_t(off, n):
            return jnp.dot(wt_ref[off:off + n, :], ht_buf[cur], preferred_element_type=F32)

        code = proj(OFF_CODE, COL_CODE).astype(BF16)
        z = proj(0, COL_GLA)
        gla_ref[0, :, 0:GLA_QK] = (z[:, 0:GLA_QK] * (GLA_DK ** -0.5)).astype(BF16)
        gla_ref[0, :, GLA_QK:COL_GLA] = z[:, GLA_QK:COL_GLA].astype(BF16)

        n2 = 2 * GLA_QK
        lg = _log_gate(jnp.dot(code, wa_ref[...], preferred_element_type=F32) + ba_ref[...])
        for i, p in enumerate(_split(lg)):
            lg_ref[0, :, i * n2:(i + 1) * n2] = p
        for i, p in enumerate(_split(lg.T)):
            lgt_ref[0, i * n2:(i + 1) * n2, :] = p
        c = GLA_FAST_CHUNK
        totals = [jnp.sum(lg[i * c:(i + 1) * c], axis=0, keepdims=True) for i in range(tm // c)]
        tot_ref[0, 0] = jnp.concatenate(totals + [jnp.zeros((GATE_TOTAL_ROWS - tm // c, n2), F32)], axis=0)

        gm_ref[0] = proj(OFF_GM, COL_GM).astype(BF16)
        nb, nj = _next_tile(b, j, pl.num_programs(0), nt)
        prologue(refs[:n_stream], nb, nj, 1 - cur)
        kt_ref[0] = proj_t(ROW_GK, GLA_QK).astype(BF16)

        sk_ref[0] = _rope(proj(OFF_SK, COL_SK), cos_ref[...], sin_ref[...]).astype(BF16)
        sqt_ref[0] = (_rope_t(proj_t(ROW_SQ, SWA_Q), cost_ref[...], sint_ref[...]) * SWA_Q_SCALE).astype(BF16)
        svt_ref[0] = proj_t(ROW_SV, SWA_KV).astype(BF16)

    for cur in range(2):
        pl.when(slot == cur)(functools.partial(project, cur))


def _next_tile(b, j, nb, nt):
    wrap = j + 1 == nt
    last = jnp.logical_and(wrap, b + 1 == nb)
    return jnp.where(wrap & jnp.logical_not(last), b + 1, b), jnp.where(last, j, jnp.where(wrap, 0, j + 1))


def _inproj(x, ctx, mod, n_pre, weights, rope, *, t_lat, t_all, tm, ctx_row):
    bsz, _, d = x.shape
    nt = t_all // tm
    nxt, nxt_specs = _stream_specs(x, ctx, tm, functools.partial(_next_tile, nb=bsz, nt=nt))
    first, first_specs = _stream_specs(x, ctx, tm, lambda b, j: (0, 0))
    stream, stream_specs = nxt + first, nxt_specs + first_specs
    tok = lambda n: pl.BlockSpec((1, tm, n), lambda b, j: (b, j, 0))
    tok_t = lambda n: pl.BlockSpec((1, n, tm), lambda b, j: (b, 0, j))
    n2 = 2 * GLA_QK
    cols = (COL_GLA, GATE_PIECES * n2, COL_GM, COL_SK)
    rows = (SWA_Q, SWA_KV, GLA_QK, GATE_PIECES * n2)
    out_shape = [jax.ShapeDtypeStruct((bsz, t_all, n), BF16) for n in cols]
    out_shape += [jax.ShapeDtypeStruct((bsz, n, t_all), BF16) for n in rows]
    assert tm // GLA_FAST_CHUNK <= GATE_TOTAL_ROWS
    out_shape.append(jax.ShapeDtypeStruct((bsz, nt, GATE_TOTAL_ROWS, n2), F32))
    totals_spec = pl.BlockSpec((1, 1, GATE_TOTAL_ROWS, n2), lambda b, j: (b, j, 0, 0))
    consts = (mod, n_pre) + tuple(weights)
    cos, sin, cos_t, sin_t = rope
    return pl.pallas_call(
        functools.partial(_inproj_kernel, t_lat=t_lat, tm=tm, ctx_row=ctx_row, n_stream=len(nxt),
                          ctx_separate=ctx is not None),
        grid=(bsz, nt),
        in_specs=stream_specs + [_const_spec(a.shape) for a in consts] +
                 [pl.BlockSpec((tm, 128), lambda b, j: (j, 0)), pl.BlockSpec((tm, 128), lambda b, j: (j, 0)),
                  pl.BlockSpec((128, tm), lambda b, j: (0, j)), pl.BlockSpec((128, tm), lambda b, j: (0, j))],
        out_specs=[tok(n) for n in cols] + [tok_t(n) for n in rows] + [totals_spec],
        out_shape=out_shape,
        scratch_shapes=[pltpu.VMEM((2, tm, d), BF16), pltpu.VMEM((2, d, tm), BF16)],
        compiler_params=_cparams("arbitrary", "arbitrary"),
        name="inproj",
    )(*stream, *consts, cos, sin, cos_t, sin_t)


GLA_FAST_CHUNK = 128
GLA_FAST_UNROLL = 6
GLA_SAFE_LOG_DECAY = -40.0


def _gla_fast_tables():
    c = GLA_FAST_CHUNK
    t = np.arange(c)[:, None]
    r = np.arange(c)[None, :]
    incl = [(r <= t), (r >= t)]
    a3 = np.stack([np.tile(m, (1, GATE_PIECES)) for m in incl]).astype(np.float32)
    a3t_neg = -np.stack([np.tile(m.T, (GATE_PIECES, 1)) for m in incl]).astype(np.float32)
    mask_w = np.stack([np.tile(m, (1, GLA_HEADS)) for m in incl]).astype(np.float32)
    half = (np.arange(128)[None, :] // GLA_DV) == ((np.arange(GLA_QK)[:, None] // GLA_DK) % 2)
    return a3, a3t_neg, mask_w, half.astype(np.float32)


def _gla_fast_chunks(jobs, a3, a3t_neg, mask_w, bd_c, states):
    c = GLA_FAST_CHUNK
    dk, hd = GLA_DK, GLA_QK
    cums = [(jnp.dot(a3[d], g3, preferred_element_type=F32),
             jnp.dot(gt3, a3t_neg[d], preferred_element_type=F32))
            for d, _, _, _, g3, gt3 in jobs]

    zr = lambda n: jnp.zeros((n, c), BF16)
    qes, dcols, khs, atts = [], [], [], []
    for (d, q, kt, _, _, _), (cum, ncum_t) in zip(jobs, cums):
        qe = (q.astype(F32) * jnp.exp(cum)).astype(BF16)
        ke_t = kt.astype(F32) * jnp.exp(ncum_t)
        edge = 0 if d else c - 1
        dcol = jnp.exp(-ncum_t[:, edge:edge + 1])
        khs.append((ke_t * dcol).astype(BF16))
        ke_t = ke_t.astype(BF16)
        cols = []
        for hh in range(GLA_HEADS):
            parts = ([zr(hh * dk)] if hh else []) + [ke_t[hh * dk:(hh + 1) * dk]]
            parts += [zr(hd - (hh + 1) * dk)] if hh < GLA_HEADS - 1 else []
            cols.append(jnp.concatenate(parts, axis=0))
        kbd = jnp.concatenate(cols, axis=1)
        atts.append(jnp.dot(qe, kbd, preferred_element_type=F32))
        qes.append(qe)
        dcols.append(dcol)

    kvs = [jnp.concatenate([jnp.dot(kh[0:hd // 2], v[:, 0:128], preferred_element_type=F32),
                            jnp.dot(kh[hd // 2:hd], v[:, 128:256], preferred_element_type=F32)], axis=0)
           for (_, _, _, v, _, _), kh in zip(jobs, khs)]

    lane = lax.broadcasted_iota(jnp.int32, (1, 128), 1)
    m_lo = (lane < GLA_DV).astype(BF16)
    m_hi = (lane >= GLA_DV).astype(BF16)
    z128 = jnp.zeros((c, 128), BF16)
    z64 = jnp.zeros((hd // 2, 128), BF16)
    states = list(states)
    outs = []
    for (d, _, _, v, _, _), qe, dcol, att, kv in zip(jobs, qes, dcols, atts, kvs):
        v_lo, v_hi = v[:, 0:128], v[:, 128:256]
        vbd = jnp.concatenate([jnp.concatenate([v_lo * m_lo, z128], axis=1),
                               jnp.concatenate([v_lo * m_hi, z128], axis=1),
                               jnp.concatenate([z128, v_hi * m_lo], axis=1),
                               jnp.concatenate([z128, v_hi * m_hi], axis=1)], axis=0)
        sb = states[d].astype(BF16)
        s_full = jnp.concatenate([jnp.concatenate([sb[0:hd // 2], z64], axis=1),
                                  jnp.concatenate([z64, sb[hd // 2:hd]], axis=1)], axis=0)
        lhs = jnp.concatenate([att.astype(BF16) * mask_w[d], qe], axis=1)
        outs.append(jnp.dot(lhs, jnp.concatenate([vbd, s_full], axis=0),
                            preferred_element_type=F32))
        states[d] = states[d] * dcol + kv * bd_c
    return outs, states


def _gla_tables():
    c = GLA_CHUNK
    t = np.arange(c)[:, None]
    r = np.arange(c)[None, :]
    sizes = [2 * b for b in GLA_LEVELS]
    same = lambda b: (t // b) == (r // b)
    a_f = [same(b) & (r <= t) for b in sizes] + [same(b) & (r > t) for b in sizes]
    a_b = [same(b) & (r >= t) for b in sizes] + [same(b) & (r < t) for b in sizes]
    m_f = [((t // b) % 2 == 1) & ((r // b) == (t // b) - 1) for b in GLA_LEVELS] + [t == r]
    m_b = [((t // b) % 2 == 0) & ((r // b) == (t // b) + 1) for b in GLA_LEVELS] + [t == r]
    amat = np.stack([np.tile(np.concatenate(a, 0), (1, GATE_PIECES)) for a in (a_f, a_b)]).astype(np.float32)
    mask = np.stack([np.stack([np.tile(m, (GLA_HEADS, 1)) for m in ms]) for ms in (m_f, m_b)])
    return amat, mask.astype(np.float32)


def _gla_chunk(q, k, v, g3, amat, masks, hmask_q, bd_v, bd_s, s_prev, *, backward):
    c = GLA_CHUNK
    nl = len(GLA_LEVELS)
    g = sum(g3[i * c:(i + 1) * c].astype(F32) for i in range(GATE_PIECES))
    ps = jnp.dot(amat, g3, preferred_element_type=F32)
    blk = lambda i: ps[i * c:(i + 1) * c]
    q_exp = [g] + [blk(i) for i in range(nl - 1)]
    k_exp = [None] + [blk(nl + i) for i in range(nl - 1)]
    q_full = blk(nl - 1)
    k_full = blk(2 * nl - 1)

    att = None
    for lvl in range(nl + 1):
        if lvl < nl:
            qe = q * jnp.exp(q_exp[lvl])
            ke = k if k_exp[lvl] is None else k * jnp.exp(k_exp[lvl])
        else:
            qe, ke = q, k
        qbd = jnp.concatenate([qe.astype(BF16)] * GLA_HEADS, axis=0) * hmask_q
        p = lax.dot_general(qbd, ke.astype(BF16), (((1,), (1,)), ((), ())), preferred_element_type=F32)
        p = p * masks[lvl]
        att = p if att is None else att + p

    r = jnp.dot(att.astype(BF16), v, preferred_element_type=F32) * bd_v
    o = r[0:c]
    for hh in range(1, GLA_HEADS):
        o = o + r[hh * c:(hh + 1) * c]
    o = o + jnp.dot((q * jnp.exp(q_full)).astype(BF16), s_prev.astype(BF16), preferred_element_type=F32)

    tot = q_full[0:1] if backward else q_full[c - 1:c]
    dcol = jnp.transpose(jnp.broadcast_to(jnp.exp(tot), (GLA_QK, GLA_QK)))
    dcol = jnp.concatenate([dcol, dcol], axis=1)
    kv = lax.dot_general((k * jnp.exp(k_full)).astype(BF16), v, (((0,), (0,)), ((), ())),
                         preferred_element_type=F32)
    return o, s_prev * dcol + kv * bd_s


def _gla_kernel(gla_ref, lg_ref, kt_ref, lgt_ref, tot_ref, a3_ref, a3t_ref, mw_ref, bdc_ref, amat_ref, mask_ref,
                hq_ref, bdv_ref, bds_ref, ind_ref, norm_ref, out_ref, of_ref, ob_ref, *, t_lat, t_all, t_out):
    n2 = 2 * GLA_QK
    o_v = 2 * GLA_QK

    def pieces(rows, d):
        return jnp.concatenate([lg_ref[0, rows, i * n2 + d * GLA_QK:i * n2 + (d + 1) * GLA_QK]
                                for i in range(GATE_PIECES)], axis=0)

    safe = jnp.min(tot_ref[0]) >= GLA_SAFE_LOG_DECAY

    @pl.when(safe)
    def _fast():
        c = GLA_FAST_CHUNK
        n_lat = t_lat // c
        n_all = t_all // c
        per_step = GLA_FAST_UNROLL
        assert n_all % per_step == 0

        def job(ci, d):
            rows = pl.ds(pl.multiple_of(ci * c, c), c)
            gt3 = jnp.concatenate([lgt_ref[0, i * n2 + d * GLA_QK:i * n2 + (d + 1) * GLA_QK, rows]
                                   for i in range(GATE_PIECES)], axis=1)
            return rows, (d, gla_ref[0, rows, 0:GLA_QK], kt_ref[0, :, rows], gla_ref[0, rows, o_v:o_v + GLA_V],
                          pieces(rows, d), gt3)

        def step(i, carry):
            chunks = [(lax.rem(i * per_step + u + n_lat, n_all), 0) for u in range(per_step)]
            chunks += [(n_all - 1 - (i * per_step + u), 1) for u in range(per_step)]
            rows, jobs = zip(*[job(ci, d) for ci, d in chunks])
            outs, states = _gla_fast_chunks(jobs, a3_ref, a3t_ref, mw_ref, bdc_ref[...], carry)
            for (_, d), r, o in zip(chunks, rows, outs):
                (ob_ref if d else of_ref)[r, :] = o
            return tuple(states)

        zero = jnp.zeros((GLA_QK, 128), F32)
        lax.fori_loop(0, n_all // per_step, step, (zero, zero))

    @pl.when(jnp.logical_not(safe))
    def _robust():
        c = GLA_CHUNK
        n_lat = t_lat // c
        n_all = t_all // c
        hq = hq_ref[...]
        bdv = bdv_ref[...]
        bds = bds_ref[...]

        def one(ci, d, s):
            start = pl.multiple_of(ci * c, c)
            rows = pl.ds(start, c)
            q = gla_ref[0, rows, 0:GLA_QK].astype(F32)
            k = gla_ref[0, rows, GLA_QK:2 * GLA_QK].astype(F32)
            v = gla_ref[0, rows, o_v:o_v + GLA_V]
            o, s = _gla_chunk(q, k, v, pieces(rows, d), amat_ref[d],
                              [mask_ref[d, l] for l in range(len(GLA_LEVELS) + 1)], hq, bdv, bds, s,
                              backward=bool(d))
            (ob_ref if d else of_ref)[rows, :] = o
            return s

        def step(i, carry):
            s_f, s_b = carry
            return one(lax.rem(i + n_lat, n_all), 0, s_f), one(n_all - 1 - i, 1, s_b)

        zero = jnp.zeros((GLA_QK, GLA_V), F32)
        lax.fori_loop(0, n_all, step, (zero, zero))

    tr = 256
    ind = ind_ref[...]
    for r0 in range(0, t_out, tr):
        o = of_ref[r0:r0 + tr, :] + ob_ref[r0:r0 + tr, :]
        sq = o * o
        hi = sq.astype(BF16)
        lo = (sq - hi.astype(F32)).astype(BF16)
        ms = jnp.dot(jnp.concatenate([hi, lo], axis=1), ind, preferred_element_type=F32)
        gate = gla_ref[0, r0:r0 + tr, o_v + GLA_V:COL_GLA].astype(F32)
        y = o * lax.rsqrt(ms + 1e-6) * norm_ref[...] * _silu(gate)
        out_ref[0, r0:r0 + tr, :] = y.astype(BF16)


def _gla(gla, lg3, kt, lgt3, totals, consts, norm_t, *, t_lat, t_out):
    bsz, t_all, _ = gla.shape
    n2 = 2 * GLA_QK
    return pl.pallas_call(
        functools.partial(_gla_kernel, t_lat=t_lat, t_all=t_all, t_out=t_out),
        grid=(bsz,),
        in_specs=[pl.BlockSpec((1, t_all, COL_GLA), lambda b: (b, 0, 0)),
                  pl.BlockSpec((1, t_all, GATE_PIECES * n2), lambda b: (b, 0, 0)),
                  pl.BlockSpec((1, GLA_QK, t_all), lambda b: (b, 0, 0)),
                  pl.BlockSpec((1, GATE_PIECES * n2, t_all), lambda b: (b, 0, 0)),
                  pl.BlockSpec((1,) + totals.shape[1:], lambda b: (b, 0, 0, 0))] +
                 [_const_spec(a.shape) for a in consts] + [_const_spec(norm_t.shape)],
        out_specs=pl.BlockSpec((1, t_out, GLA_V), lambda b: (b, 0, 0)),
        out_shape=jax.ShapeDtypeStruct((bsz, t_out, GLA_V), BF16),
        scratch_shapes=[pltpu.VMEM((t_all, GLA_V), F32), pltpu.VMEM((t_all, GLA_V), F32)],
        compiler_params=_cparams("parallel"),
        name="gla",
    )(gla, lg3, kt, lgt3, totals, *consts, norm_t)


def _gmlp_kernel(z_ref, lng_ref, lnb_ref, ws_ref, bs_ref, gmask_ref, og_ref, out_ref, *, tm):
    z = z_ref[0].astype(F32)
    zf = 0.5 * z * (1.0 + lax.erf(z * (2.0 ** -0.5)))
    u = zf[:, 0:GMLP_WIDTH]
    v = zf[:, GMLP_WIDTH:]
    mu = jnp.mean(v, axis=-1, keepdims=True)
    vc = v - mu
    v = vc * lax.rsqrt(jnp.mean(vc * vc, axis=-1, keepdims=True) + 1e-5) * lng_ref[...] + lnb_ref[...]
    vb = v.astype(BF16)
    ws = ws_ref[...]
    p = GMLP_CHUNK
    for ci in range(tm // p):
        r = jnp.dot(ws, vb[ci * p:(ci + 1) * p], preferred_element_type=F32) * gmask_ref[...]
        mixed = bs_ref[...]
        for g in range(GMLP_GROUPS):
            mixed = mixed + r[g * p:(g + 1) * p]
        y = u[ci * p:(ci + 1) * p] * mixed
        out_ref[0, ci * p:(ci + 1) * p, :] = _rms(y, og_ref[...]).astype(BF16)


def _gmlp(gm, ln_g, ln_b, ws_s, bs_t, gmask, out_g, *, t_out, tm):
    bsz = gm.shape[0]
    return pl.pallas_call(
        functools.partial(_gmlp_kernel, tm=tm),
        grid=(bsz, t_out // tm),
        in_specs=[pl.BlockSpec((1, tm, COL_GM), lambda b, j: (b, j, 0)),
                  _const_spec(ln_g.shape), _const_spec(ln_b.shape), _const_spec(ws_s.shape),
                  _const_spec(bs_t.shape), _const_spec(gmask.shape), _const_spec(out_g.shape)],
        out_specs=pl.BlockSpec((1, tm, GMLP_WIDTH), lambda b, j: (b, j, 0)),
        out_shape=jax.ShapeDtypeStruct((bsz, t_out, GMLP_WIDTH), BF16),
        compiler_params=_cparams("parallel", "parallel"),
        name="gmlp",
    )(gm, ln_g, ln_b, ws_s, bs_t, gmask, out_g)


def _swa_attend(qt_ref, k_ref, vt_ref, sink_ref, og_ref, eye_ref, out_ref, s_ref, p_ref, u, key_rows, biases):
    w = SWA_WINDOW
    d = HEAD_DIM
    kt = SWA_KEY_TILE
    ph = SWA_PART_HEADS
    pw = ph * w
    parts = range(SWA_HEADS // ph)
    group = lambda part: part * ph // SWA_REP
    s_ref = s_ref.at[u]
    p_ref = p_ref.at[u]
    cols = slice(u * w, (u + 1) * w)
    keys = jnp.concatenate([k_ref[0, pl.ds(start, size), :] for start, size in key_rows], axis=0)
    nk = keys.shape[0]

    zero = jnp.zeros((d, pw), BF16)
    sinks, tops = [], []
    for part in parts:
        q = jnp.concatenate([qt_ref[0, h * d:(h + 1) * d, cols] for h in range(part * ph, (part + 1) * ph)], axis=1)
        qbd = jnp.concatenate([q if g == group(part) else zero for g in range(SWA_KV_HEADS)], axis=0)
        best, off = None, 0
        for (_, size), bias in zip(key_rows, biases):
            blk = jnp.dot(keys[off:off + size], qbd, preferred_element_type=F32)
            if bias is not None:
                blk = blk + jnp.concatenate([bias] * ph, axis=1)
            s_ref[part, off:off + size, :] = blk
            m8 = jnp.max(blk.reshape(size // 8, 8, pw), axis=0)
            best = m8 if best is None else jnp.maximum(best, m8)
            off += size
        sink = sink_ref[:, part * pw:(part + 1) * pw] * LOG2E
        sinks.append(sink)
        tops.append(jnp.maximum(jnp.max(best, axis=0, keepdims=True), sink))
    yield

    for part in parts:
        for r0 in range(0, nk, kt):
            p_ref[part, r0:r0 + kt, :] = jnp.exp2(s_ref[part, r0:r0 + kt, :] - tops[part]).astype(BF16)
    yield

    ones = jnp.ones((SWA_DEN_ROWS, nk), BF16)
    vts = [jnp.concatenate([vt_ref[0, g * d:(g + 1) * d, pl.ds(start, size)] for start, size in key_rows] , axis=1)
           for g in range(SWA_KV_HEADS)]
    heads = []
    for part in parts:
        o_ext = jnp.dot(jnp.concatenate([vts[group(part)], ones], axis=0), p_ref[part, 0:nk, :],
                        preferred_element_type=F32)
        scaled = o_ext[0:d] / (o_ext[d:d + 1] + jnp.exp2(sinks[part] - tops[part]))
        heads += [scaled[:, r * w:(r + 1) * w] for r in range(ph)]
    o_t = jnp.concatenate(heads, axis=0)
    yield

    y_t = o_t * lax.rsqrt(jnp.mean(o_t * o_t, axis=0, keepdims=True) + 1e-6) * og_ref[...]
    out = lax.dot_general(eye_ref[...], y_t.astype(BF16), (((1,), (1,)), ((), ())), preferred_element_type=F32)
    out_ref[0, cols, :] = out.astype(BF16)


def _run_stages(blocks):
    for t in range(len(blocks) + 2):
        for u in (t, t - 1, t - 2, t - 1):
            if 0 <= u < len(blocks):
                next(blocks[u], None)


def _swa_kernel(qt_ref, k_ref, vt_ref, sink_ref, og_ref, eye_ref, out_ref, s_ref, p_ref, *, t_lat, t_all, t_out, sb):
    w = SWA_WINDOW
    step = pl.program_id(1)
    n_lat = t_lat // w
    ctx_rows = (t_lat, t_all - t_lat)
    attend = functools.partial(_swa_attend, qt_ref, k_ref, vt_ref, sink_ref, og_ref, eye_ref, out_ref, s_ref, p_ref)

    def run(latent):
        blk = lambda i: (pl.multiple_of(i * w, w), w)
        sk = lax.broadcasted_iota(jnp.int32, (w, w), 0)
        tq = lax.broadcasted_iota(jnp.int32, (w, w), 1)
        neg = jnp.full((w, w), -jnp.inf, F32)
        zero = jnp.zeros((w, w), F32)
        blocks = []
        for u, is_latent in enumerate(latent):
            if not is_latent:
                blocks.append(attend(u, [ctx_rows], [None]))
                continue
            n = step * sb + u
            b_prev = jnp.where((sk >= tq) & (n >= 1), zero, neg)
            b_next = jnp.where((sk <= tq) & (n < n_lat - 1), zero, neg)
            blocks.append(attend(u, [ctx_rows, blk(jnp.maximum(n - 1, 0)), blk(n), blk(jnp.minimum(n + 1, n_lat - 1))],
                                 [None, b_prev, None, b_next]))
        _run_stages(blocks)

    n_steps = t_out // (sb * w)
    makeup = [tuple(s * sb + u < n_lat for u in range(sb)) for s in range(n_steps)]
    for kind in sorted(set(makeup), reverse=True):
        steps = [s for s in range(n_steps) if makeup[s] == kind]
        assert steps == list(range(steps[0], steps[-1] + 1))
        pl.when((step >= steps[0]) & (step <= steps[-1]))(functools.partial(run, kind))


def _swa(sqt, sk, svt, sink_t, out_g, eye, *, t_lat, t_out, sb):
    bsz, t_all, _ = sk.shape
    w = SWA_WINDOW
    n_keys = t_all - t_lat + 3 * w
    n_parts = SWA_HEADS // SWA_PART_HEADS
    assert t_out % (sb * w) == 0
    return pl.pallas_call(
        functools.partial(_swa_kernel, t_lat=t_lat, t_all=t_all, t_out=t_out, sb=sb),
        grid=(bsz, t_out // (sb * w)),
        in_specs=[pl.BlockSpec((1, SWA_Q, sb * w), lambda b, n: (b, 0, n)),
                  pl.BlockSpec((1, t_all, SWA_KV), lambda b, n: (b, 0, 0)),
                  pl.BlockSpec((1, SWA_KV, t_all), lambda b, n: (b, 0, 0)),
                  _const_spec(sink_t.shape), _const_spec(out_g.shape), _const_spec(eye.shape)],
        out_specs=pl.BlockSpec((1, sb * w, SWA_Q), lambda b, n: (b, n, 0)),
        out_shape=jax.ShapeDtypeStruct((bsz, t_out, SWA_Q), BF16),
        scratch_shapes=[pltpu.VMEM((sb, n_parts, n_keys, SWA_PART_HEADS * w), F32),
                        pltpu.VMEM((sb, n_parts, n_keys, SWA_PART_HEADS * w), BF16)],
        compiler_params=_cparams("parallel", "parallel"),
        name="swa",
    )(sqt, sk, svt, sink_t, out_g, eye)


FFN_CHUNKS = ((0, 1536), (1536, 1280))


def _post_kernel(*refs, t_lat, tm, ctx_row, n_stream, ctx_separate):
    a_ref, b_ref, c_ref, mod_ref, n1_ref, n2a_ref, n2b_ref, wo_ref, wgu_ref, wd_ref, out_ref = refs[n_stream:]
    b = pl.program_id(0)
    j = pl.program_id(1)
    x = _stream_tile(refs[:n_stream], j, tm=tm, t_lat=t_lat, ctx_separate=ctx_separate)
    mod = functools.partial(_mod_vectors, mod_ref, b, j, t_lat=t_lat, tm=tm, ctx_row=ctx_row)
    split = _row_split(t_lat, tm)
    gate1, shift2, scale2, gate2 = mod(2), mod(3), mod(4), mod(5)
    g1 = [n1_ref[...] * g for g in gate1]
    g2a = [n2a_ref[...] * (1.0 + s) for s in scale2]
    g2b = [n2b_ref[...] * g for g in gate2]

    half = tm // 2
    halves = ((0, half), (half, tm))
    pieces = sorted({0, half, split, tm})
    pieces = [(lo, hi, 0 if hi <= split else 1) for lo, hi in zip(pieces[:-1], pieces[1:])]

    def rows_of(parts, lo, hi):
        k = 0 if hi <= half else 1
        return parts[k][lo - halves[k][0]:hi - halves[k][0]]

    cat = jnp.concatenate([a_ref[0], b_ref[0], c_ref[0]], axis=-1)
    y = [jnp.dot(cat[lo:hi], wo_ref[...], preferred_element_type=F32) for lo, hi in halves]
    x1 = [x[lo:hi] + _rms(rows_of(y, lo, hi), g1[p]) for lo, hi, p in pieces]
    h_pieces = [(_rms(xp, g2a[p]) + shift2[p]).astype(BF16) for xp, (_, _, p) in zip(x1, pieces)]
    h_half = [jnp.concatenate([hp for hp, (lo, hi, _) in zip(h_pieces, pieces) if a <= lo and hi <= b], axis=0)
              for a, b in halves]
    h = jnp.concatenate(h_half, axis=0)
    f = None
    for i, (off, n) in enumerate(FFN_CHUNKS):
        if i == 0:
            per_half = [[jnp.dot(hh, wgu_ref[:, o:o + n], preferred_element_type=F32) for o in (off, D_FF + off)]
                        for hh in h_half]
            gt, up = [jnp.concatenate([both[k] for both in per_half], axis=0) for k in range(2)]
        else:
            gt = jnp.dot(h, wgu_ref[:, off:off + n], preferred_element_type=F32)
            up = jnp.dot(h, wgu_ref[:, D_FF + off:D_FF + off + n], preferred_element_type=F32)
        act = (_silu(gt) * up).astype(BF16)
        if i < len(FFN_CHUNKS) - 1:
            part = jnp.dot(act, wd_ref[off:off + n, :], preferred_element_type=F32)
            f = part if f is None else f + part
        else:
            f = [f[lo:hi] + jnp.dot(act[lo:hi], wd_ref[off:off + n, :], preferred_element_type=F32)
                 for lo, hi in halves]
    for xp, (lo, hi, p) in zip(x1, pieces):
        out_ref[0, lo:hi, :] = xp + _rms(rows_of(f, lo, hi), g2b[p])


def _post(a, bo, c, x, ctx, mod, n1_post, n2_pre, n2_post, wo, wgu, wd, *, t_lat, t_out, tm, ctx_row):
    bsz = x.shape[0]
    stream, stream_specs = _stream_specs(x, ctx, tm)
    tok = lambda n: pl.BlockSpec((1, tm, n), lambda b, j: (b, j, 0))
    once = lambda arr: pl.BlockSpec(arr.shape, lambda b, j: (0,) * arr.ndim, pipeline_mode=pl.Buffered(1))
    return pl.pallas_call(
        functools.partial(_post_kernel, t_lat=t_lat, tm=tm, ctx_row=ctx_row, n_stream=len(stream),
                          ctx_separate=ctx is not None),
        grid=(bsz, t_out // tm),
        in_specs=stream_specs + [tok(GLA_V), tok(GMLP_WIDTH), tok(SWA_Q), _const_spec(mod.shape),
                                 _const_spec(n1_post.shape), _const_spec(n2_pre.shape), _const_spec(n2_post.shape),
                                 once(wo), once(wgu), once(wd)],
        out_specs=tok(D_MODEL),
        out_shape=jax.ShapeDtypeStruct((bsz, t_out, D_MODEL), F32),
        compiler_params=_cparams("parallel", "parallel"),
        name="post",
    )(*stream, a, bo, c, mod, n1_post, n2_pre, n2_post, wo, wgu, wd)


def _rope_tables(t_lat, t_ctx):
    rows = t_lat // GRID_W
    row = jnp.repeat(jnp.arange(rows), GRID_W).astype(F32)
    col = jnp.tile(jnp.arange(GRID_W), rows).astype(F32)
    inv_freq = jnp.power(ROPE_THETA, -jnp.arange(0, ROPE_AXIS_DIM, 2, dtype=F32) / ROPE_AXIS_DIM)
    ang_row = row[:, None] * inv_freq[None, :]
    ang_col = col[:, None] * inv_freq[None, :]
    ang = jnp.concatenate([ang_row, ang_row, ang_col, ang_col], axis=-1)
    sign = jnp.tile(jnp.concatenate([-jnp.ones((16,), F32), jnp.ones((16,), F32)]), 2)
    cos = jnp.tile(jnp.concatenate([jnp.cos(ang), jnp.ones((t_ctx, HEAD_DIM), F32)], axis=0), (1, 2))
    sin = jnp.tile(jnp.concatenate([jnp.sin(ang) * sign, jnp.zeros((t_ctx, HEAD_DIM), F32)], axis=0), (1, 2))
    return cos, sin, cos.T, sin.T


def _inproj_weights(w, wa2, ba):
    r = GLA_GATE_RANK
    o_code = 2 * GLA_QK + 2 * GLA_V
    o_gm = o_code + 2 * r
    o_sq = o_gm + 2 * GMLP_WIDTH
    o_sk = o_sq + SWA_Q
    o_sv = o_sk + SWA_KV
    code = jnp.concatenate([w[:, o_code:o_gm], jnp.zeros((w.shape[0], COL_CODE - 2 * r), w.dtype)], axis=1)
    w_tok = jnp.concatenate([w[:, 0:o_code], w[:, o_gm:o_sq], w[:, o_sk:o_sv], code], axis=1)
    w_t = jnp.concatenate([w[:, o_sq:o_sk], w[:, o_sv:], w[:, GLA_QK:2 * GLA_QK]], axis=1).T
    wa = jnp.zeros((COL_CODE, 2 * GLA_QK), F32)
    wa = wa.at[0:r, 0:GLA_QK].set(wa2[0]).at[r:2 * r, GLA_QK:].set(wa2[1])
    return w_tok.astype(BF16), w_t.astype(BF16), wa.astype(BF16), ba.reshape(1, 2 * GLA_QK)


def _gla_consts():
    a3, a3t_neg, mask_w, bd_c = _gla_fast_tables()
    amat, mask = _gla_tables()
    lane_head = np.arange(GLA_QK)[None, :] // GLA_DK
    row_head = np.arange(GLA_HEADS * GLA_CHUNK)[:, None] // GLA_CHUNK
    hq = (lane_head == row_head).astype(np.float32)
    vlane_head = np.arange(GLA_V)[None, :] // GLA_DV
    bdv = (vlane_head == row_head).astype(np.float32)
    bds = (vlane_head == (np.arange(GLA_QK)[:, None] // GLA_DK)).astype(np.float32)
    ind = np.tile((vlane_head == vlane_head.T).astype(np.float32) / GLA_DV, (2, 1))
    return (jnp.asarray(a3, BF16), jnp.asarray(a3t_neg, BF16), jnp.asarray(mask_w, BF16),
            jnp.asarray(bd_c, F32), jnp.asarray(amat, BF16), jnp.asarray(mask, F32), jnp.asarray(hq, BF16),
            jnp.asarray(bdv, F32), jnp.asarray(bds, F32), jnp.asarray(ind, BF16))


def kernel(x, c, ctx, c_ctx, mod_w, mod_b, n1_pre, n1_post, n2_pre, n2_post, w_in, w_out, gla_wa2, gla_ba,
           gla_norm, gmlp_ln_g, gmlp_ln_b, gmlp_ws, gmlp_bs, gmlp_out_g, swa_sink, swa_out_g, ffn_w_gu,
           ffn_w_down):
    bsz, t_lat, d = x.shape
    t_ctx = ctx.shape[1]
    t_all = t_lat + t_ctx
    depth = mod_w.shape[0]
    assert d == D_MODEL and bsz < MOD_ROWS
    assert t_lat % 1024 == 0 and t_ctx % 256 == 0 and t_all % 768 == 0
    ctx_row = bsz
    tm_all = 768
    assert t_ctx == tm_all - _row_split(t_lat, tm_all)

    cc = jnp.zeros((MOD_ROWS, d), F32).at[0:bsz].set(c).at[ctx_row].set(c_ctx)
    mods = _modulation(cc, mod_w, mod_b)
    rope = _rope_tables(t_lat, t_ctx)
    gla_consts = _gla_consts()
    eye = jnp.eye(SWA_WINDOW, dtype=BF16)
    gmask = jnp.asarray((np.arange(GMLP_WIDTH)[None, :] // GMLP_GDIM ==
                         np.arange(GMLP_GROUPS * GMLP_CHUNK)[:, None] // GMLP_CHUNK).astype(np.float32))
    row = lambda v: v.reshape(1, -1)

    assert t_ctx == STREAM_PIECE
    xs, xs_ctx = x, ctx
    for l in range(depth):
        last = l == depth - 1
        t_out = t_lat if last else t_all
        gla, lg3, gm, sk, sqt, svt, kt, lgt3, gate_totals = _inproj(
            xs, xs_ctx, mods[l], row(n1_pre[l]), _inproj_weights(w_in[l], gla_wa2[l], gla_ba[l]), rope,
            t_lat=t_lat, t_all=t_all, tm=tm_all, ctx_row=ctx_row)
        a_out = _gla(gla, lg3, kt, lgt3, gate_totals, gla_consts, row(jnp.tile(gla_norm[l], GLA_HEADS)),
                     t_lat=t_lat, t_out=t_out)
        ws_s = gmlp_ws[l].reshape(GMLP_GROUPS * GMLP_CHUNK, GMLP_CHUNK).astype(BF16)
        bs_t = jnp.repeat(gmlp_bs[l].T, GMLP_GDIM, axis=1)
        b_out = _gmlp(gm, row(gmlp_ln_g[l]), row(gmlp_ln_b[l]), ws_s, bs_t, gmask, row(gmlp_out_g[l]),
                      t_out=t_out, tm=1024 if last else tm_all)
        sink_t = row(jnp.repeat(swa_sink[l], SWA_WINDOW))
        out_g_t = jnp.broadcast_to(swa_out_g[l][:, None], (SWA_Q, SWA_WINDOW))
        sb = next(n for n in SWA_STEP_BLOCKS if (t_out // SWA_WINDOW) % n == 0)
        c_out = _swa(sqt, sk, svt, sink_t, out_g_t, eye, t_lat=t_lat, t_out=t_out, sb=sb)
        xs = _post(a_out, b_out, c_out, xs, xs_ctx, mods[l], row(n1_post[l]), row(n2_pre[l]), row(n2_post[l]),
                   w_out[l].astype(BF16), ffn_w_gu[l].astype(BF16), ffn_w_down[l].astype(BF16),
                   t_lat=t_lat, t_out=t_out, tm=1024 if last else tm_all, ctx_row=ctx_row)
        xs_ctx = None
    return xs
```
